```python
import math
import jax, jax.numpy as jnp
from jax import lax
import numpy as np

D_MODEL = 2048
BATCH = 2
SEQ = 4096
DEPTH = 2
DEC_BATCH = 16
DEC_SEQ = 32
PAST_LEN = 2048

CHUNK = 64
Q_BLOCK = 128
MLA_HEADS = 16
Q_LORA = 512
KV_LORA = 512
NOPE_DIM = 128
ROPE_DIM = 64
V_DIM = 128
ROPE_THETA = 10000.0
MLA_SCALE = (NOPE_DIM + ROPE_DIM) ** -0.5
SB_HEADS = 8
SB_DIM = 128
SB_SCALE = SB_DIM ** -0.5
FF_DIM = -(-8 * D_MODEL // (3 * 256)) * 256
PLE_DIM = 256
ALPHA = (2 * DEPTH) ** 0.25
BETA = (8 * DEPTH) ** -0.25
LN_EPS = 1e-5
RMS_EPS = 1e-6
SB_WIDTH = SB_HEADS * SB_DIM
IN_SIZES = (Q_LORA, KV_LORA, ROPE_DIM, SB_WIDTH, SB_WIDTH, SB_WIDTH, 2 * D_MODEL)
IN_COLS = Q_LORA + KV_LORA + ROPE_DIM + 3 * SB_WIDTH + 2 * D_MODEL
IN_SPLITS = [Q_LORA, Q_LORA + KV_LORA, Q_LORA + KV_LORA + ROPE_DIM,
             Q_LORA + KV_LORA + ROPE_DIM + SB_WIDTH,
             Q_LORA + KV_LORA + ROPE_DIM + 2 * SB_WIDTH,
             Q_LORA + KV_LORA + ROPE_DIM + 3 * SB_WIDTH]

kernel_name = "hybrid_mla_stickbreaking_streaming_step"


def layer_norm(x, g, b):
    xf = x.astype(jnp.float32)
    mu = jnp.mean(xf, -1, keepdims=True)
    var = jnp.mean(jnp.square(xf - mu), -1, keepdims=True)
    return ((xf - mu) * lax.rsqrt(var + LN_EPS) * g + b).astype(x.dtype)


def rms_norm(x, g):
    xf = x.astype(jnp.float32)
    return (xf * lax.rsqrt(jnp.mean(jnp.square(xf), -1, keepdims=True) + RMS_EPS) * g).astype(x.dtype)


def rope(x, pos):
    half = ROPE_DIM // 2
    inv_freq = 1.0 / (ROPE_THETA ** (jnp.arange(half, dtype=jnp.float32) * (2.0 / ROPE_DIM)))
    ang = pos.astype(jnp.float32)[:, None] * inv_freq[None, :]
    ang = ang.reshape(ang.shape[0], *([1] * (x.ndim - 3)), half)
    cos, sin = jnp.cos(ang), jnp.sin(ang)
    x1, x2 = x[..., :half], x[..., half:]
    return jnp.concatenate([x1 * cos - x2 * sin, x2 * cos + x1 * sin], -1).astype(x.dtype)


def over_query_blocks(attend, qs, q_pos):
    n_q = q_pos.shape[0]
    if n_q <= Q_BLOCK:
        return attend(qs, q_pos)
    nb = n_q // Q_BLOCK

    def to_blocks(a):
        return jnp.moveaxis(a.reshape(a.shape[0], nb, Q_BLOCK, *a.shape[2:]), 1, 0)

    out = lax.map(lambda args: attend(args[0], args[1]),
                  (tuple(to_blocks(q) for q in qs), q_pos.reshape(nb, Q_BLOCK)))
    out = jnp.moveaxis(out, 0, 1)
    return out.reshape(out.shape[0], n_q, *out.shape[3:])


def mla_attend(q_lat, q_rope, q_pos, lat, k_rope, k_pos, w_uv):
    s = (jnp.einsum('bqhc,bkc->bhqk', q_lat, lat)
         + jnp.einsum('bqhr,bkr->bhqk', q_rope, k_rope)).astype(jnp.float32) * MLA_SCALE
    visible = (k_pos[None, :] // CHUNK) <= (q_pos[:, None] // CHUNK)
    s = jnp.where(visible, s, -jnp.inf)
    p = jax.nn.softmax(s, axis=-1).astype(lat.dtype)
    ctx = jnp.einsum('bhqk,bkc->bqhc', p, lat)
    return jnp.einsum('bqhc,chv->bqhv', ctx, w_uv)


def sb_attend(q, q_pos, k, v, k_pos):
    z = jnp.einsum('bqhd,bkhd->bhqk', q, k).astype(jnp.float32) * SB_SCALE
    before = k_pos[None, :] < q_pos[:, None]
    log_beta = jax.nn.log_sigmoid(z)
    log_rest = jnp.where(before, jax.nn.log_sigmoid(-z), 0.0)
    tail = lax.cumsum(log_rest, axis=log_rest.ndim - 1, reverse=True)
    tail = jnp.concatenate([tail[..., 1:], jnp.zeros_like(tail[..., :1])], -1)
    a = jnp.where(before, jnp.exp(log_beta + tail), 0.0).astype(v.dtype)
    return jnp.einsum('bhqk,bkhd->bqhd', a, v)


def trunk_layer(x, p, past_lat, past_kr, past_k, past_v,
                w_in, b_gate, q_a_norm_g, w_q_b, kv_a_norm_g, w_kv_b,
                w_branch_a, w_branch_b, w_out, ln1_g, ln1_b,
                w_ffn_gu, w_ffn_down, ln2_g, ln2_b,
                w_ple_gate, b_ple_gate, w_ple_proj, ln3_g, ln3_b):
    b, s, _ = x.shape
    n_past = past_lat.shape[1]
    q_pos = n_past + jnp.arange(s, dtype=jnp.int32)
    k_pos = jnp.arange(n_past + s, dtype=jnp.int32)

    h = x @ w_in
    q_a, kv_a, k_r, sb_q, sb_k, sb_v, gates = jnp.split(h, IN_SPLITS, axis=-1)

    q = (rms_norm(q_a, q_a_norm_g) @ w_q_b).reshape(b, s, MLA_HEADS, NOPE_DIM + ROPE_DIM)
    q_rope = rope(q[..., NOPE_DIM:], q_pos)
    w_kv = w_kv_b.reshape(KV_LORA, MLA_HEADS, NOPE_DIM + V_DIM)
    w_uk, w_uv = w_kv[..., :NOPE_DIM], w_kv[..., NOPE_DIM:]
    q_lat = jnp.einsum('bshn,chn->bshc', q[..., :NOPE_DIM], w_uk)
    new_lat = rms_norm(kv_a, kv_a_norm_g)
    new_kr = rope(k_r, q_pos)
    lat_all = jnp.concatenate([past_lat, new_lat], 1)
    kr_all = jnp.concatenate([past_kr, new_kr], 1)
    o_a = over_query_blocks(
        lambda qs, qp: mla_attend(qs[0], qs[1], qp, lat_all, kr_all, k_pos, w_uv),
        (q_lat, q_rope), q_pos)

    sbq = sb_q.reshape(b, s, SB_HEADS, SB_DIM)
    new_k = sb_k.reshape(b, s, SB_HEADS, SB_DIM)
    new_v = sb_v.reshape(b, s, SB_HEADS, SB_DIM)
    k_all = jnp.concatenate([past_k, new_k], 1)
    v_all = jnp.concatenate([past_v, new_v], 1)
    o_b = over_query_blocks(
        lambda qs, qp: sb_attend(qs[0], qp, k_all, v_all, k_pos), (sbq,), q_pos)

    g_a, g_b = jnp.split(jax.nn.sigmoid(gates + b_gate), 2, axis=-1)
    merged = (g_a * (o_a.reshape(b, s, -1) @ w_branch_a)
              + g_b * (o_b.reshape(b, s, -1) @ w_branch_b))
    x = layer_norm(ALPHA * x + merged @ w_out, ln1_g, ln1_b)

    f_g, f_u = jnp.split(x @ w_ffn_gu, 2, axis=-1)
    x = layer_norm(ALPHA * x + (jax.nn.silu(f_g) * f_u) @ w_ffn_down, ln2_g, ln2_b)

    ple = jax.nn.sigmoid(x @ w_ple_gate + b_ple_gate) * (p @ w_ple_proj)
    x = layer_norm(ALPHA * x + ple, ln3_g, ln3_b)
    return x, (new_lat, new_kr, new_k, new_v)


def setup_inputs(seed: int = 0) -> dict:
    key = jax.random.key(seed)
    ks = iter(jax.random.split(key, 40))

    def nrm(shape, scale=1.0):
        return jax.random.normal(next(ks), shape, jnp.float32) * scale

    def gain(shape):
        return 1.0 + nrm(shape, 0.02)

    L = DEPTH
    return {
        "x_prompt": nrm((BATCH, SEQ, D_MODEL)),
        "x_sample": nrm((DEC_BATCH, DEC_SEQ, D_MODEL)),
        "cache_mla_latent": nrm((L, DEC_BATCH, PAST_LEN, KV_LORA)),
        "cache_mla_krope": nrm((L, DEC_BATCH, PAST_LEN, ROPE_DIM)),
        "cache_sb_k": nrm((L, DEC_BATCH, PAST_LEN, SB_HEADS, SB_DIM)),
        "cache_sb_v": nrm((L, DEC_BATCH, PAST_LEN, SB_HEADS, SB_DIM)),
        "p_prompt": nrm((L, BATCH, SEQ, PLE_DIM)),
        "p_sample": nrm((L, DEC_BATCH, DEC_SEQ, PLE_DIM)),
        "ln_in_g": gain((D_MODEL,)),
        "ln_in_b": nrm((D_MODEL,), 0.02),
        "w_in": nrm((L, D_MODEL, IN_COLS), D_MODEL ** -0.5),
        "b_gate": nrm((L, 2 * D_MODEL), 0.02),
        "q_a_norm_g": gain((L, Q_LORA)),
        "w_q_b": nrm((L, Q_LORA, MLA_HEADS * (NOPE_DIM + ROPE_DIM)), Q_LORA ** -0.5),
        "kv_a_norm_g": gain((L, KV_LORA)),
        "w_kv_b": nrm((L, KV_LORA, MLA_HEADS * (NOPE_DIM + V_DIM)), KV_LORA ** -0.5),
        "w_branch_a": nrm((L, MLA_HEADS * V_DIM, D_MODEL), (MLA_HEADS * V_DIM) ** -0.5),
        "w_branch_b": nrm((L, SB_WIDTH, D_MODEL), SB_WIDTH ** -0.5),
        "w_out": nrm((L, D_MODEL, D_MODEL), BETA * D_MODEL ** -0.5),
        "ln1_g": gain((L, D_MODEL)),
        "ln1_b": nrm((L, D_MODEL), 0.02),
        "w_ffn_gu": nrm((L, D_MODEL, 2 * FF_DIM), D_MODEL ** -0.5),
        "w_ffn_down": nrm((L, FF_DIM, D_MODEL), BETA * FF_DIM ** -0.5),
        "ln2_g": gain((L, D_MODEL)),
        "ln2_b": nrm((L, D_MODEL), 0.02),
        "w_ple_gate": nrm((L, D_MODEL, D_MODEL), D_MODEL ** -0.5),
        "b_ple_gate": nrm((L, D_MODEL), 0.02),
        "w_ple_proj": nrm((L, PLE_DIM, D_MODEL), BETA * PLE_DIM ** -0.5),
        "ln3_g": gain((L, D_MODEL)),
        "ln3_b": nrm((L, D_MODEL), 0.02),
    }


def reference(x_prompt, x_sample, cache_mla_latent, cache_mla_krope, cache_sb_k, cache_sb_v,
              p_prompt, p_sample, ln_in_g, ln_in_b, w_in, b_gate, q_a_norm_g, w_q_b,
              kv_a_norm_g, w_kv_b, w_branch_a, w_branch_b, w_out, ln1_g, ln1_b,
              w_ffn_gu, w_ffn_down, ln2_g, ln2_b, w_ple_gate, b_ple_gate, w_ple_proj,
              ln3_g, ln3_b):
    def run(x, p, c_lat, c_kr, c_k, c_v):
        x = layer_norm(x, ln_in_g, ln_in_b)
        rows = []
        for i in range(DEPTH):
            x, r = trunk_layer(x, p[i], c_lat[i], c_kr[i], c_k[i], c_v[i],
                               w_in[i], b_gate[i], q_a_norm_g[i], w_q_b[i],
                               kv_a_norm_g[i], w_kv_b[i], w_branch_a[i], w_branch_b[i],
                               w_out[i], ln1_g[i], ln1_b[i], w_ffn_gu[i], w_ffn_down[i],
                               ln2_g[i], ln2_b[i], w_ple_gate[i], b_ple_gate[i],
                               w_ple_proj[i], ln3_g[i], ln3_b[i])
            rows.append(r)
        lat, kr, k, v = (jnp.stack(t) for t in zip(*rows))
        return x, lat, kr, k, v

    bp = x_prompt.shape[0]
    dt = x_prompt.dtype
    y_prompt, lat_p, kr_p, k_p, v_p = run(
        x_prompt, p_prompt,
        jnp.zeros((DEPTH, bp, 0, KV_LORA), dt), jnp.zeros((DEPTH, bp, 0, ROPE_DIM), dt),
        jnp.zeros((DEPTH, bp, 0, SB_HEADS, SB_DIM), dt), jnp.zeros((DEPTH, bp, 0, SB_HEADS, SB_DIM), dt))
    y_sample, lat_s, kr_s, k_s, v_s = run(
        x_sample, p_sample, cache_mla_latent, cache_mla_krope, cache_sb_k, cache_sb_v)
    return (y_prompt, y_sample, lat_p, kr_p, k_p, v_p, lat_s, kr_s, k_s, v_s)
```

```python
import functools

import numpy as np
import jax
import jax.numpy as jnp
from jax import lax
from jax.experimental import pallas as pl
from jax.experimental.pallas import tpu as pltpu

CHUNK = 64
MLA_HEADS = 16
Q_LORA = 512
KV_LORA = 512
NOPE_DIM = 128
ROPE_DIM = 64
V_DIM = 128
ROPE_THETA = 10000.0
MLA_SCALE = (NOPE_DIM + ROPE_DIM) ** -0.5
SB_HEADS = 8
SB_DIM = 128
SB_SCALE = SB_DIM ** -0.5
LN_EPS = 1e-5
RMS_EPS = 1e-6

BF16 = jnp.bfloat16
F32 = jnp.float32

V7X_VMEM_LIMIT_BYTES = 56 * 1024 * 1024
ROW_TILE = 544
COL_TILE = 512
MLA_Q_TILE = 128
MLA_K_TILE = 512
SB_TILE = 512
SB_SUB = 256
NEG_BIG = -1e30


def _pick_tile(n, target, mult):
    best = None
    for t in range(mult, min(n, target) + 1, mult):
        if n % t == 0:
            best = t
    if best is None:
        return n
    return best


def _cparams(sem):
    return pltpu.CompilerParams(dimension_semantics=sem,
                                vmem_limit_bytes=V7X_VMEM_LIMIT_BYTES)


def _dot(a, b):
    return jnp.dot(a, b, preferred_element_type=F32)


def _dot_nt(a, b):
    return lax.dot_general(a, b, (((1,), (1,)), ((), ())), preferred_element_type=F32)


def _layer_norm_rows(z, g, b):
    mu = jnp.mean(z, axis=-1, keepdims=True)
    zc = z - mu
    var = jnp.mean(zc * zc, axis=-1, keepdims=True)
    return zc * lax.rsqrt(var + LN_EPS) * g + b


def _rms_norm_rows(z, g):
    return z * lax.rsqrt(jnp.mean(z * z, axis=-1, keepdims=True) + RMS_EPS) * g


def _sigmoid(z):
    return 1.0 / (1.0 + jnp.exp(-z))


def _rope_rows(x, cos, sin):
    half = ROPE_DIM // 2
    x1 = x[:, :half]
    x2 = x[:, half:]
    return jnp.concatenate([x1 * cos - x2 * sin, x2 * cos + x1 * sin], axis=-1)


def _ln_in_kernel(x_ref, g_ref, b_ref, o_ref, ob_ref):
    y = _layer_norm_rows(x_ref[...], g_ref[...], b_ref[...])
    o_ref[...] = y
    ob_ref[...] = y.astype(BF16)


def _ln_in(x, g, b):
    t, d = x.shape
    tm = _pick_tile(t, ROW_TILE, 16)
    row = pl.BlockSpec((tm, d), lambda i: (i, 0))
    vec = pl.BlockSpec((1, d), lambda i: (0, 0))
    return pl.pallas_call(
        _ln_in_kernel,
        grid=(t // tm,),
        in_specs=[row, vec, vec],
        out_specs=[row, row],
        out_shape=[jax.ShapeDtypeStruct((t, d), F32), jax.ShapeDtypeStruct((t, d), BF16)],
        compiler_params=_cparams(("parallel",)),
        name="ln_in",
    )(x, g.reshape(1, d), b.reshape(1, d))


def _proj_qa_kernel(x_ref, w_ref, g_ref, o_ref):
    o_ref[...] = _rms_norm_rows(_dot(x_ref[...], w_ref[...]), g_ref[...]).astype(BF16)


def _proj_kv_kernel(x_ref, wl_ref, wr_ref, g_ref, cos_ref, sin_ref,
                    lat_ref, kr_ref, latb_ref, krb_ref):
    x = x_ref[...]
    lat = _rms_norm_rows(_dot(x, wl_ref[...]), g_ref[...])
    kr = _rope_rows(_dot(x, wr_ref[...]), cos_ref[...], sin_ref[...])
    lat_ref[...] = lat
    kr_ref[...] = kr
    latb_ref[...] = lat.astype(BF16)
    krb_ref[...] = kr.astype(BF16)


def _proj_scaled_kernel(x_ref, w_ref, o_ref, *, scale):
    o_ref[...] = (_dot(x_ref[...], w_ref[...]) * scale).astype(BF16)


def _proj_dual_kernel(x_ref, w_ref, o_ref, ob_ref):
    y = _dot(x_ref[...], w_ref[...])
    o_ref[...] = y
    ob_ref[...] = y.astype(BF16)


def _proj_gate_kernel(x_ref, w_ref, b_ref, o_ref):
    o_ref[...] = _sigmoid(_dot(x_ref[...], w_ref[...]) + b_ref[...]).astype(BF16)


def _proj_qa(xb, w, g):
    t, d = xb.shape
    n = w.shape[1]
    tm = _pick_tile(t, ROW_TILE, 16)
    return pl.pallas_call(
        _proj_qa_kernel,
        grid=(t // tm,),
        in_specs=[pl.BlockSpec((tm, d), lambda i: (i, 0)),
                  pl.BlockSpec((d, n), lambda i: (0, 0)),
                  pl.BlockSpec((1, n), lambda i: (0, 0))],
        out_specs=pl.BlockSpec((tm, n), lambda i: (i, 0)),
        out_shape=jax.ShapeDtypeStruct((t, n), BF16),
        compiler_params=_cparams(("parallel",)),
        name="proj_qa",
    )(xb, w, g.reshape(1, n))


def _proj_kv(xb, w_lat, w_kr, g, cos, sin):
    t, d = xb.shape
    c = w_lat.shape[1]
    r = w_kr.shape[1]
    tm = _pick_tile(t, ROW_TILE, 16)
    rows = lambda w: pl.BlockSpec((tm, w), lambda i: (i, 0))
    full = lambda a, b: pl.BlockSpec((a, b), lambda i: (0, 0))
    return pl.pallas_call(
        _proj_kv_kernel,
        grid=(t // tm,),
        in_specs=[rows(d), full(d, c), full(d, r), full(1, c), rows(r // 2), rows(r // 2)],
        out_specs=[rows(c), rows(r), rows(c), rows(r)],
        out_shape=[jax.ShapeDtypeStruct((t, c), F32), jax.ShapeDtypeStruct((t, r), F32),
                   jax.ShapeDtypeStruct((t, c), BF16), jax.ShapeDtypeStruct((t, r), BF16)],
        compiler_params=_cparams(("parallel",)),
        name="proj_kv",
    )(xb, w_lat, w_kr, g.reshape(1, c), cos, sin)


def _proj_tiled(kernel, xb, w, extra, out_dtypes, name):
    t, d = xb.shape
    n = w.shape[1]
    tm = _pick_tile(t, ROW_TILE, 16)
    tn = _pick_tile(n, COL_TILE, 128)
    out_block = pl.BlockSpec((tm, tn), lambda i, j: (i, j))
    return pl.pallas_call(
        kernel,
        grid=(t // tm, n // tn),
        in_specs=[pl.BlockSpec((tm, d), lambda i, j: (i, 0)),
                  pl.BlockSpec((d, tn), lambda i, j: (0, j))]
                 + [pl.BlockSpec((1, tn), lambda i, j: (0, j)) for _ in extra],
        out_specs=[out_block for _ in out_dtypes],
        out_shape=[jax.ShapeDtypeStruct((t, n), dt) for dt in out_dtypes],
        compiler_params=_cparams(("parallel", "parallel")),
        name=name,
    )(xb, w, *extra)


def _q_heads_kernel(qa_ref, wq_ref, wuk_ref, cos_ref, sin_ref, ql_ref, qr_ref):
    q = _dot(qa_ref[...], wq_ref[...])
    nope = q[:, :NOPE_DIM].astype(BF16)
    ql_ref[...] = (_dot(nope, wuk_ref[...]) * MLA_SCALE).astype(BF16)
    qr = _rope_rows(q[:, NOPE_DIM:], cos_ref[...], sin_ref[...])
    qr_ref[...] = (qr * MLA_SCALE).astype(BF16)


def _q_heads(qa_n, wq_h, wuk_h, cos, sin):
    t, ql = qa_n.shape
    h, _, hd = wq_h.shape
    c = wuk_h.shape[2]
    tm = _pick_tile(t, ROW_TILE, 16)
    half = ROPE_DIM // 2
    return pl.pallas_call(
        _q_heads_kernel,
        grid=(t // tm, h),
        in_specs=[pl.BlockSpec((tm, ql), lambda i, j: (i, 0)),
                  pl.BlockSpec((None, ql, hd), lambda i, j: (j, 0, 0)),
                  pl.BlockSpec((None, NOPE_DIM, c), lambda i, j: (j, 0, 0)),
                  pl.BlockSpec((tm, half), lambda i, j: (i, 0)),
                  pl.BlockSpec((tm, half), lambda i, j: (i, 0))],
        out_specs=[pl.BlockSpec((None, tm, c), lambda i, j: (j, i, 0)),
                   pl.BlockSpec((None, tm, ROPE_DIM), lambda i, j: (j, i, 0))],
        out_shape=[jax.ShapeDtypeStruct((h, t, c), BF16),
                   jax.ShapeDtypeStruct((h, t, ROPE_DIM), BF16)],
        compiler_params=_cparams(("parallel", "parallel")),
        name="q_heads",
    )(qa_n, wq_h, wuk_h, cos, sin)


def _mla_update(s, klat, m_ref, l_ref, acc_ref):
    m_prev = m_ref[...]
    m_new = jnp.maximum(m_prev, jnp.max(s, axis=-1, keepdims=True))
    alpha = jnp.exp(m_prev - m_new)
    p = jnp.exp(s - m_new)
    l_ref[...] = alpha * l_ref[...] + jnp.sum(p, axis=-1, keepdims=True)
    acc_ref[...] = alpha * acc_ref[...] + _dot(p.astype(BF16), klat)
    m_ref[...] = m_new


def _mla_init(m_ref, l_ref, acc_ref):
    m_ref[...] = jnp.full(m_ref.shape, NEG_BIG, F32)
    l_ref[...] = jnp.zeros(l_ref.shape, F32)
    acc_ref[...] = jnp.zeros(acc_ref.shape, F32)


def _mla_finalize(wuv_ref, o_ref, l_ref, acc_ref, tq):
    inv = 1.0 / l_ref[...]
    for h in range(MLA_HEADS):
        rows = slice(h * tq, (h + 1) * tq)
        ctx = (acc_ref[rows, :] * inv[rows, :]).astype(BF16)
        o_ref[:, h * V_DIM:(h + 1) * V_DIM] = _dot(ctx, wuv_ref[h]).astype(BF16)


def _chunk_visible(q_pos0, k_pos0, rows, tq, tk):
    t = lax.rem(lax.broadcasted_iota(jnp.int32, (rows, 1), 0), tq)
    q_chunk = (q_pos0 + t) // CHUNK
    k_chunk = (k_pos0 + lax.broadcasted_iota(jnp.int32, (1, tk), 1)) // CHUNK
    return k_chunk <= q_chunk


def _mla_prompt_kernel(qb_ref, kb_ref, qp_ref, kp_ref, fl_ref,
                       ql_ref, qr_ref, kl_ref, kr_ref, wuv_ref, o_ref,
                       m_ref, l_ref, acc_ref, *, tq, tk):
    n = pl.program_id(0)
    rows = MLA_HEADS * tq

    @pl.when(fl_ref[n] % 2 == 1)
    def _():
        _mla_init(m_ref, l_ref, acc_ref)

    q_lat = ql_ref[...].reshape(rows, ql_ref.shape[-1])
    q_rope = qr_ref[...].reshape(rows, qr_ref.shape[-1])
    klat = kl_ref[...]
    s = _dot_nt(q_lat, klat) + _dot_nt(q_rope, kr_ref[...])
    s = jnp.where(_chunk_visible(qp_ref[n], kp_ref[n], rows, tq, tk), s, NEG_BIG)
    _mla_update(s, klat, m_ref, l_ref, acc_ref)

    @pl.when(fl_ref[n] >= 2)
    def _():
        _mla_finalize(wuv_ref, o_ref, l_ref, acc_ref, tq)


def _mla_prompt(q_lat, q_rope, k_lat, k_rope, w_uv, batch, seq):
    h, t, c = q_lat.shape
    tq = _pick_tile(seq, MLA_Q_TILE, CHUNK)
    tk = _pick_tile(seq, MLA_K_TILE, CHUNK)
    nq, nk = seq // tq, seq // tk
    qb, kb, qp, kp, fl = [], [], [], [], []
    for b in range(batch):
        for qi in range(nq):
            last = ((qi + 1) * tq - 1) // tk
            for kj in range(last + 1):
                qb.append(b * nq + qi)
                kb.append(b * nk + kj)
                qp.append(qi * tq)
                kp.append(kj * tk)
                fl.append((1 if kj == 0 else 0) + (2 if kj == last else 0))
    tabs = [jnp.asarray(np.asarray(a, np.int32)) for a in (qb, kb, qp, kp, fl)]
    rows = h * tq
    grid_spec = pltpu.PrefetchScalarGridSpec(
        num_scalar_prefetch=5,
        grid=(len(qb),),
        in_specs=[
            pl.BlockSpec((h, tq, c), lambda n, qb, kb, qp, kp, fl: (0, qb[n], 0)),
            pl.BlockSpec((h, tq, ROPE_DIM), lambda n, qb, kb, qp, kp, fl: (0, qb[n], 0)),
            pl.BlockSpec((tk, c), lambda n, qb, kb, qp, kp, fl: (kb[n], 0)),
            pl.BlockSpec((tk, ROPE_DIM), lambda n, qb, kb, qp, kp, fl: (kb[n], 0)),
            pl.BlockSpec((h, c, V_DIM), lambda n, qb, kb, qp, kp, fl: (0, 0, 0)),
        ],
        out_specs=pl.BlockSpec((tq, h * V_DIM), lambda n, qb, kb, qp, kp, fl: (qb[n], 0)),
        scratch_shapes=[pltpu.VMEM((rows, 1), F32), pltpu.VMEM((rows, 1), F32),
                        pltpu.VMEM((rows, c), F32)],
    )
    return pl.pallas_call(
        functools.partial(_mla_prompt_kernel, tq=tq, tk=tk),
        grid_spec=grid_spec,
        out_shape=jax.ShapeDtypeStruct((batch * seq, h * V_DIM), BF16),
        compiler_params=_cparams(("arbitrary",)),
        name="mla_prompt",
    )(*tabs, q_lat, q_rope, k_lat, k_rope, w_uv)


def _mla_sample_kernel(ql_ref, qr_ref, pl_ref, pr_ref, nl_ref, nr_ref, wuv_ref, o_ref,
                       m_ref, l_ref, acc_ref, *, tq, n_past, n_kt):
    j = pl.program_id(1)
    rows = MLA_HEADS * tq
    q_lat = ql_ref[...].reshape(rows, ql_ref.shape[-1])
    q_rope = qr_ref[...].reshape(rows, qr_ref.shape[-1])

    @pl.when(j == 0)
    def _():
        _mla_init(m_ref, l_ref, acc_ref)

    @pl.when(j < n_kt)
    def _():
        klat = pl_ref[...].astype(BF16)
        s = _dot_nt(q_lat, klat) + _dot_nt(q_rope, pr_ref[...].astype(BF16))
        _mla_update(s, klat, m_ref, l_ref, acc_ref)

    @pl.when(j == n_kt)
    def _():
        klat = nl_ref[...]
        s = _dot_nt(q_lat, klat) + _dot_nt(q_rope, nr_ref[...])
        s = jnp.where(_chunk_visible(n_past, n_past, rows, tq, tq), s, NEG_BIG)
        _mla_update(s, klat, m_ref, l_ref, acc_ref)
        _mla_finalize(wuv_ref, o_ref, l_ref, acc_ref, tq)


def _mla_sample(q_lat, q_rope, past_lat, past_kr, layer, k_lat, k_rope, w_uv,
                row0, dec_batch, dec_seq):
    h, t, c = q_lat.shape
    n_past = past_lat.shape[2]
    tk = _pick_tile(n_past, MLA_K_TILE, 128)
    n_kt = n_past // tk
    blk0 = row0 // dec_seq
    rows = h * dec_seq
    past_idx = lambda b, j: (layer, b, jnp.minimum(j, n_kt - 1), 0)
    return pl.pallas_call(
        functools.partial(_mla_sample_kernel, tq=dec_seq, n_past=n_past, n_kt=n_kt),
        grid=(dec_batch, n_kt + 1),
        in_specs=[
            pl.BlockSpec((h, dec_seq, c), lambda b, j: (0, blk0 + b, 0)),
            pl.BlockSpec((h, dec_seq, ROPE_DIM), lambda b, j: (0, blk0 + b, 0)),
            pl.BlockSpec((None, None, tk, c), past_idx),
            pl.BlockSpec((None, None, tk, ROPE_DIM), past_idx),
            pl.BlockSpec((dec_seq, c), lambda b, j: (blk0 + b, 0)),
            pl.BlockSpec((dec_seq, ROPE_DIM), lambda b, j: (blk0 + b, 0)),
            pl.BlockSpec((h, c, V_DIM), lambda b, j: (0, 0, 0)),
        ],
        out_specs=pl.BlockSpec((dec_seq, h * V_DIM), lambda b, j: (b, 0)),
        out_shape=jax.ShapeDtypeStruct((dec_batch * dec_seq, h * V_DIM), BF16),
        scratch_shapes=[pltpu.VMEM((rows, 1), F32), pltpu.VMEM((rows, 1), F32),
                        pltpu.VMEM((rows, c), F32)],
        compiler_params=_cparams(("parallel", "arbitrary")),
        name="mla_sample",
    )(q_lat, q_rope, past_lat, past_kr, k_lat, k_rope, w_uv)


def _scan_matrix(sub):
    j = np.arange(sub)[:, None]
    s = np.arange(sub)[None, :]
    u = (j > s).astype(np.float32)
    return jnp.asarray(np.concatenate([u, u], axis=0), dtype=BF16)


def _sb_block(qh, kh, vh, u2, before, carry):
    z = _dot_nt(qh, kh)
    log_beta = jnp.minimum(z, 0.0) - jnp.log(1.0 + jnp.exp(-jnp.abs(z)))
    log_rest = log_beta - z
    if before is not None:
        log_rest = jnp.where(before, log_rest, 0.0)
    hi = log_rest.astype(BF16)
    lo = (log_rest - hi.astype(F32)).astype(BF16)
    tail = _dot(jnp.concatenate([hi, lo], axis=-1), u2) + carry
    a = jnp.exp(log_beta + tail)
    if before is not None:
        a = jnp.where(before, a, 0.0)
    out = _dot(a.astype(BF16), vh)
    return out, carry + jnp.sum(log_rest, axis=-1, keepdims=True)


def _sb_prompt_kernel(qb_ref, kb_ref, qp_ref, kp_ref, fl_ref,
                      q_ref, k_ref, v_ref, u_ref, o_ref, acc_ref, carry_ref, *, tq, tk, sub):
    n = pl.program_id(0)

    @pl.when(fl_ref[n] % 2 == 1)
    def _():
        acc_ref[...] = jnp.zeros(acc_ref.shape, F32)
        carry_ref[...] = jnp.zeros(carry_ref.shape, F32)

    q_pos = qp_ref[n] + lax.broadcasted_iota(jnp.int32, (tq, 1), 0)
    u2 = u_ref[...]
    for c in reversed(range(tk // sub)):
        k_pos = kp_ref[n] + c * sub + lax.broadcasted_iota(jnp.int32, (1, sub), 1)
        before = k_pos < q_pos
        for h in range(SB_HEADS):
            cols = slice(h * SB_DIM, (h + 1) * SB_DIM)
            out, carry = _sb_block(q_ref[:, cols], k_ref[c * sub:(c + 1) * sub, cols],
                                   v_ref[c * sub:(c + 1) * sub, cols], u2, before,
                                   carry_ref[h])
            acc_ref[:, cols] += out
            carry_ref[h] = carry

    @pl.when(fl_ref[n] >= 2)
    def _():
        o_ref[...] = acc_ref[...].astype(BF16)


def _sb_prompt(q, k, v, batch, seq):
    width = q.shape[1]
    tq = _pick_tile(seq, SB_TILE, 128)
    tk = tq
    sub = _pick_tile(tk, SB_SUB, 128)
    nq = seq // tq
    qb, kb, qp, kp, fl = [], [], [], [], []
    for b in range(batch):
        for qi in range(nq):
            for kj in range(qi, -1, -1):
                qb.append(b * nq + qi)
                kb.append(b * nq + kj)
                qp.append(qi * tq)
                kp.append(kj * tk)
                fl.append((1 if kj == qi else 0) + (2 if kj == 0 else 0))
    tabs = [jnp.asarray(np.asarray(a, np.int32)) for a in (qb, kb, qp, kp, fl)]
    grid_spec = pltpu.PrefetchScalarGridSpec(
        num_scalar_prefetch=5,
        grid=(len(qb),),
        in_specs=[
            pl.BlockSpec((tq, width), lambda n, qb, kb, qp, kp, fl: (qb[n], 0)),
            pl.BlockSpec((tk, width), lambda n, qb, kb, qp, kp, fl: (kb[n], 0)),
            pl.BlockSpec((tk, width), lambda n, qb, kb, qp, kp, fl: (kb[n], 0)),
            pl.BlockSpec((2 * sub, sub), lambda n, qb, kb, qp, kp, fl: (0, 0)),
        ],
        out_specs=pl.BlockSpec((tq, width), lambda n, qb, kb, qp, kp, fl: (qb[n], 0)),
        scratch_shapes=[pltpu.VMEM((tq, width), F32), pltpu.VMEM((SB_HEADS, tq, 1), F32)],
    )
    return pl.pallas_call(
        functools.partial(_sb_prompt_kernel, tq=tq, tk=tk, sub=sub),
        grid_spec=grid_spec,
        out_shape=jax.ShapeDtypeStruct((batch * seq, width), BF16),
        compiler_params=_cparams(("arbitrary",)),
        name="sb_prompt",
    )(*tabs, q, k, v, _scan_matrix(sub))


def _sb_sample_kernel(q_ref, pk_ref, pv_ref, nk_ref, nv_ref, un_ref, up_ref, o_ref,
                      acc_ref, carry_ref, *, tq, tk, sub, n_kt):
    j = pl.program_id(1)

    @pl.when(j == 0)
    def _():
        acc_ref[...] = jnp.zeros(acc_ref.shape, F32)
        before = (lax.broadcasted_iota(jnp.int32, (1, tq), 1)
                  < lax.broadcasted_iota(jnp.int32, (tq, 1), 0))
        u2 = un_ref[...]
        for h in range(SB_HEADS):
            cols = slice(h * SB_DIM, (h + 1) * SB_DIM)
            out, carry = _sb_block(q_ref[:, cols], nk_ref[:, cols], nv_ref[:, cols], u2,
                                   before, jnp.zeros((tq, 1), F32))
            acc_ref[:, cols] += out
            carry_ref[h] = carry

    @pl.when(j > 0)
    def _():
        u2 = up_ref[...]
        for c in reversed(range(tk // sub)):
            for h in range(SB_HEADS):
                cols = slice(h * SB_DIM, (h + 1) * SB_DIM)
                kh = pk_ref[c * sub:(c + 1) * sub, cols].astype(BF16)
                vh = pv_ref[c * sub:(c + 1) * sub, cols].astype(BF16)
                out, carry = _sb_block(q_ref[:, cols], kh, vh, u2, None, carry_ref[h])
                acc_ref[:, cols] += out
                carry_ref[h] = carry

    @pl.when(j == n_kt)
    def _():
        o_ref[...] = acc_ref[...].astype(BF16)


def _sb_sample(q, past_k, past_v, layer, k, v, row0, dec_batch, dec_seq):
    width = q.shape[1]
    n_past = past_k.shape[2]
    tk = _pick_tile(n_past, SB_TILE, 128)
    sub = _pick_tile(tk, SB_SUB, 128)
    n_kt = n_past // tk
    blk0 = row0 // dec_seq
    past_idx = lambda b, j: (layer, b, n_kt - jnp.maximum(j, 1), 0)
    new_rows = pl.BlockSpec((dec_seq, width), lambda b, j: (blk0 + b, 0))
    return pl.pallas_call(
        functools.partial(_sb_sample_kernel, tq=dec_seq, tk=tk, sub=sub, n_kt=n_kt),
        grid=(dec_batch, n_kt + 1),
        in_specs=[
            new_rows,
            pl.BlockSpec((None, None, tk, width), past_idx),
            pl.BlockSpec((None, None, tk, width), past_idx),
            new_rows,
            new_rows,
            pl.BlockSpec((2 * dec_seq, dec_seq), lambda b, j: (0, 0)),
            pl.BlockSpec((2 * sub, sub), lambda b, j: (0, 0)),
        ],
        out_specs=pl.BlockSpec((dec_seq, width), lambda b, j: (b, 0)),
        out_shape=jax.ShapeDtypeStruct((dec_batch * dec_seq, width), BF16),
        scratch_shapes=[pltpu.VMEM((dec_seq, width), F32),
                        pltpu.VMEM((SB_HEADS, dec_seq, 1), F32)],
        compiler_params=_cparams(("parallel", "arbitrary")),
        name="sb_sample",
    )(q, past_k, past_v, k, v, _scan_matrix(dec_seq), _scan_matrix(sub))


def _merge_kernel(oa_ref, ob_ref, wa_ref, wb_ref, ga_ref, gb_ref, o_ref):
    a = _dot(oa_ref[...], wa_ref[...])
    b = _dot(ob_ref[...], wb_ref[...])
    o_ref[...] = (ga_ref[...].astype(F32) * a + gb_ref[...].astype(F32) * b).astype(BF16)


def _merge(o_a, o_b, w_a, w_b, gates):
    t, da = o_a.shape
    db = o_b.shape[1]
    n = w_a.shape[1]
    tm = _pick_tile(t, ROW_TILE, 16)
    tn = _pick_tile(n, COL_TILE, 128)
    nj = n // tn
    return pl.pallas_call(
        _merge_kernel,
        grid=(t // tm, nj),
        in_specs=[pl.BlockSpec((tm, da), lambda i, j: (i, 0)),
                  pl.BlockSpec((tm, db), lambda i, j: (i, 0)),
                  pl.BlockSpec((da, tn), lambda i, j: (0, j)),
                  pl.BlockSpec((db, tn), lambda i, j: (0, j)),
                  pl.BlockSpec((tm, tn), lambda i, j: (i, j)),
                  pl.BlockSpec((tm, tn), lambda i, j: (i, j + nj))],
        out_specs=pl.BlockSpec((tm, tn), lambda i, j: (i, j)),
        out_shape=jax.ShapeDtypeStruct((t, n), BF16),
        compiler_params=_cparams(("parallel", "parallel")),
        name="merge",
    )(o_a, o_b, w_a, w_b, gates, gates)


def _res_ln_finalize(z_ref, g_ref, b_ref, o_ref, ob_ref):
    n_tiles, _, tn = z_ref.shape
    width = n_tiles * tn
    total = z_ref[0].sum(axis=-1, keepdims=True)
    for c in range(1, n_tiles):
        total += z_ref[c].sum(axis=-1, keepdims=True)
    mu = total / width
    sq = jnp.square(z_ref[0] - mu).sum(axis=-1, keepdims=True)
    for c in range(1, n_tiles):
        sq += jnp.square(z_ref[c] - mu).sum(axis=-1, keepdims=True)
    rstd = lax.rsqrt(sq / width + LN_EPS)
    for c in range(n_tiles):
        cols = slice(c * tn, (c + 1) * tn)
        y = (z_ref[c] - mu) * rstd * g_ref[:, cols] + b_ref[:, cols]
        o_ref[:, cols] = y
        ob_ref[:, cols] = y.astype(BF16)


def _out_ln_kernel(m_ref, w_ref, x_ref, g_ref, b_ref, o_ref, ob_ref, z_ref, *, alpha):
    j = pl.program_id(1)
    z_ref[j] = alpha * x_ref[...] + _dot(m_ref[...], w_ref[...])

    @pl.when(j == pl.num_programs(1) - 1)
    def _():
        _res_ln_finalize(z_ref, g_ref, b_ref, o_ref, ob_ref)


def _out_ln(merged, w, x, g, b, alpha):
    t, d = x.shape
    k = merged.shape[1]
    tm = _pick_tile(t, ROW_TILE, 16)
    tn = _pick_tile(d, COL_TILE, 128)
    full_row = pl.BlockSpec((tm, d), lambda i, j: (i, 0))
    vec = pl.BlockSpec((1, d), lambda i, j: (0, 0))
    return pl.pallas_call(
        functools.partial(_out_ln_kernel, alpha=alpha),
        grid=(t // tm, d // tn),
        in_specs=[pl.BlockSpec((tm, k), lambda i, j: (i, 0)),
                  pl.BlockSpec((k, tn), lambda i, j: (0, j)),
                  pl.BlockSpec((tm, tn), lambda i, j: (i, j)),
                  vec, vec],
        out_specs=[full_row, full_row],
        out_shape=[jax.ShapeDtypeStruct((t, d), F32), jax.ShapeDtypeStruct((t, d), BF16)],
        scratch_shapes=[pltpu.VMEM((d // tn, tm, tn), F32)],
        compiler_params=_cparams(("parallel", "arbitrary")),
        name="out_ln",
    )(merged, w, x, g.reshape(1, d), b.reshape(1, d))


def _ffn_ln_kernel(xb_ref, wg_ref, wu_ref, wd_ref, x_ref, g_ref, b_ref, o_ref, ob_ref,
                   acc_ref, *, alpha):
    f = pl.program_id(1)

    @pl.when(f == 0)
    def _():
        acc_ref[...] = jnp.zeros(acc_ref.shape, F32)

    xb = xb_ref[...]
    gate = _dot(xb, wg_ref[...])
    up = _dot(xb, wu_ref[...])
    hidden = (gate * _sigmoid(gate) * up).astype(BF16)
    acc_ref[...] += _dot(hidden, wd_ref[...])

    @pl.when(f == pl.num_programs(1) - 1)
    def _():
        y = _layer_norm_rows(alpha * x_ref[...] + acc_ref[...], g_ref[...], b_ref[...])
        o_ref[...] = y
        ob_ref[...] = y.astype(BF16)


def _ffn_ln(xb, w_gu, w_down, x, g, b, alpha):
    t, d = x.shape
    ff = w_down.shape[0]
    tm = _pick_tile(t, ROW_TILE, 16)
    tf = _pick_tile(ff, COL_TILE, 128)
    nf = ff // tf
    full_row = pl.BlockSpec((tm, d), lambda i, f: (i, 0))
    vec = pl.BlockSpec((1, d), lambda i, f: (0, 0))
    return pl.pallas_call(
        functools.partial(_ffn_ln_kernel, alpha=alpha),
        grid=(t // tm, nf),
        in_specs=[full_row,
                  pl.BlockSpec((d, tf), lambda i, f: (0, f)),
                  pl.BlockSpec((d, tf), lambda i, f: (0, f + nf)),
                  pl.BlockSpec((tf, d), lambda i, f: (f, 0)),
                  full_row, vec, vec],
        out_specs=[full_row, full_row],
        out_shape=[jax.ShapeDtypeStruct((t, d), F32), jax.ShapeDtypeStruct((t, d), BF16)],
        scratch_shapes=[pltpu.VMEM((tm, d), F32)],
        compiler_params=_cparams(("parallel", "arbitrary")),
        name="ffn_ln",
    )(xb, w_gu, w_gu, w_down, x, g.reshape(1, d), b.reshape(1, d))


def _ple_ln_kernel(xb_ref, wg_ref, bg_ref, p_ref, wp_ref, x_ref, g_ref, b_ref, o_ref, ob_ref,
                   z_ref, *, alpha):
    j = pl.program_id(1)
    gate = _sigmoid(_dot(xb_ref[...], wg_ref[...]) + bg_ref[...])
    proj = _dot(p_ref[...].astype(BF16), wp_ref[...])
    z_ref[j] = alpha * x_ref[...] + gate * proj

    @pl.when(j == pl.num_programs(1) - 1)
    def _():
        _res_ln_finalize(z_ref, g_ref, b_ref, o_ref, ob_ref)


def _ple_ln(xb, w_gate, b_gate, p, w_proj, x, g, b, alpha):
    t, d = x.shape
    pd = p.shape[1]
    tm = _pick_tile(t, ROW_TILE, 16)
    tn = _pick_tile(d, COL_TILE, 128)
    full_row = pl.BlockSpec((tm, d), lambda i, j: (i, 0))
    vec = pl.BlockSpec((1, d), lambda i, j: (0, 0))
    return pl.pallas_call(
        functools.partial(_ple_ln_kernel, alpha=alpha),
        grid=(t // tm, d // tn),
        in_specs=[full_row,
                  pl.BlockSpec((d, tn), lambda i, j: (0, j)),
                  pl.BlockSpec((1, tn), lambda i, j: (0, j)),
                  pl.BlockSpec((tm, pd), lambda i, j: (i, 0)),
                  pl.BlockSpec((pd, tn), lambda i, j: (0, j)),
                  pl.BlockSpec((tm, tn), lambda i, j: (i, j)),
                  vec, vec],
        out_specs=[full_row, full_row],
        out_shape=[jax.ShapeDtypeStruct((t, d), F32), jax.ShapeDtypeStruct((t, d), BF16)],
        scratch_shapes=[pltpu.VMEM((d // tn, tm, tn), F32)],
        compiler_params=_cparams(("parallel", "arbitrary")),
        name="ple_ln",
    )(xb, w_gate, b_gate.reshape(1, d), p, w_proj, x, g.reshape(1, d), b.reshape(1, d))


def _rope_tables(pos):
    half = ROPE_DIM // 2
    inv_freq = 1.0 / (ROPE_THETA ** (jnp.arange(half, dtype=F32) * (2.0 / ROPE_DIM)))
    ang = pos.astype(F32)[:, None] * inv_freq[None, :]
    return jnp.cos(ang), jnp.sin(ang)


def kernel(x_prompt, x_sample, cache_mla_latent, cache_mla_krope, cache_sb_k, cache_sb_v,
           p_prompt, p_sample, ln_in_g, ln_in_b, w_in, b_gate, q_a_norm_g, w_q_b,
           kv_a_norm_g, w_kv_b, w_branch_a, w_branch_b, w_out, ln1_g, ln1_b,
           w_ffn_gu, w_ffn_down, ln2_g, ln2_b, w_ple_gate, b_ple_gate, w_ple_proj,
           ln3_g, ln3_b):
    batch, seq, d = x_prompt.shape
    dec_batch, dec_seq, _ = x_sample.shape
    depth = w_in.shape[0]
    n_past = cache_mla_latent.shape[2]
    tp = batch * seq
    ts = dec_batch * dec_seq
    sb_width = SB_HEADS * SB_DIM
    alpha = (2 * depth) ** 0.25

    pos = jnp.concatenate([jnp.tile(jnp.arange(seq, dtype=jnp.int32), batch),
                           jnp.tile(n_past + jnp.arange(dec_seq, dtype=jnp.int32), dec_batch)])
    cos, sin = _rope_tables(pos)

    past_k = cache_sb_k.reshape(depth, dec_batch, n_past, sb_width)
    past_v = cache_sb_v.reshape(depth, dec_batch, n_past, sb_width)

    x_all = jnp.concatenate([x_prompt.reshape(tp, d), x_sample.reshape(ts, d)], axis=0)
    x, xb = _ln_in(x_all, ln_in_g, ln_in_b)

    splits = np.cumsum([0, Q_LORA, KV_LORA, ROPE_DIM, sb_width, sb_width, sb_width, 2 * d])
    new_lat, new_kr, new_k, new_v = [], [], [], []
    for l in range(depth):
        w_cols = [w_in[l][:, splits[i]:splits[i + 1]].astype(BF16) for i in range(7)]
        w_qa, w_lat, w_kr, w_sq, w_sk, w_sv, w_gates = w_cols
        wq_h = w_q_b[l].reshape(Q_LORA, MLA_HEADS, NOPE_DIM + ROPE_DIM)
        wq_h = wq_h.transpose(1, 0, 2).astype(BF16)
        w_kv = w_kv_b[l].reshape(KV_LORA, MLA_HEADS, NOPE_DIM + V_DIM)
        wuk_h = w_kv[..., :NOPE_DIM].transpose(1, 2, 0).astype(BF16)
        wuv_h = w_kv[..., NOPE_DIM:].transpose(1, 0, 2).astype(BF16)

        qa_n = _proj_qa(xb, w_qa, q_a_norm_g[l])
        lat, kr, lat_b, kr_b = _proj_kv(xb, w_lat, w_kr, kv_a_norm_g[l], cos, sin)
        (sbq,) = _proj_tiled(functools.partial(_proj_scaled_kernel, scale=SB_SCALE),
                             xb, w_sq, [], [BF16], "proj_sbq")
        sbk, sbk_b = _proj_tiled(_proj_dual_kernel, xb, w_sk, [], [F32, BF16], "proj_sbk")
        sbv, sbv_b = _proj_tiled(_proj_dual_kernel, xb, w_sv, [], [F32, BF16], "proj_sbv")
        (gates,) = _proj_tiled(_proj_gate_kernel, xb, w_gates, [b_gate[l].reshape(1, 2 * d)],
                               [BF16], "proj_gates")

        q_lat, q_rope = _q_heads(qa_n, wq_h, wuk_h, cos, sin)
        oa_p = _mla_prompt(q_lat, q_rope, lat_b, kr_b, wuv_h, batch, seq)
        oa_s = _mla_sample(q_lat, q_rope, cache_mla_latent, cache_mla_krope, l, lat_b, kr_b,
                           wuv_h, tp, dec_batch, dec_seq)
        ob_p = _sb_prompt(sbq, sbk_b, sbv_b, batch, seq)
        ob_s = _sb_sample(sbq, past_k, past_v, l, sbk_b, sbv_b, tp, dec_batch, dec_seq)
        o_a = jnp.concatenate([oa_p, oa_s], axis=0)
        o_b = jnp.concatenate([ob_p, ob_s], axis=0)

        merged = _merge(o_a, o_b, w_branch_a[l].astype(BF16), w_branch_b[l].astype(BF16), gates)
        x, xb = _out_ln(merged, w_out[l].astype(BF16), x, ln1_g[l], ln1_b[l], alpha)
        x, xb = _ffn_ln(xb, w_ffn_gu[l].astype(BF16), w_ffn_down[l].astype(BF16), x,
                        ln2_g[l], ln2_b[l], alpha)
        p_all = jnp.concatenate([p_prompt[l].reshape(tp, -1), p_sample[l].reshape(ts, -1)], axis=0)
        x, xb = _ple_ln(xb, w_ple_gate[l].astype(BF16), b_ple_gate[l], p_all,
                        w_ple_proj[l].astype(BF16), x, ln3_g[l], ln3_b[l], alpha)

        new_lat.append(lat)
        new_kr.append(kr)
        new_k.append(sbk)
        new_v.append(sbv)

    def split(rows_list, tail):
        stacked = jnp.stack(rows_list)
        return (stacked[:, :tp].reshape(depth, batch, seq, *tail),
                stacked[:, tp:].reshape(depth, dec_batch, dec_seq, *tail))

    lat_p, lat_s = split(new_lat, (KV_LORA,))
    kr_p, kr_s = split(new_kr, (ROPE_DIM,))
    k_p, k_s = split(new_k, (SB_HEADS, SB_DIM))
    v_p, v_s = split(new_v, (SB_HEADS, SB_DIM))
    y_prompt = x[:tp].reshape(batch, seq, d)
    y_sample = x[tp:].reshape(dec_batch, dec_seq, d)
    return (y_prompt, y_sample, lat_p, kr_p, k_p, v_p, lat_s, kr_s, k_s, v_s)
```

```python
import collections
import functools

import numpy as np
import jax
import jax.numpy as jnp
from jax import lax
from jax.experimental import pallas as pl
from jax.experimental.pallas import tpu as pltpu

CHUNK = 64
MLA_HEADS = 16
Q_LORA = 512
KV_LORA = 512
NOPE_DIM = 128
ROPE_DIM = 64
V_DIM = 128
ROPE_THETA = 10000.0
MLA_SCALE = (NOPE_DIM + ROPE_DIM) ** -0.5
SB_HEADS = 8
SB_DIM = 128
SB_SCALE = SB_DIM ** -0.5
LN_EPS = 1e-5
RMS_EPS = 1e-6
LOG2E = 1.4426950408889634
MLA_QSCALE = MLA_SCALE * LOG2E
SB_QSCALE = SB_SCALE * LOG2E
SB_DEAD_LOG2 = -160.0

BF16 = jnp.bfloat16
F32 = jnp.float32

V7X_VMEM_LIMIT_BYTES = 56 * 1024 * 1024
ROW_TILE = 544
COL_TILE = 512
MLA_Q_TILE = 128
MLA_K_TILE = 512
MLA_SOFTMAX_COLS = 128
SB_TILE = 512
SB_SUB = 256
SB_ROWS = 128
NEG_BIG = -1e30


def _pick_tile(n, target, mult):
    best = None
    for t in range(mult, min(n, target) + 1, mult):
        if n % t == 0:
            best = t
    if best is None:
        return n
    return best


def _cparams(sem):
    return pltpu.CompilerParams(dimension_semantics=sem,
                                vmem_limit_bytes=V7X_VMEM_LIMIT_BYTES)


def _dot(a, b):
    return jnp.dot(a, b, preferred_element_type=F32)


def _dot_nt(a, b):
    return lax.dot_general(a, b, (((1,), (1,)), ((), ())), preferred_element_type=F32)


def _layer_norm_rows(z, g, b):
    mu = jnp.mean(z, axis=-1, keepdims=True)
    zc = z - mu
    var = jnp.mean(zc * zc, axis=-1, keepdims=True)
    return zc * lax.rsqrt(var + LN_EPS) * g + b


def _rms_norm_rows(z, g):
    return z * lax.rsqrt(jnp.mean(z * z, axis=-1, keepdims=True) + RMS_EPS) * g


def _sigmoid(z):
    return 1.0 / (1.0 + jnp.exp(-z))


def _rope_rows(x, cos, sin):
    half = ROPE_DIM // 2
    x1 = x[:, :half]
    x2 = x[:, half:]
    return jnp.concatenate([x1 * cos - x2 * sin, x2 * cos + x1 * sin], axis=-1)


def _ln_in_kernel(x_ref, g_ref, b_ref, o_ref, ob_ref):
    y = _layer_norm_rows(x_ref[...], g_ref[...], b_ref[...])
    o_ref[...] = y
    ob_ref[...] = y.astype(BF16)


def _ln_in(x, g, b):
    t, d = x.shape
    tm = _pick_tile(t, ROW_TILE, 16)
    row = pl.BlockSpec((tm, d), lambda i: (i, 0))
    vec = pl.BlockSpec((1, d), lambda i: (0, 0))
    return pl.pallas_call(
        _ln_in_kernel,
        grid=(t // tm,),
        in_specs=[row, vec, vec],
        out_specs=[row, row],
        out_shape=[jax.ShapeDtypeStruct((t, d), F32), jax.ShapeDtypeStruct((t, d), BF16)],
        compiler_params=_cparams(("parallel",)),
        name="ln_in",
    )(x, g.reshape(1, d), b.reshape(1, d))


def _proj_qa_kernel(x_ref, w_ref, g_ref, o_ref):
    o_ref[...] = _rms_norm_rows(_dot(x_ref[...], w_ref[...]), g_ref[...]).astype(BF16)


def _proj_kv_kernel(x_ref, wl_ref, wr_ref, g_ref, cos_ref, sin_ref,
                    lat_ref, kr_ref, latb_ref, krb_ref):
    x = x_ref[...]
    lat = _rms_norm_rows(_dot(x, wl_ref[...]), g_ref[...])
    kr = _rope_rows(_dot(x, wr_ref[...]), cos_ref[...], sin_ref[...])
    lat_ref[...] = lat
    kr_ref[...] = kr
    latb_ref[...] = lat.astype(BF16)
    krb_ref[...] = kr.astype(BF16)


def _proj_scaled_kernel(x_ref, w_ref, o_ref, *, scale):
    o_ref[...] = (_dot(x_ref[...], w_ref[...]) * scale).astype(BF16)


def _proj_dual_kernel(x_ref, w_ref, o_ref, ob_ref):
    y = _dot(x_ref[...], w_ref[...])
    o_ref[...] = y
    ob_ref[...] = y.astype(BF16)


def _proj_gate_kernel(x_ref, w_ref, b_ref, o_ref):
    o_ref[...] = _sigmoid(_dot(x_ref[...], w_ref[...]) + b_ref[...]).astype(BF16)


def _proj_qa(xb, w, g):
    t, d = xb.shape
    n = w.shape[1]
    tm = _pick_tile(t, ROW_TILE, 16)
    return pl.pallas_call(
        _proj_qa_kernel,
        grid=(t // tm,),
        in_specs=[pl.BlockSpec((tm, d), lambda i: (i, 0)),
                  pl.BlockSpec((d, n), lambda i: (0, 0)),
                  pl.BlockSpec((1, n), lambda i: (0, 0))],
        out_specs=pl.BlockSpec((tm, n), lambda i: (i, 0)),
        out_shape=jax.ShapeDtypeStruct((t, n), BF16),
        compiler_params=_cparams(("parallel",)),
        name="proj_qa",
    )(xb, w, g.reshape(1, n))


def _proj_kv(xb, w_lat, w_kr, g, cos, sin):
    t, d = xb.shape
    c = w_lat.shape[1]
    r = w_kr.shape[1]
    tm = _pick_tile(t, ROW_TILE, 16)
    rows = lambda w: pl.BlockSpec((tm, w), lambda i: (i, 0))
    full = lambda a, b: pl.BlockSpec((a, b), lambda i: (0, 0))
    return pl.pallas_call(
        _proj_kv_kernel,
        grid=(t // tm,),
        in_specs=[rows(d), full(d, c), full(d, r), full(1, c), rows(r // 2), rows(r // 2)],
        out_specs=[rows(c), rows(r), rows(c), rows(r)],
        out_shape=[jax.ShapeDtypeStruct((t, c), F32), jax.ShapeDtypeStruct((t, r), F32),
                   jax.ShapeDtypeStruct((t, c), BF16), jax.ShapeDtypeStruct((t, r), BF16)],
        compiler_params=_cparams(("parallel",)),
        name="proj_kv",
    )(xb, w_lat, w_kr, g.reshape(1, c), cos, sin)


def _proj_tiled(kernel, xb, w, extra, out_dtypes, name):
    t, d = xb.shape
    n = w.shape[1]
    tm = _pick_tile(t, ROW_TILE, 16)
    tn = _pick_tile(n, COL_TILE, 128)
    out_block = pl.BlockSpec((tm, tn), lambda i, j: (i, j))
    return pl.pallas_call(
        kernel,
        grid=(t // tm, n // tn),
        in_specs=[pl.BlockSpec((tm, d), lambda i, j: (i, 0)),
                  pl.BlockSpec((d, tn), lambda i, j: (0, j))]
                 + [pl.BlockSpec((1, tn), lambda i, j: (0, j)) for _ in extra],
        out_specs=[out_block for _ in out_dtypes],
        out_shape=[jax.ShapeDtypeStruct((t, n), dt) for dt in out_dtypes],
        compiler_params=_cparams(("parallel", "parallel")),
        name=name,
    )(xb, w, *extra)


def _q_heads_kernel(qa_ref, wq_ref, wuk_ref, cos_ref, sin_ref, ql_ref, qr_ref):
    q = _dot(qa_ref[...], wq_ref[...])
    nope = q[:, :NOPE_DIM].astype(BF16)
    ql_ref[...] = (_dot(nope, wuk_ref[...]) * MLA_QSCALE).astype(BF16)
    qr = _rope_rows(q[:, NOPE_DIM:], cos_ref[...], sin_ref[...])
    qr_ref[...] = (qr * MLA_QSCALE).astype(BF16)


def _q_heads(qa_n, wq_h, wuk_h, cos, sin):
    t, ql = qa_n.shape
    h, _, hd = wq_h.shape
    c = wuk_h.shape[2]
    tm = _pick_tile(t, ROW_TILE, 16)
    half = ROPE_DIM // 2
    return pl.pallas_call(
        _q_heads_kernel,
        grid=(t // tm, h),
        in_specs=[pl.BlockSpec((tm, ql), lambda i, j: (i, 0)),
                  pl.BlockSpec((None, ql, hd), lambda i, j: (j, 0, 0)),
                  pl.BlockSpec((None, NOPE_DIM, c), lambda i, j: (j, 0, 0)),
                  pl.BlockSpec((tm, half), lambda i, j: (i, 0)),
                  pl.BlockSpec((tm, half), lambda i, j: (i, 0))],
        out_specs=[pl.BlockSpec((None, tm, c), lambda i, j: (j, i, 0)),
                   pl.BlockSpec((None, tm, ROPE_DIM), lambda i, j: (j, i, 0))],
        out_shape=[jax.ShapeDtypeStruct((h, t, c), BF16),
                   jax.ShapeDtypeStruct((h, t, ROPE_DIM), BF16)],
        compiler_params=_cparams(("parallel", "parallel")),
        name="q_heads",
    )(qa_n, wq_h, wuk_h, cos, sin)


def _mla_update(s, klat, m_ref, l_ref, acc_ref):
    m_prev = m_ref[...]
    m_new = jnp.maximum(m_prev, jnp.max(s, axis=-1, keepdims=True))
    alpha = jnp.exp2(m_prev - m_new)
    p = jnp.exp2(s - m_new)
    l_ref[...] = alpha * l_ref[...] + jnp.sum(p, axis=-1, keepdims=True)
    acc_ref[...] = alpha * acc_ref[...] + _dot(p.astype(BF16), klat)
    m_ref[...] = m_new


def _mla_init(m_ref, l_ref, acc_ref):
    m_ref[...] = jnp.full(m_ref.shape, NEG_BIG, F32)
    l_ref[...] = jnp.zeros(l_ref.shape, F32)
    acc_ref[...] = jnp.zeros(acc_ref.shape, F32)


def _mla_finalize(wuv_ref, o_ref, l_ref, acc_ref, tq):
    inv = 1.0 / l_ref[...]
    for h in range(MLA_HEADS):
        rows = slice(h * tq, (h + 1) * tq)
        ctx = (acc_ref[rows, :] * inv[rows, :]).astype(BF16)
        o_ref[:, h * V_DIM:(h + 1) * V_DIM] = _dot(ctx, wuv_ref[h]).astype(BF16)


def _chunk_visible(q_pos0, k_pos0, rows, tq, tk):
    t = lax.rem(lax.broadcasted_iota(jnp.int32, (rows, 1), 0), tq)
    q_chunk = (q_pos0 + t) // CHUNK
    k_chunk = (k_pos0 + lax.broadcasted_iota(jnp.int32, (1, tk), 1)) // CHUNK
    return k_chunk <= q_chunk


def _mla_prompt_tile(ql_ref, qr_ref, klat, kr, masked, q_pos0, k_pos0,
                     m_ref, l_ref, acc_ref, s_ref, p_ref, a_ref, tq, tk):
    rows = MLA_HEADS * tq
    q_lat = ql_ref[...].reshape(rows, ql_ref.shape[-1])
    q_rope = qr_ref[...].reshape(rows, qr_ref.shape[-1])
    s_ref[...] = _dot_nt(klat, q_lat) + _dot_nt(kr, q_rope)
    if masked:
        k_chunk = (k_pos0 + lax.broadcasted_iota(jnp.int32, (tk, 1), 0)) // CHUNK
    for c in range(rows // MLA_SOFTMAX_COLS):
        cols = slice(c * MLA_SOFTMAX_COLS, (c + 1) * MLA_SOFTMAX_COLS)
        s = s_ref[:, cols]
        if masked:
            t0 = (c * MLA_SOFTMAX_COLS) % tq
            q_chunk = (q_pos0 + t0
                       + lax.broadcasted_iota(jnp.int32, (1, MLA_SOFTMAX_COLS), 1)) // CHUNK
            s = jnp.where(k_chunk <= q_chunk, s, NEG_BIG)
        m_prev = m_ref[:, cols]
        m_new = jnp.maximum(m_prev, jnp.max(s, axis=0, keepdims=True))
        alpha = jnp.exp2(m_prev - m_new)
        p = jnp.exp2(s - m_new)
        l_ref[:, cols] = alpha * l_ref[:, cols] + jnp.sum(p, axis=0, keepdims=True)
        m_ref[:, cols] = m_new
        a_ref[:, cols] = alpha
        p_ref[:, cols] = p.astype(BF16)
    acc_ref[...] = a_ref[...] * acc_ref[...] + lax.dot_general(
        klat, p_ref[...], (((0,), (0,)), ((), ())), preferred_element_type=F32)


def _mla_prompt_finalize(wuv_ref, o_ref, l_ref, acc_ref, tq):
    inv = 1.0 / l_ref[...]
    for h in range(MLA_HEADS):
        cols = slice(h * tq, (h + 1) * tq)
        ctx_t = (acc_ref[:, cols] * inv[:, cols]).astype(BF16)
        out = lax.dot_general(ctx_t, wuv_ref[h], (((0,), (0,)), ((), ())),
                              preferred_element_type=F32)
        o_ref[:, h * V_DIM:(h + 1) * V_DIM] = out.astype(BF16)


def _mla_prompt_kernel(qb_ref, kb_ref, qp_ref, kp_ref, fl_ref,
                       ql_ref, qr_ref, kl_ref, kr_ref, wuv_ref, o_ref,
                       m_ref, l_ref, acc_ref, s_ref, p_ref, a_ref, *, tq, tk):
    n = pl.program_id(0)
    scratch = (m_ref, l_ref, acc_ref, s_ref, p_ref, a_ref)

    @pl.when(fl_ref[n] % 2 == 1)
    def _():
        _mla_init(m_ref, l_ref, acc_ref)

    @pl.when(fl_ref[n] < 2)
    def _():
        _mla_prompt_tile(ql_ref, qr_ref, kl_ref[...], kr_ref[...], False, 0, 0, *scratch, tq, tk)

    @pl.when(fl_ref[n] >= 2)
    def _():
        _mla_prompt_tile(ql_ref, qr_ref, kl_ref[...], kr_ref[...], True, qp_ref[n], kp_ref[n],
                         *scratch, tq, tk)
        _mla_prompt_finalize(wuv_ref, o_ref, l_ref, acc_ref, tq)


def _mla_prompt(q_lat, q_rope, k_lat, k_rope, w_uv, batch, seq):
    h, t, c = q_lat.shape
    tq = _pick_tile(seq, MLA_Q_TILE, CHUNK)
    tk = _pick_tile(seq, MLA_K_TILE, CHUNK)
    nq, nk = seq // tq, seq // tk
    qb, kb, qp, kp, fl = [], [], [], [], []
    for b in range(batch):
        for qi in range(nq):
            last = ((qi + 1) * tq - 1) // tk
            for kj in range(last + 1):
                qb.append(b * nq + qi)
                kb.append(b * nk + kj)
                qp.append(qi * tq)
                kp.append(kj * tk)
                fl.append((1 if kj == 0 else 0) + (2 if kj == last else 0))
    tabs = [jnp.asarray(np.asarray(a, np.int32)) for a in (qb, kb, qp, kp, fl)]
    rows = h * tq
    grid_spec = pltpu.PrefetchScalarGridSpec(
        num_scalar_prefetch=5,
        grid=(len(qb),),
        in_specs=[
            pl.BlockSpec((h, tq, c), lambda n, qb, kb, qp, kp, fl: (0, qb[n], 0)),
            pl.BlockSpec((h, tq, ROPE_DIM), lambda n, qb, kb, qp, kp, fl: (0, qb[n], 0)),
            pl.BlockSpec((tk, c), lambda n, qb, kb, qp, kp, fl: (kb[n], 0)),
            pl.BlockSpec((tk, ROPE_DIM), lambda n, qb, kb, qp, kp, fl: (kb[n], 0)),
            pl.BlockSpec((h, c, V_DIM), lambda n, qb, kb, qp, kp, fl: (0, 0, 0)),
        ],
        out_specs=pl.BlockSpec((tq, h * V_DIM), lambda n, qb, kb, qp, kp, fl: (qb[n], 0)),
        scratch_shapes=[pltpu.VMEM((1, rows), F32), pltpu.VMEM((1, rows), F32),
                        pltpu.VMEM((c, rows), F32), pltpu.VMEM((tk, rows), F32),
                        pltpu.VMEM((tk, rows), BF16), pltpu.VMEM((1, rows), F32)],
    )
    return pl.pallas_call(
        functools.partial(_mla_prompt_kernel, tq=tq, tk=tk),
        grid_spec=grid_spec,
        out_shape=jax.ShapeDtypeStruct((batch * seq, h * V_DIM), BF16),
        compiler_params=_cparams(("arbitrary",)),
        name="mla_prompt",
    )(*tabs, q_lat, q_rope, k_lat, k_rope, w_uv)


def _mla_sample_kernel(ql_ref, qr_ref, pl_ref, pr_ref, nl_ref, nr_ref, wuv_ref, o_ref,
                       m_ref, l_ref, acc_ref, *, tq, n_past, n_kt):
    j = pl.program_id(1)
    rows = MLA_HEADS * tq
    q_lat = ql_ref[...].reshape(rows, ql_ref.shape[-1])
    q_rope = qr_ref[...].reshape(rows, qr_ref.shape[-1])

    @pl.when(j == 0)
    def _():
        _mla_init(m_ref, l_ref, acc_ref)

    @pl.when(j < n_kt)
    def _():
        klat = pl_ref[...].astype(BF16)
        s = _dot_nt(q_lat, klat) + _dot_nt(q_rope, pr_ref[...].astype(BF16))
        _mla_update(s, klat, m_ref, l_ref, acc_ref)

    @pl.when(j == n_kt)
    def _():
        klat = nl_ref[...]
        s = _dot_nt(q_lat, klat) + _dot_nt(q_rope, nr_ref[...])
        s = jnp.where(_chunk_visible(n_past, n_past, rows, tq, tq), s, NEG_BIG)
        _mla_update(s, klat, m_ref, l_ref, acc_ref)
        _mla_finalize(wuv_ref, o_ref, l_ref, acc_ref, tq)


def _mla_sample(q_lat, q_rope, past_lat, past_kr, layer, k_lat, k_rope, w_uv,
                row0, dec_batch, dec_seq):
    h, t, c = q_lat.shape
    n_past = past_lat.shape[2]
    tk = _pick_tile(n_past, MLA_K_TILE, 128)
    n_kt = n_past // tk
    blk0 = row0 // dec_seq
    rows = h * dec_seq
    past_idx = lambda b, j: (layer, b, jnp.minimum(j, n_kt - 1), 0)
    return pl.pallas_call(
        functools.partial(_mla_sample_kernel, tq=dec_seq, n_past=n_past, n_kt=n_kt),
        grid=(dec_batch, n_kt + 1),
        in_specs=[
            pl.BlockSpec((h, dec_seq, c), lambda b, j: (0, blk0 + b, 0)),
            pl.BlockSpec((h, dec_seq, ROPE_DIM), lambda b, j: (0, blk0 + b, 0)),
            pl.BlockSpec((None, None, tk, c), past_idx),
            pl.BlockSpec((None, None, tk, ROPE_DIM), past_idx),
            pl.BlockSpec((dec_seq, c), lambda b, j: (blk0 + b, 0)),
            pl.BlockSpec((dec_seq, ROPE_DIM), lambda b, j: (blk0 + b, 0)),
            pl.BlockSpec((h, c, V_DIM), lambda b, j: (0, 0, 0)),
        ],
        out_specs=pl.BlockSpec((dec_seq, h * V_DIM), lambda b, j: (b, 0)),
        out_shape=jax.ShapeDtypeStruct((dec_batch * dec_seq, h * V_DIM), BF16),
        scratch_shapes=[pltpu.VMEM((rows, 1), F32), pltpu.VMEM((rows, 1), F32),
                        pltpu.VMEM((rows, c), F32)],
        compiler_params=_cparams(("parallel", "arbitrary")),
        name="mla_sample",
    )(q_lat, q_rope, past_lat, past_kr, k_lat, k_rope, w_uv)


def _scan_matrix(sub):
    j = np.arange(sub)[:, None]
    s = np.arange(sub)[None, :]
    u = (j > s).astype(np.float32)
    return jnp.asarray(np.concatenate([u, u], axis=0), dtype=BF16)


_SbChunk = collections.namedtuple("_SbChunk", "q k v before carry_idx acc_idx")


def _sb_scores(qc, kc, before):
    z = _dot_nt(qc, kc)
    softplus = jnp.log(1.0 + jnp.exp2(-jnp.abs(z))) * LOG2E
    log_beta = jnp.minimum(z, 0.0) - softplus
    log_rest = log_beta - z
    if before is not None:
        log_rest = jnp.where(before, log_rest, 0.0)
    hi = log_rest.astype(BF16)
    lo = (log_rest - hi.astype(F32)).astype(BF16)
    return (log_beta, jnp.concatenate([hi, lo], axis=-1),
            jnp.sum(log_rest, axis=-1, keepdims=True))


def _sb_weights(log_beta, hi_lo, u2, carry, before):
    a = jnp.exp2(log_beta + _dot(hi_lo, u2) + carry)
    if before is not None:
        a = jnp.where(before, a, 0.0)
    return a.astype(BF16)


def _sb_pipeline(chunks, u2, acc_ref, carry_ref):
    scored = None
    weighted = None
    for j in range(len(chunks) + 2):
        new_scored = None
        if j < len(chunks):
            ch = chunks[j]
            new_scored = (ch,) + _sb_scores(ch.q(), ch.k(), ch.before)
        new_weighted = None
        if scored is not None:
            ch, log_beta, hi_lo, row_sum = scored
            carry = carry_ref[ch.carry_idx]
            new_weighted = (ch, _sb_weights(log_beta, hi_lo, u2, carry, ch.before))
            carry_ref[ch.carry_idx] = carry + row_sum
        if weighted is not None:
            ch, a = weighted
            acc_ref[ch.acc_idx] += _dot(a, ch.v())
        scored, weighted = new_scored, new_weighted


def _sb_tile(q_ref, k_ref, v_ref, u2, acc_ref, carry_ref, diagonal, tq, tk, sub, rq):
    chunks = []
    for c in reversed(range(tk // sub)):
        k0, k1 = c * sub, (c + 1) * sub
        for r in range(tq // rq):
            r0, r1 = r * rq, (r + 1) * rq
            before = None
            if diagonal:
                if k0 >= r1 - 1:
                    continue
                if k1 > r0:
                    before = ((k0 + lax.broadcasted_iota(jnp.int32, (1, sub), 1))
                              < (r0 + lax.broadcasted_iota(jnp.int32, (rq, 1), 0)))
            for h in range(SB_HEADS):
                cols = slice(h * SB_DIM, (h + 1) * SB_DIM)
                rows, keys = slice(r0, r1), slice(k0, k1)
                chunks.append(_SbChunk(
                    q=functools.partial(lambda rr, cc: q_ref[rr, cc], rows, cols),
                    k=functools.partial(lambda kk, cc: k_ref[kk, cc], keys, cols),
                    v=functools.partial(lambda kk, cc: v_ref[kk, cc], keys, cols),
                    before=before, carry_idx=(h, rows, slice(None)), acc_idx=(rows, cols)))
    _sb_pipeline(chunks, u2, acc_ref, carry_ref)


def _sb_prompt_kernel(qb_ref, kb_ref, fl_ref,
                      q_ref, k_ref, v_ref, u_ref, o_ref, acc_ref, carry_ref, live_ref,
                      *, tq, tk, sub, rq):
    n = pl.program_id(0)

    @pl.when(fl_ref[n] % 2 == 1)
    def _():
        acc_ref[...] = jnp.zeros(acc_ref.shape, F32)
        carry_ref[...] = jnp.zeros(carry_ref.shape, F32)
        _sb_tile(q_ref, k_ref, v_ref, u_ref[...], acc_ref, carry_ref, True, tq, tk, sub, rq)
        live_ref[0] = (jnp.max(carry_ref[...]) > SB_DEAD_LOG2).astype(jnp.int32)

    @pl.when(jnp.logical_and(fl_ref[n] % 2 == 0, live_ref[0] == 1))
    def _():
        _sb_tile(q_ref, k_ref, v_ref, u_ref[...], acc_ref, carry_ref, False, tq, tk, sub, rq)
        live_ref[0] = (jnp.max(carry_ref[...]) > SB_DEAD_LOG2).astype(jnp.int32)

    @pl.when(fl_ref[n] >= 2)
    def _():
        o_ref[...] = acc_ref[...].astype(BF16)


def _sb_prompt(q, k, v, batch, seq):
    width = q.shape[1]
    tq = _pick_tile(seq, SB_TILE, 128)
    tk = tq
    sub = _pick_tile(tk, SB_SUB, 128)
    rq = _pick_tile(tq, SB_ROWS, 16)
    nq = seq // tq
    qb, kb, fl = [], [], []
    for b in range(batch):
        for qi in range(nq):
            for kj in range(qi, -1, -1):
                qb.append(b * nq + qi)
                kb.append(b * nq + kj)
                fl.append((1 if kj == qi else 0) + (2 if kj == 0 else 0))
    tabs = [jnp.asarray(np.asarray(a, np.int32)) for a in (qb, kb, fl)]
    grid_spec = pltpu.PrefetchScalarGridSpec(
        num_scalar_prefetch=3,
        grid=(len(qb),),
        in_specs=[
            pl.BlockSpec((tq, width), lambda n, qb, kb, fl: (qb[n], 0)),
            pl.BlockSpec((tk, width), lambda n, qb, kb, fl: (kb[n], 0)),
            pl.BlockSpec((tk, width), lambda n, qb, kb, fl: (kb[n], 0)),
            pl.BlockSpec((2 * sub, sub), lambda n, qb, kb, fl: (0, 0)),
        ],
        out_specs=pl.BlockSpec((tq, width), lambda n, qb, kb, fl: (qb[n], 0)),
        scratch_shapes=[pltpu.VMEM((tq, width), F32), pltpu.VMEM((SB_HEADS, tq, 1), F32),
                        pltpu.SMEM((1,), jnp.int32)],
    )
    return pl.pallas_call(
        functools.partial(_sb_prompt_kernel, tq=tq, tk=tk, sub=sub, rq=rq),
        grid_spec=grid_spec,
        out_shape=jax.ShapeDtypeStruct((batch * seq, width), BF16),
        compiler_params=_cparams(("arbitrary",)),
        name="sb_prompt",
    )(*tabs, q, k, v, _scan_matrix(sub))


def _sb_sample_kernel(q_ref, pk_ref, pv_ref, nk_ref, nv_ref, un_ref, up_ref, o_ref,
                      acc_ref, carry_ref, live_ref, *, tq, tk, sub, n_kt):
    j = pl.program_id(1)

    def head_chunks(k_ref, v_ref, keys, before):
        chunks = []
        for h in range(SB_HEADS):
            cols = slice(h * SB_DIM, (h + 1) * SB_DIM)
            chunks.append(_SbChunk(
                q=functools.partial(lambda cc: q_ref[:, cc], cols),
                k=functools.partial(lambda cc: k_ref[keys, cc].astype(BF16), cols),
                v=functools.partial(lambda cc: v_ref[keys, cc].astype(BF16), cols),
                before=before, carry_idx=(h, slice(None), slice(None)),
                acc_idx=(slice(None), cols)))
        return chunks

    @pl.when(j == 0)
    def _():
        acc_ref[...] = jnp.zeros(acc_ref.shape, F32)
        carry_ref[...] = jnp.zeros(carry_ref.shape, F32)
        before = (lax.broadcasted_iota(jnp.int32, (1, tq), 1)
                  < lax.broadcasted_iota(jnp.int32, (tq, 1), 0))
        _sb_pipeline(head_chunks(nk_ref, nv_ref, slice(None), before), un_ref[...],
                     acc_ref, carry_ref)
        live_ref[0] = (jnp.max(carry_ref[...]) > SB_DEAD_LOG2).astype(jnp.int32)

    @pl.when(jnp.logical_and(j > 0, live_ref[0] == 1))
    def _():
        chunks = []
        for c in reversed(range(tk // sub)):
            chunks += head_chunks(pk_ref, pv_ref, slice(c * sub, (c + 1) * sub), None)
        _sb_pipeline(chunks, up_ref[...], acc_ref, carry_ref)
        live_ref[0] = (jnp.max(carry_ref[...]) > SB_DEAD_LOG2).astype(jnp.int32)

    @pl.when(j == n_kt)
    def _():
        o_ref[...] = acc_ref[...].astype(BF16)


def _sb_sample(q, past_k, past_v, layer, k, v, row0, dec_batch, dec_seq):
    width = q.shape[1]
    n_past = past_k.shape[2]
    tk = _pick_tile(n_past, SB_TILE, 128)
    sub = _pick_tile(tk, SB_SUB, 128)
    n_kt = n_past // tk
    blk0 = row0 // dec_seq
    past_idx = lambda b, j: (layer, b, n_kt - jnp.maximum(j, 1), 0)
    new_rows = pl.BlockSpec((dec_seq, width), lambda b, j: (blk0 + b, 0))
    return pl.pallas_call(
        functools.partial(_sb_sample_kernel, tq=dec_seq, tk=tk, sub=sub, n_kt=n_kt),
        grid=(dec_batch, n_kt + 1),
        in_specs=[
            new_rows,
            pl.BlockSpec((None, None, tk, width), past_idx),
            pl.BlockSpec((None, None, tk, width), past_idx),
            new_rows,
            new_rows,
            pl.BlockSpec((2 * dec_seq, dec_seq), lambda b, j: (0, 0)),
            pl.BlockSpec((2 * sub, sub), lambda b, j: (0, 0)),
        ],
        out_specs=pl.BlockSpec((dec_seq, width), lambda b, j: (b, 0)),
        out_shape=jax.ShapeDtypeStruct((dec_batch * dec_seq, width), BF16),
        scratch_shapes=[pltpu.VMEM((dec_seq, width), F32),
                        pltpu.VMEM((SB_HEADS, dec_seq, 1), F32),
                        pltpu.SMEM((1,), jnp.int32)],
        compiler_params=_cparams(("parallel", "arbitrary")),
        name="sb_sample",
    )(q, past_k, past_v, k, v, _scan_matrix(dec_seq), _scan_matrix(sub))


def _merge_kernel(oa_ref, ob_ref, wa_ref, wb_ref, ga_ref, gb_ref, o_ref):
    a = _dot(oa_ref[...], wa_ref[...])
    b = _dot(ob_ref[...], wb_ref[...])
    o_ref[...] = (ga_ref[...].astype(F32) * a + gb_ref[...].astype(F32) * b).astype(BF16)


def _merge(o_a, o_b, w_a, w_b, gates):
    t, da = o_a.shape
    db = o_b.shape[1]
    n = w_a.shape[1]
    tm = _pick_tile(t, ROW_TILE, 16)
    tn = _pick_tile(n, COL_TILE, 128)
    nj = n // tn
    return pl.pallas_call(
        _merge_kernel,
        grid=(t // tm, nj),
        in_specs=[pl.BlockSpec((tm, da), lambda i, j: (i, 0)),
                  pl.BlockSpec((tm, db), lambda i, j: (i, 0)),
                  pl.BlockSpec((da, tn), lambda i, j: (0, j)),
                  pl.BlockSpec((db, tn), lambda i, j: (0, j)),
                  pl.BlockSpec((tm, tn), lambda i, j: (i, j)),
                  pl.BlockSpec((tm, tn), lambda i, j: (i, j + nj))],
        out_specs=pl.BlockSpec((tm, tn), lambda i, j: (i, j)),
        out_shape=jax.ShapeDtypeStruct((t, n), BF16),
        compiler_params=_cparams(("parallel", "parallel")),
        name="merge",
    )(o_a, o_b, w_a, w_b, gates, gates)


def _res_ln_finalize(z_ref, g_ref, b_ref, o_ref, ob_ref):
    n_tiles, _, tn = z_ref.shape
    width = n_tiles * tn
    total = z_ref[0].sum(axis=-1, keepdims=True)
    for c in range(1, n_tiles):
        total += z_ref[c].sum(axis=-1, keepdims=True)
    mu = total / width
    sq = jnp.square(z_ref[0] - mu).sum(axis=-1, keepdims=True)
    for c in range(1, n_tiles):
        sq += jnp.square(z_ref[c] - mu).sum(axis=-1, keepdims=True)
    rstd = lax.rsqrt(sq / width + LN_EPS)
    for c in range(n_tiles):
        cols = slice(c * tn, (c + 1) * tn)
        y = (z_ref[c] - mu) * rstd * g_ref[:, cols] + b_ref[:, cols]
        o_ref[:, cols] = y
        ob_ref[:, cols] = y.astype(BF16)


def _out_ln_kernel(m_ref, w_ref, x_ref, g_ref, b_ref, o_ref, ob_ref, z_ref, *, alpha):
    j = pl.program_id(1)
    z_ref[j] = alpha * x_ref[...] + _dot(m_ref[...], w_ref[...])

    @pl.when(j == pl.num_programs(1) - 1)
    def _():
        _res_ln_finalize(z_ref, g_ref, b_ref, o_ref, ob_ref)


def _out_ln(merged, w, x, g, b, alpha):
    t, d = x.shape
    k = merged.shape[1]
    tm = _pick_tile(t, ROW_TILE, 16)
    tn = _pick_tile(d, COL_TILE, 128)
    full_row = pl.BlockSpec((tm, d), lambda i, j: (i, 0))
    vec = pl.BlockSpec((1, d), lambda i, j: (0, 0))
    return pl.pallas_call(
        functools.partial(_out_ln_kernel, alpha=alpha),
        grid=(t // tm, d // tn),
        in_specs=[pl.BlockSpec((tm, k), lambda i, j: (i, 0)),
                  pl.BlockSpec((k, tn), lambda i, j: (0, j)),
                  pl.BlockSpec((tm, tn), lambda i, j: (i, j)),
                  vec, vec],
        out_specs=[full_row, full_row],
        out_shape=[jax.ShapeDtypeStruct((t, d), F32), jax.ShapeDtypeStruct((t, d), BF16)],
        scratch_shapes=[pltpu.VMEM((d // tn, tm, tn), F32)],
        compiler_params=_cparams(("parallel", "arbitrary")),
        name="out_ln",
    )(merged, w, x, g.reshape(1, d), b.reshape(1, d))


def _ffn_ln_kernel(xb_ref, wg_ref, wu_ref, wd_ref, x_ref, g_ref, b_ref, o_ref, ob_ref,
                   acc_ref, *, alpha):
    f = pl.program_id(1)

    @pl.when(f == 0)
    def _():
        acc_ref[...] = jnp.zeros(acc_ref.shape, F32)

    xb = xb_ref[...]
    gate = _dot(xb, wg_ref[...])
    up = _dot(xb, wu_ref[...])
    hidden = (gate * _sigmoid(gate) * up).astype(BF16)
    acc_ref[...] += _dot(hidden, wd_ref[...])

    @pl.when(f == pl.num_programs(1) - 1)
    def _():
        y = _layer_norm_rows(alpha * x_ref[...] + acc_ref[...], g_ref[...], b_ref[...])
        o_ref[...] = y
        ob_ref[...] = y.astype(BF16)


def _ffn_ln(xb, w_gu, w_down, x, g, b, alpha):
    t, d = x.shape
    ff = w_down.shape[0]
    tm = _pick_tile(t, ROW_TILE, 16)
    tf = _pick_tile(ff, COL_TILE, 128)
    nf = ff // tf
    full_row = pl.BlockSpec((tm, d), lambda i, f: (i, 0))
    vec = pl.BlockSpec((1, d), lambda i, f: (0, 0))
    return pl.pallas_call(
        functools.partial(_ffn_ln_kernel, alpha=alpha),
        grid=(t // tm, nf),
        in_specs=[full_row,
                  pl.BlockSpec((d, tf), lambda i, f: (0, f)),
                  pl.BlockSpec((d, tf), lambda i, f: (0, f + nf)),
                  pl.BlockSpec((tf, d), lambda i, f: (f, 0)),
                  full_row, vec, vec],
        out_specs=[full_row, full_row],
        out_shape=[jax.ShapeDtypeStruct((t, d), F32), jax.ShapeDtypeStruct((t, d), BF16)],
        scratch_shapes=[pltpu.VMEM((tm, d), F32)],
        compiler_params=_cparams(("parallel", "arbitrary")),
        name="ffn_ln",
    )(xb, w_gu, w_gu, w_down, x, g.reshape(1, d), b.reshape(1, d))


def _ple_ln_kernel(xb_ref, wg_ref, bg_ref, p_ref, wp_ref, x_ref, g_ref, b_ref, o_ref, ob_ref,
                   z_ref, *, alpha):
    j = pl.program_id(1)
    gate = _sigmoid(_dot(xb_ref[...], wg_ref[...]) + bg_ref[...])
    proj = _dot(p_ref[...].astype(BF16), wp_ref[...])
    z_ref[j] = alpha * x_ref[...] + gate * proj

    @pl.when(j == pl.num_programs(1) - 1)
    def _():
        _res_ln_finalize(z_ref, g_ref, b_ref, o_ref, ob_ref)


def _ple_ln(xb, w_gate, b_gate, p, w_proj, x, g, b, alpha):
    t, d = x.shape
    pd = p.shape[1]
    tm = _pick_tile(t, ROW_TILE, 16)
    tn = _pick_tile(d, COL_TILE, 128)
    full_row = pl.BlockSpec((tm, d), lambda i, j: (i, 0))
    vec = pl.BlockSpec((1, d), lambda i, j: (0, 0))
    return pl.pallas_call(
        functools.partial(_ple_ln_kernel, alpha=alpha),
        grid=(t // tm, d // tn),
        in_specs=[full_row,
                  pl.BlockSpec((d, tn), lambda i, j: (0, j)),
                  pl.BlockSpec((1, tn), lambda i, j: (0, j)),
                  pl.BlockSpec((tm, pd), lambda i, j: (i, 0)),
                  pl.BlockSpec((pd, tn), lambda i, j: (0, j)),
                  pl.BlockSpec((tm, tn), lambda i, j: (i, j)),
                  vec, vec],
        out_specs=[full_row, full_row],
        out_shape=[jax.ShapeDtypeStruct((t, d), F32), jax.ShapeDtypeStruct((t, d), BF16)],
        scratch_shapes=[pltpu.VMEM((d // tn, tm, tn), F32)],
        compiler_params=_cparams(("parallel", "arbitrary")),
        name="ple_ln",
    )(xb, w_gate, b_gate.reshape(1, d), p, w_proj, x, g.reshape(1, d), b.reshape(1, d))


def _rope_tables(pos):
    half = ROPE_DIM // 2
    inv_freq = 1.0 / (ROPE_THETA ** (jnp.arange(half, dtype=F32) * (2.0 / ROPE_DIM)))
    ang = pos.astype(F32)[:, None] * inv_freq[None, :]
    return jnp.cos(ang), jnp.sin(ang)


def kernel(x_prompt, x_sample, cache_mla_latent, cache_mla_krope, cache_sb_k, cache_sb_v,
           p_prompt, p_sample, ln_in_g, ln_in_b, w_in, b_gate, q_a_norm_g, w_q_b,
           kv_a_norm_g, w_kv_b, w_branch_a, w_branch_b, w_out, ln1_g, ln1_b,
           w_ffn_gu, w_ffn_down, ln2_g, ln2_b, w_ple_gate, b_ple_gate, w_ple_proj,
           ln3_g, ln3_b):
    batch, seq, d = x_prompt.shape
    dec_batch, dec_seq, _ = x_sample.shape
    depth = w_in.shape[0]
    n_past = cache_mla_latent.shape[2]
    tp = batch * seq
    ts = dec_batch * dec_seq
    sb_width = SB_HEADS * SB_DIM
    alpha = (2 * depth) ** 0.25

    pos = jnp.concatenate([jnp.tile(jnp.arange(seq, dtype=jnp.int32), batch),
                           jnp.tile(n_past + jnp.arange(dec_seq, dtype=jnp.int32), dec_batch)])
    cos, sin = _rope_tables(pos)

    past_k = cache_sb_k.reshape(depth, dec_batch, n_past, sb_width)
    past_v = cache_sb_v.reshape(depth, dec_batch, n_past, sb_width)

    x_all = jnp.concatenate([x_prompt.reshape(tp, d), x_sample.reshape(ts, d)], axis=0)
    x, xb = _ln_in(x_all, ln_in_g, ln_in_b)

    splits = np.cumsum([0, Q_LORA, KV_LORA, ROPE_DIM, sb_width, sb_width, sb_width, 2 * d])
    new_lat, new_kr, new_k, new_v = [], [], [], []
    for l in range(depth):
        w_cols = [w_in[l][:, splits[i]:splits[i + 1]].astype(BF16) for i in range(7)]
        w_qa, w_lat, w_kr, w_sq, w_sk, w_sv, w_gates = w_cols
        wq_h = w_q_b[l].reshape(Q_LORA, MLA_HEADS, NOPE_DIM + ROPE_DIM)
        wq_h = wq_h.transpose(1, 0, 2).astype(BF16)
        w_kv = w_kv_b[l].reshape(KV_LORA, MLA_HEADS, NOPE_DIM + V_DIM)
        wuk_h = w_kv[..., :NOPE_DIM].transpose(1, 2, 0).astype(BF16)
        wuv_h = w_kv[..., NOPE_DIM:].transpose(1, 0, 2).astype(BF16)

        qa_n = _proj_qa(xb, w_qa, q_a_norm_g[l])
        lat, kr, lat_b, kr_b = _proj_kv(xb, w_lat, w_kr, kv_a_norm_g[l], cos, sin)
        (sbq,) = _proj_tiled(functools.partial(_proj_scaled_kernel, scale=SB_QSCALE),
                             xb, w_sq, [], [BF16], "proj_sbq")
        sbk, sbk_b = _proj_tiled(_proj_dual_kernel, xb, w_sk, [], [F32, BF16], "proj_sbk")
        sbv, sbv_b = _proj_tiled(_proj_dual_kernel, xb, w_sv, [], [F32, BF16], "proj_sbv")
        (gates,) = _proj_tiled(_proj_gate_kernel, xb, w_gates, [b_gate[l].reshape(1, 2 * d)],
                               [BF16], "proj_gates")

        q_lat, q_rope = _q_heads(qa_n, wq_h, wuk_h, cos, sin)
        oa_p = _mla_prompt(q_lat, q_rope, lat_b, kr_b, wuv_h, batch, seq)
        oa_s = _mla_sample(q_lat, q_rope, cache_mla_latent, cache_mla_krope, l, lat_b, kr_b,
                           wuv_h, tp, dec_batch, dec_seq)
        ob_p = _sb_prompt(sbq, sbk_b, sbv_b, batch, seq)
        ob_s = _sb_sample(sbq, past_k, past_v, l, sbk_b, sbv_b, tp, dec_batch, dec_seq)
        o_a = jnp.concatenate([oa_p, oa_s], axis=0)
        o_b = jnp.concatenate([ob_p, ob_s], axis=0)

        merged = _merge(o_a, o_b, w_branch_a[l].astype(BF16), w_branch_b[l].astype(BF16), gates)
        x, xb = _out_ln(merged, w_out[l].astype(BF16), x, ln1_g[l], ln1_b[l], alpha)
        x, xb = _ffn_ln(xb, w_ffn_gu[l].astype(BF16), w_ffn_down[l].astype(BF16), x,
                        ln2_g[l], ln2_b[l], alpha)
        p_all = jnp.concatenate([p_prompt[l].reshape(tp, -1), p_sample[l].reshape(ts, -1)], axis=0)
        x, xb = _ple_ln(xb, w_ple_gate[l].astype(BF16), b_ple_gate[l], p_all,
                        w_ple_proj[l].astype(BF16), x, ln3_g[l], ln3_b[l], alpha)

        new_lat.append(lat)
        new_kr.append(kr)
        new_k.append(sbk)
        new_v.append(sbv)

    def split(rows_list, tail):
        stacked = jnp.stack(rows_list)
        return (stacked[:, :tp].reshape(depth, batch, seq, *tail),
                stacked[:, tp:].reshape(depth, dec_batch, dec_seq, *tail))

    lat_p, lat_s = split(new_lat, (KV_LORA,))
    kr_p, kr_s = split(new_kr, (ROPE_DIM,))
    k_p, k_s = split(new_k, (SB_HEADS, SB_DIM))
    v_p, v_s = split(new_v, (SB_HEADS, SB_DIM))
    y_prompt = x[:tp].reshape(batch, seq, d)
    y_sample = x[tp:].reshape(dec_batch, dec_seq, d)
    return (y_prompt, y_sample, lat_p, kr_p, k_p, v_p, lat_s, kr_s, k_s, v_s)
```

```python
import collections
import functools
import math

import numpy as np
import jax
import jax.numpy as jnp
from jax import lax
from jax.experimental import pallas as pl
from jax.experimental.pallas import tpu as pltpu

CHUNK = 64
MLA_HEADS = 16
Q_LORA = 512
KV_LORA = 512
NOPE_DIM = 128
ROPE_DIM = 64
V_DIM = 128
ROPE_THETA = 10000.0
MLA_SCALE = (NOPE_DIM + ROPE_DIM) ** -0.5
SB_HEADS = 8
SB_DIM = 128
SB_SCALE = SB_DIM ** -0.5
LN_EPS = 1e-5
RMS_EPS = 1e-6
LOG2E = 1.4426950408889634
MLA_QSCALE = MLA_SCALE * LOG2E
SB_QSCALE = SB_SCALE * LOG2E
SB_DEAD_LOG2 = -160.0

BF16 = jnp.bfloat16
F32 = jnp.float32

V7X_VMEM_LIMIT_BYTES = 56 * 1024 * 1024
ROW_TILE = 512
COL_TILE = 512
PROJ_COL_TILE = 1024
MLA_Q_TILE = 128
MLA_K_TILE = 512
MLA_SOFTMAX_COLS = 128
SB_TILE = 512
SB_SUB = 256
SB_ROWS = 128
NEG_BIG = -1e30


def _pick_tile(n, target, mult):
    best = None
    for t in range(mult, min(n, target) + 1, mult):
        if n % t == 0:
            best = t
    if best is None:
        return n
    return best


def _cparams(sem):
    return pltpu.CompilerParams(dimension_semantics=sem,
                                vmem_limit_bytes=V7X_VMEM_LIMIT_BYTES)


def _dot(a, b):
    return jnp.dot(a, b, preferred_element_type=F32)


def _dot_nt(a, b):
    return lax.dot_general(a, b, (((1,), (1,)), ((), ())), preferred_element_type=F32)


def _layer_norm_rows(z, g, b):
    mu = jnp.mean(z, axis=-1, keepdims=True)
    zc = z - mu
    var = jnp.mean(zc * zc, axis=-1, keepdims=True)
    return zc * lax.rsqrt(var + LN_EPS) * g + b


def _rms_norm_rows(z, g):
    return z * lax.rsqrt(jnp.mean(z * z, axis=-1, keepdims=True) + RMS_EPS) * g


def _sigmoid(z):
    return 1.0 / (1.0 + jnp.exp(-z))


def _rope_rows(x, cos, sin):
    half = ROPE_DIM // 2
    x1 = x[:, :half]
    x2 = x[:, half:]
    return jnp.concatenate([x1 * cos - x2 * sin, x2 * cos + x1 * sin], axis=-1)


def _split_specs(tm, width, n_p):
    return (pl.BlockSpec((tm, width), lambda i, *_: (jnp.minimum(i, n_p - 1), 0)),
            pl.BlockSpec((tm, width), lambda i, *_: (jnp.maximum(i - n_p, 0), 0)))


def _load_split(i, n_p, p_ref, s_ref):
    return jnp.where(i < n_p, p_ref[...], s_ref[...])


def _store_split(i, n_p, p_ref, s_ref, value):
    @pl.when(i < n_p)
    def _():
        p_ref[...] = value

    @pl.when(i >= n_p)
    def _():
        s_ref[...] = value


def _ln_in_kernel(xp_ref, xs_ref, g_ref, b_ref, o_ref, ob_ref, *, n_p):
    x = _load_split(pl.program_id(0), n_p, xp_ref, xs_ref)
    y = _layer_norm_rows(x, g_ref[...], b_ref[...])
    o_ref[...] = y
    ob_ref[...] = y.astype(BF16)


def _ln_in(x_p, x_s, g, b, tm):
    d = x_p.shape[1]
    t = x_p.shape[0] + x_s.shape[0]
    n_p = x_p.shape[0] // tm
    row = pl.BlockSpec((tm, d), lambda i: (i, 0))
    vec = pl.BlockSpec((1, d), lambda i: (0, 0))
    return pl.pallas_call(
        functools.partial(_ln_in_kernel, n_p=n_p),
        grid=(t // tm,),
        in_specs=[*_split_specs(tm, d, n_p), vec, vec],
        out_specs=[row, row],
        out_shape=[jax.ShapeDtypeStruct((t, d), F32), jax.ShapeDtypeStruct((t, d), BF16)],
        compiler_params=_cparams(("parallel",)),
        name="ln_in",
    )(x_p, x_s, g.reshape(1, d), b.reshape(1, d))


def _proj_qa_kernel(x_ref, w_ref, g_ref, o_ref):
    o_ref[...] = _rms_norm_rows(_dot(x_ref[...], w_ref[...]), g_ref[...]).astype(BF16)


def _proj_kv_kernel(x_ref, wl_ref, wr_ref, g_ref, cos_ref, sin_ref,
                    latp_ref, lats_ref, krp_ref, krs_ref, latb_ref, krb_ref, *, n_p):
    i = pl.program_id(0)
    x = x_ref[...]
    lat = _rms_norm_rows(_dot(x, wl_ref[...]), g_ref[...])
    kr = _rope_rows(_dot(x, wr_ref[...]), cos_ref[...], sin_ref[...])
    _store_split(i, n_p, latp_ref, lats_ref, lat)
    _store_split(i, n_p, krp_ref, krs_ref, kr)
    latb_ref[...] = lat.astype(BF16)
    krb_ref[...] = kr.astype(BF16)


def _proj_sb_kernel(x_ref, wq_ref, wk_ref, wv_ref,
                    q_ref, kp_ref, ks_ref, vp_ref, vs_ref, kb_ref, vb_ref, *, n_p):
    i = pl.program_id(0)
    x = x_ref[...]
    q_ref[...] = (_dot(x, wq_ref[...]) * SB_QSCALE).astype(BF16)
    k = _dot(x, wk_ref[...])
    _store_split(i, n_p, kp_ref, ks_ref, k)
    kb_ref[...] = k.astype(BF16)
    v = _dot(x, wv_ref[...])
    _store_split(i, n_p, vp_ref, vs_ref, v)
    vb_ref[...] = v.astype(BF16)


def _proj_gate_kernel(x_ref, w_ref, b_ref, o_ref):
    o_ref[...] = _sigmoid(_dot(x_ref[...], w_ref[...]) + b_ref[...]).astype(BF16)


def _proj_qa(xb, w, g, tm):
    t, d = xb.shape
    n = w.shape[1]
    return pl.pallas_call(
        _proj_qa_kernel,
        grid=(t // tm,),
        in_specs=[pl.BlockSpec((tm, d), lambda i: (i, 0)),
                  pl.BlockSpec((d, n), lambda i: (0, 0)),
                  pl.BlockSpec((1, n), lambda i: (0, 0))],
        out_specs=pl.BlockSpec((tm, n), lambda i: (i, 0)),
        out_shape=jax.ShapeDtypeStruct((t, n), BF16),
        compiler_params=_cparams(("parallel",)),
        name="proj_qa",
    )(xb, w, g.reshape(1, n))


def _proj_kv(xb, w_lat, w_kr, g, cos, sin, tm, tp):
    t, d = xb.shape
    c = w_lat.shape[1]
    r = w_kr.shape[1]
    n_p = tp // tm
    rows = lambda w: pl.BlockSpec((tm, w), lambda i: (i, 0))
    full = lambda a, b: pl.BlockSpec((a, b), lambda i: (0, 0))
    f32 = lambda n, w: jax.ShapeDtypeStruct((n, w), F32)
    return pl.pallas_call(
        functools.partial(_proj_kv_kernel, n_p=n_p),
        grid=(t // tm,),
        in_specs=[rows(d), full(d, c), full(d, r), full(1, c), rows(r // 2), rows(r // 2)],
        out_specs=[*_split_specs(tm, c, n_p), *_split_specs(tm, r, n_p), rows(c), rows(r)],
        out_shape=[f32(tp, c), f32(t - tp, c), f32(tp, r), f32(t - tp, r),
                   jax.ShapeDtypeStruct((t, c), BF16), jax.ShapeDtypeStruct((t, r), BF16)],
        compiler_params=_cparams(("arbitrary",)),
        name="proj_kv",
    )(xb, w_lat, w_kr, g.reshape(1, c), cos, sin)


def _proj_sb(xb, w_q, w_k, w_v, tm, tp):
    t, d = xb.shape
    n = w_q.shape[1]
    n_p = tp // tm
    rows = lambda w: pl.BlockSpec((tm, w), lambda i: (i, 0))
    weight = pl.BlockSpec((d, n), lambda i: (0, 0), pipeline_mode=pl.Buffered(1))
    f32 = lambda m: jax.ShapeDtypeStruct((m, n), F32)
    bf16 = jax.ShapeDtypeStruct((t, n), BF16)
    return pl.pallas_call(
        functools.partial(_proj_sb_kernel, n_p=n_p),
        grid=(t // tm,),
        in_specs=[rows(d), weight, weight, weight],
        out_specs=[rows(n), *_split_specs(tm, n, n_p), *_split_specs(tm, n, n_p), rows(n), rows(n)],
        out_shape=[bf16, f32(tp), f32(t - tp), f32(tp), f32(t - tp), bf16, bf16],
        compiler_params=_cparams(("arbitrary",)),
        name="proj_sb",
    )(xb, w_q, w_k, w_v)


def _proj_tiled(kernel, xb, w, extra, out_dtypes, name, tm):
    t, d = xb.shape
    n = w.shape[1]
    tn = _pick_tile(n, PROJ_COL_TILE, 128)
    out_block = pl.BlockSpec((tm, tn), lambda i, j: (i, j))
    return pl.pallas_call(
        kernel,
        grid=(t // tm, n // tn),
        in_specs=[pl.BlockSpec((tm, d), lambda i, j: (i, 0)),
                  pl.BlockSpec((d, tn), lambda i, j: (0, j))]
                 + [pl.BlockSpec((1, tn), lambda i, j: (0, j)) for _ in extra],
        out_specs=[out_block for _ in out_dtypes],
        out_shape=[jax.ShapeDtypeStruct((t, n), dt) for dt in out_dtypes],
        compiler_params=_cparams(("parallel", "parallel")),
        name=name,
    )(xb, w, *extra)


def _q_heads_kernel(qa_ref, wn_ref, wr_ref, wuk_ref, cos_ref, sin_ref, ql_ref, qr_ref):
    qa = qa_ref[...]
    q_nope = _dot(qa, wn_ref[...]).astype(BF16)
    q_rope = _dot(qa, wr_ref[...])
    cos = cos_ref[...]
    sin = sin_ref[...]
    for h in range(MLA_HEADS):
        nope = q_nope[:, h * NOPE_DIM:(h + 1) * NOPE_DIM]
        ql_ref[h] = (_dot(nope, wuk_ref[h]) * MLA_QSCALE).astype(BF16)
        qr = _rope_rows(q_rope[:, h * ROPE_DIM:(h + 1) * ROPE_DIM], cos, sin)
        qr_ref[h] = (qr * MLA_QSCALE).astype(BF16)


def _q_heads(qa_n, w_nope, w_rope, wuk_h, cos, sin, tm):
    t, ql = qa_n.shape
    h, _, c = wuk_h.shape
    half = ROPE_DIM // 2
    return pl.pallas_call(
        _q_heads_kernel,
        grid=(t // tm,),
        in_specs=[pl.BlockSpec((tm, ql), lambda i: (i, 0)),
                  pl.BlockSpec((ql, h * NOPE_DIM), lambda i: (0, 0)),
                  pl.BlockSpec((ql, h * ROPE_DIM), lambda i: (0, 0)),
                  pl.BlockSpec((h, NOPE_DIM, c), lambda i: (0, 0, 0)),
                  pl.BlockSpec((tm, half), lambda i: (i, 0)),
                  pl.BlockSpec((tm, half), lambda i: (i, 0))],
        out_specs=[pl.BlockSpec((h, tm, c), lambda i: (0, i, 0)),
                   pl.BlockSpec((h, tm, ROPE_DIM), lambda i: (0, i, 0))],
        out_shape=[jax.ShapeDtypeStruct((h, t, c), BF16),
                   jax.ShapeDtypeStruct((h, t, ROPE_DIM), BF16)],
        compiler_params=_cparams(("parallel",)),
        name="q_heads",
    )(qa_n, w_nope, w_rope, wuk_h, cos, sin)


def _mla_update(s, klat, m_ref, l_ref, acc_ref):
    m_prev = m_ref[...]
    m_new = jnp.maximum(m_prev, jnp.max(s, axis=-1, keepdims=True))
    alpha = jnp.exp2(m_prev - m_new)
    p = jnp.exp2(s - m_new)
    l_ref[...] = alpha * l_ref[...] + jnp.sum(p, axis=-1, keepdims=True)
    acc_ref[...] = alpha * acc_ref[...] + _dot(p.astype(BF16), klat)
    m_ref[...] = m_new


def _mla_init(m_ref, l_ref, acc_ref):
    m_ref[...] = jnp.full(m_ref.shape, NEG_BIG, F32)
    l_ref[...] = jnp.zeros(l_ref.shape, F32)
    acc_ref[...] = jnp.zeros(acc_ref.shape, F32)


def _mla_finalize(wuv_ref, o_ref, l_ref, acc_ref, tq):
    inv = 1.0 / l_ref[...]
    for h in range(MLA_HEADS):
        rows = slice(h * tq, (h + 1) * tq)
        ctx = (acc_ref[rows, :] * inv[rows, :]).astype(BF16)
        o_ref[:, h * V_DIM:(h + 1) * V_DIM] = _dot(ctx, wuv_ref[h]).astype(BF16)


def _chunk_visible(q_pos0, k_pos0, rows, tq, tk):
    t = lax.rem(lax.broadcasted_iota(jnp.int32, (rows, 1), 0), tq)
    q_chunk = (q_pos0 + t) // CHUNK
    k_chunk = (k_pos0 + lax.broadcasted_iota(jnp.int32, (1, tk), 1)) // CHUNK
    return k_chunk <= q_chunk


def _mla_prompt_tile(ql_ref, qr_ref, klat, kr, masked, q_pos0, k_pos0,
                     m_ref, l_ref, acc_ref, s_ref, p_ref, a_ref, tq, tk):
    rows = MLA_HEADS * tq
    q_lat = ql_ref[...].reshape(rows, ql_ref.shape[-1])
    q_rope = qr_ref[...].reshape(rows, qr_ref.shape[-1])
    s_ref[...] = _dot_nt(klat, q_lat) + _dot_nt(kr, q_rope)
    if masked:
        k_chunk = (k_pos0 + lax.broadcasted_iota(jnp.int32, (tk, 1), 0)) // CHUNK
    for c in range(rows // MLA_SOFTMAX_COLS):
        cols = slice(c * MLA_SOFTMAX_COLS, (c + 1) * MLA_SOFTMAX_COLS)
        s = s_ref[:, cols]
        if masked:
            t0 = (c * MLA_SOFTMAX_COLS) % tq
            q_chunk = (q_pos0 + t0
                       + lax.broadcasted_iota(jnp.int32, (1, MLA_SOFTMAX_COLS), 1)) // CHUNK
            s = jnp.where(k_chunk <= q_chunk, s, NEG_BIG)
        m_prev = m_ref[:, cols]
        m_new = jnp.maximum(m_prev, jnp.max(s, axis=0, keepdims=True))
        alpha = jnp.exp2(m_prev - m_new)
        p = jnp.exp2(s - m_new)
        l_ref[:, cols] = alpha * l_ref[:, cols] + jnp.sum(p, axis=0, keepdims=True)
        m_ref[:, cols] = m_new
        a_ref[:, cols] = alpha
        p_ref[:, cols] = p.astype(BF16)
    acc_ref[...] = a_ref[...] * acc_ref[...] + lax.dot_general(
        klat, p_ref[...], (((0,), (0,)), ((), ())), preferred_element_type=F32)


def _mla_prompt_finalize(wuv_ref, o_ref, l_ref, acc_ref, tq):
    inv = 1.0 / l_ref[...]
    for h in range(MLA_HEADS):
        cols = slice(h * tq, (h + 1) * tq)
        ctx_t = (acc_ref[:, cols] * inv[:, cols]).astype(BF16)
        out = lax.dot_general(ctx_t, wuv_ref[h], (((0,), (0,)), ((), ())),
                              preferred_element_type=F32)
        o_ref[:, h * V_DIM:(h + 1) * V_DIM] = out.astype(BF16)


def _mla_prompt_kernel(qb_ref, kb_ref, qp_ref, kp_ref, fl_ref,
                       ql_ref, qr_ref, kl_ref, kr_ref, wuv_ref, o_ref,
                       m_ref, l_ref, acc_ref, s_ref, p_ref, a_ref, *, tq, tk):
    n = pl.program_id(0)
    scratch = (m_ref, l_ref, acc_ref, s_ref, p_ref, a_ref)

    @pl.when(fl_ref[n] % 2 == 1)
    def _():
        _mla_init(m_ref, l_ref, acc_ref)

    @pl.when(fl_ref[n] < 2)
    def _():
        _mla_prompt_tile(ql_ref, qr_ref, kl_ref[...], kr_ref[...], False, 0, 0, *scratch, tq, tk)

    @pl.when(fl_ref[n] >= 2)
    def _():
        _mla_prompt_tile(ql_ref, qr_ref, kl_ref[...], kr_ref[...], True, qp_ref[n], kp_ref[n],
                         *scratch, tq, tk)
        _mla_prompt_finalize(wuv_ref, o_ref, l_ref, acc_ref, tq)


def _mla_prompt(q_lat, q_rope, k_lat, k_rope, w_uv, batch, seq):
    h, t, c = q_lat.shape
    tq = _pick_tile(seq, MLA_Q_TILE, CHUNK)
    tk = _pick_tile(seq, MLA_K_TILE, CHUNK)
    nq, nk = seq // tq, seq // tk
    qb, kb, qp, kp, fl = [], [], [], [], []
    for b in range(batch):
        for qi in range(nq):
            last = ((qi + 1) * tq - 1) // tk
            for kj in range(last + 1):
                qb.append(b * nq + qi)
                kb.append(b * nk + kj)
                qp.append(qi * tq)
                kp.append(kj * tk)
                fl.append((1 if kj == 0 else 0) + (2 if kj == last else 0))
    tabs = [jnp.asarray(np.asarray(a, np.int32)) for a in (qb, kb, qp, kp, fl)]
    rows = h * tq
    grid_spec = pltpu.PrefetchScalarGridSpec(
        num_scalar_prefetch=5,
        grid=(len(qb),),
        in_specs=[
            pl.BlockSpec((h, tq, c), lambda n, qb, kb, qp, kp, fl: (0, qb[n], 0)),
            pl.BlockSpec((h, tq, ROPE_DIM), lambda n, qb, kb, qp, kp, fl: (0, qb[n], 0)),
            pl.BlockSpec((tk, c), lambda n, qb, kb, qp, kp, fl: (kb[n], 0)),
            pl.BlockSpec((tk, ROPE_DIM), lambda n, qb, kb, qp, kp, fl: (kb[n], 0)),
            pl.BlockSpec((h, c, V_DIM), lambda n, qb, kb, qp, kp, fl: (0, 0, 0)),
        ],
        out_specs=pl.BlockSpec((tq, h * V_DIM), lambda n, qb, kb, qp, kp, fl: (qb[n], 0)),
        scratch_shapes=[pltpu.VMEM((1, rows), F32), pltpu.VMEM((1, rows), F32),
                        pltpu.VMEM((c, rows), F32), pltpu.VMEM((tk, rows), F32),
                        pltpu.VMEM((tk, rows), BF16), pltpu.VMEM((1, rows), F32)],
    )
    return pl.pallas_call(
        functools.partial(_mla_prompt_kernel, tq=tq, tk=tk),
        grid_spec=grid_spec,
        out_shape=jax.ShapeDtypeStruct((batch * seq, h * V_DIM), BF16),
        compiler_params=_cparams(("arbitrary",)),
        name="mla_prompt",
    )(*tabs, q_lat, q_rope, k_lat, k_rope, w_uv)


def _mla_sample_kernel(ql_ref, qr_ref, pl_ref, pr_ref, nl_ref, nr_ref, wuv_ref, o_ref,
                       m_ref, l_ref, acc_ref, *, tq, n_past, n_kt):
    j = pl.program_id(1)
    rows = MLA_HEADS * tq
    q_lat = ql_ref[...].reshape(rows, ql_ref.shape[-1])
    q_rope = qr_ref[...].reshape(rows, qr_ref.shape[-1])

    @pl.when(j == 0)
    def _():
        _mla_init(m_ref, l_ref, acc_ref)

    @pl.when(j < n_kt)
    def _():
        klat = pl_ref[...].astype(BF16)
        s = _dot_nt(q_lat, klat) + _dot_nt(q_rope, pr_ref[...].astype(BF16))
        _mla_update(s, klat, m_ref, l_ref, acc_ref)

    @pl.when(j == n_kt)
    def _():
        klat = nl_ref[...]
        s = _dot_nt(q_lat, klat) + _dot_nt(q_rope, nr_ref[...])
        s = jnp.where(_chunk_visible(n_past, n_past, rows, tq, tq), s, NEG_BIG)
        _mla_update(s, klat, m_ref, l_ref, acc_ref)
        _mla_finalize(wuv_ref, o_ref, l_ref, acc_ref, tq)


def _mla_sample(q_lat, q_rope, past_lat, past_kr, layer, k_lat, k_rope, w_uv,
                row0, dec_batch, dec_seq):
    h, t, c = q_lat.shape
    n_past = past_lat.shape[2]
    tk = _pick_tile(n_past, MLA_K_TILE, 128)
    n_kt = n_past // tk
    blk0 = row0 // dec_seq
    rows = h * dec_seq
    past_idx = lambda b, j: (layer, b, jnp.minimum(j, n_kt - 1), 0)
    return pl.pallas_call(
        functools.partial(_mla_sample_kernel, tq=dec_seq, n_past=n_past, n_kt=n_kt),
        grid=(dec_batch, n_kt + 1),
        in_specs=[
            pl.BlockSpec((h, dec_seq, c), lambda b, j: (0, blk0 + b, 0)),
            pl.BlockSpec((h, dec_seq, ROPE_DIM), lambda b, j: (0, blk0 + b, 0)),
            pl.BlockSpec((None, None, tk, c), past_idx),
            pl.BlockSpec((None, None, tk, ROPE_DIM), past_idx),
            pl.BlockSpec((dec_seq, c), lambda b, j: (blk0 + b, 0)),
            pl.BlockSpec((dec_seq, ROPE_DIM), lambda b, j: (blk0 + b, 0)),
            pl.BlockSpec((h, c, V_DIM), lambda b, j: (0, 0, 0)),
        ],
        out_specs=pl.BlockSpec((dec_seq, h * V_DIM), lambda b, j: (b, 0)),
        out_shape=jax.ShapeDtypeStruct((dec_batch * dec_seq, h * V_DIM), BF16),
        scratch_shapes=[pltpu.VMEM((rows, 1), F32), pltpu.VMEM((rows, 1), F32),
                        pltpu.VMEM((rows, c), F32)],
        compiler_params=_cparams(("parallel", "arbitrary")),
        name="mla_sample",
    )(q_lat, q_rope, past_lat, past_kr, k_lat, k_rope, w_uv)


def _scan_matrix(sub):
    j = np.arange(sub)[:, None]
    s = np.arange(sub)[None, :]
    u = (j > s).astype(np.float32)
    return jnp.asarray(np.concatenate([u, u], axis=0), dtype=BF16)


_SbChunk = collections.namedtuple("_SbChunk", "q k v before carry_idx acc_idx")


def _sb_scores(qc, kc, before):
    z = _dot_nt(qc, kc)
    softplus = jnp.log(1.0 + jnp.exp2(-jnp.abs(z))) * LOG2E
    log_beta = jnp.minimum(z, 0.0) - softplus
    log_rest = log_beta - z
    if before is not None:
        log_rest = jnp.where(before, log_rest, 0.0)
    hi = log_rest.astype(BF16)
    lo = (log_rest - hi.astype(F32)).astype(BF16)
    return (log_beta, jnp.concatenate([hi, lo], axis=-1),
            jnp.sum(log_rest, axis=-1, keepdims=True))


def _sb_weights(log_beta, hi_lo, u2, carry, before):
    a = jnp.exp2(log_beta + _dot(hi_lo, u2) + carry)
    if before is not None:
        a = jnp.where(before, a, 0.0)
    return a.astype(BF16)


def _sb_pipeline(chunks, u2, acc_ref, carry_ref):
    scored = None
    weighted = None
    for j in range(len(chunks) + 2):
        new_scored = None
        if j < len(chunks):
            ch = chunks[j]
            new_scored = (ch,) + _sb_scores(ch.q(), ch.k(), ch.before)
        new_weighted = None
        if scored is not None:
            ch, log_beta, hi_lo, row_sum = scored
            carry = carry_ref[ch.carry_idx]
            new_weighted = (ch, _sb_weights(log_beta, hi_lo, u2, carry, ch.before))
            carry_ref[ch.carry_idx] = carry + row_sum
        if weighted is not None:
            ch, a = weighted
            acc_ref[ch.acc_idx] += _dot(a, ch.v())
        scored, weighted = new_scored, new_weighted


def _sb_tile(q_ref, k_ref, v_ref, u2, acc_ref, carry_ref, diagonal, tq, tk, sub, rq):
    chunks = []
    for c in reversed(range(tk // sub)):
        k0, k1 = c * sub, (c + 1) * sub
        for r in range(tq // rq):
            r0, r1 = r * rq, (r + 1) * rq
            before = None
            if diagonal:
                if k0 >= r1 - 1:
                    continue
                if k1 > r0:
                    before = ((k0 + lax.broadcasted_iota(jnp.int32, (1, sub), 1))
                              < (r0 + lax.broadcasted_iota(jnp.int32, (rq, 1), 0)))
            for h in range(SB_HEADS):
                cols = slice(h * SB_DIM, (h + 1) * SB_DIM)
                rows, keys = slice(r0, r1), slice(k0, k1)
                chunks.append(_SbChunk(
                    q=functools.partial(lambda rr, cc: q_ref[rr, cc], rows, cols),
                    k=functools.partial(lambda kk, cc: k_ref[kk, cc], keys, cols),
                    v=functools.partial(lambda kk, cc: v_ref[kk, cc], keys, cols),
                    before=before, carry_idx=(h, rows, slice(None)), acc_idx=(rows, cols)))
    _sb_pipeline(chunks, u2, acc_ref, carry_ref)


def _sb_prompt_kernel(qb_ref, kb_ref, fl_ref,
                      q_ref, k_ref, v_ref, u_ref, o_ref, acc_ref, carry_ref, live_ref,
                      *, tq, tk, sub, rq):
    n = pl.program_id(0)

    @pl.when(fl_ref[n] % 2 == 1)
    def _():
        acc_ref[...] = jnp.zeros(acc_ref.shape, F32)
        carry_ref[...] = jnp.zeros(carry_ref.shape, F32)
        _sb_tile(q_ref, k_ref, v_ref, u_ref[...], acc_ref, carry_ref, True, tq, tk, sub, rq)
        live_ref[0] = (jnp.max(carry_ref[...]) > SB_DEAD_LOG2).astype(jnp.int32)

    @pl.when(jnp.logical_and(fl_ref[n] % 2 == 0, live_ref[0] == 1))
    def _():
        _sb_tile(q_ref, k_ref, v_ref, u_ref[...], acc_ref, carry_ref, False, tq, tk, sub, rq)
        live_ref[0] = (jnp.max(carry_ref[...]) > SB_DEAD_LOG2).astype(jnp.int32)

    @pl.when(fl_ref[n] >= 2)
    def _():
        o_ref[...] = acc_ref[...].astype(BF16)


def _sb_prompt(q, k, v, batch, seq):
    width = q.shape[1]
    tq = _pick_tile(seq, SB_TILE, 128)
    tk = tq
    sub = _pick_tile(tk, SB_SUB, 128)
    rq = _pick_tile(tq, SB_ROWS, 16)
    nq = seq // tq
    qb, kb, fl = [], [], []
    for b in range(batch):
        for qi in range(nq):
            for kj in range(qi, -1, -1):
                qb.append(b * nq + qi)
                kb.append(b * nq + kj)
                fl.append((1 if kj == qi else 0) + (2 if kj == 0 else 0))
    tabs = [jnp.asarray(np.asarray(a, np.int32)) for a in (qb, kb, fl)]
    grid_spec = pltpu.PrefetchScalarGridSpec(
        num_scalar_prefetch=3,
        grid=(len(qb),),
        in_specs=[
            pl.BlockSpec((tq, width), lambda n, qb, kb, fl: (qb[n], 0)),
            pl.BlockSpec((tk, width), lambda n, qb, kb, fl: (kb[n], 0)),
            pl.BlockSpec((tk, width), lambda n, qb, kb, fl: (kb[n], 0)),
            pl.BlockSpec((2 * sub, sub), lambda n, qb, kb, fl: (0, 0)),
        ],
        out_specs=pl.BlockSpec((tq, width), lambda n, qb, kb, fl: (qb[n], 0)),
        scratch_shapes=[pltpu.VMEM((tq, width), F32), pltpu.VMEM((SB_HEADS, tq, 1), F32),
                        pltpu.SMEM((1,), jnp.int32)],
    )
    return pl.pallas_call(
        functools.partial(_sb_prompt_kernel, tq=tq, tk=tk, sub=sub, rq=rq),
        grid_spec=grid_spec,
        out_shape=jax.ShapeDtypeStruct((batch * seq, width), BF16),
        compiler_params=_cparams(("arbitrary",)),
        name="sb_prompt",
    )(*tabs, q, k, v, _scan_matrix(sub))


def _sb_sample_kernel(q_ref, nk_ref, nv_ref, un_ref, up_ref, pk_hbm, pv_hbm, o_ref,
                      acc_ref, carry_ref, kbuf, vbuf, sem, *, layer, tq, tk, n_kt):
    b = pl.program_id(0)

    def tile_copies(i, slot):
        start = (n_kt - 1 - i) * tk
        copies = []
        for h in range(SB_HEADS):
            copies.append(pltpu.make_async_copy(
                pk_hbm.at[layer, b, pl.ds(start, tk), h, :], kbuf.at[slot, h], sem.at[0, slot]))
            copies.append(pltpu.make_async_copy(
                pv_hbm.at[layer, b, pl.ds(start, tk), h, :], vbuf.at[slot, h], sem.at[1, slot]))
        return copies

    def start_tile(i, slot):
        for copy in tile_copies(i, slot):
            copy.start()

    def wait_tile(i, slot):
        for copy in tile_copies(i, slot):
            copy.wait()

    def alive():
        return (jnp.max(carry_ref[...]) > SB_DEAD_LOG2).astype(jnp.int32)

    start_tile(0, 0)

    acc_ref[...] = jnp.zeros(acc_ref.shape, F32)
    carry_ref[...] = jnp.zeros(carry_ref.shape, F32)
    before = (lax.broadcasted_iota(jnp.int32, (1, tq), 1)
              < lax.broadcasted_iota(jnp.int32, (tq, 1), 0))
    chunks = []
    for h in range(SB_HEADS):
        cols = slice(h * SB_DIM, (h + 1) * SB_DIM)
        chunks.append(_SbChunk(
            q=functools.partial(lambda cc: q_ref[:, cc], cols),
            k=functools.partial(lambda cc: nk_ref[:, cc], cols),
            v=functools.partial(lambda cc: nv_ref[:, cc], cols),
            before=before, carry_idx=(h, slice(None), slice(None)), acc_idx=(slice(None), cols)))
    _sb_pipeline(chunks, un_ref[...], acc_ref, carry_ref)

    def tile_step(state):
        i, _ = state
        slot = lax.rem(i, 2)
        wait_tile(i, slot)

        @pl.when(i + 1 < n_kt)
        def _():
            start_tile(i + 1, 1 - slot)

        chunks = []
        for h in range(SB_HEADS):
            cols = slice(h * SB_DIM, (h + 1) * SB_DIM)
            chunks.append(_SbChunk(
                q=functools.partial(lambda cc: q_ref[:, cc], cols),
                k=functools.partial(lambda hh: kbuf[slot, hh].astype(BF16), h),
                v=functools.partial(lambda hh: vbuf[slot, hh].astype(BF16), h),
                before=None, carry_idx=(h, slice(None), slice(None)), acc_idx=(slice(None), cols)))
        _sb_pipeline(chunks, up_ref[...], acc_ref, carry_ref)
        return i + 1, alive()

    tiles_done, _ = lax.while_loop(
        lambda state: jnp.logical_and(state[0] < n_kt, state[1] == 1),
        tile_step, (jnp.int32(0), alive()))

    @pl.when(tiles_done < n_kt)
    def _():
        wait_tile(tiles_done, lax.rem(tiles_done, 2))

    o_ref[...] = acc_ref[...].astype(BF16)


def _sb_sample(q, past_k, past_v, layer, k, v, row0, dec_batch, dec_seq):
    width = q.shape[1]
    n_past = past_k.shape[2]
    tk = _pick_tile(n_past, SB_SUB, 128)
    n_kt = n_past // tk
    blk0 = row0 // dec_seq
    new_rows = pl.BlockSpec((dec_seq, width), lambda b: (blk0 + b, 0))
    in_hbm = pl.BlockSpec(memory_space=pl.ANY)
    return pl.pallas_call(
        functools.partial(_sb_sample_kernel, layer=layer, tq=dec_seq, tk=tk, n_kt=n_kt),
        grid=(dec_batch,),
        in_specs=[
            new_rows, new_rows, new_rows,
            pl.BlockSpec((2 * dec_seq, dec_seq), lambda b: (0, 0)),
            pl.BlockSpec((2 * tk, tk), lambda b: (0, 0)),
            in_hbm, in_hbm,
        ],
        out_specs=pl.BlockSpec((dec_seq, width), lambda b: (b, 0)),
        out_shape=jax.ShapeDtypeStruct((dec_batch * dec_seq, width), BF16),
        scratch_shapes=[pltpu.VMEM((dec_seq, width), F32),
                        pltpu.VMEM((SB_HEADS, dec_seq, 1), F32),
                        pltpu.VMEM((2, SB_HEADS, tk, SB_DIM), F32),
                        pltpu.VMEM((2, SB_HEADS, tk, SB_DIM), F32),
                        pltpu.SemaphoreType.DMA((2, 2))],
        compiler_params=_cparams(("arbitrary",)),
        name="sb_sample",
    )(q, k, v, _scan_matrix(dec_seq), _scan_matrix(tk), past_k, past_v)


def _merge_kernel(oap_ref, oas_ref, obp_ref, obs_ref, wa_ref, wb_ref, ga_ref, gb_ref, o_ref,
                  *, n_p):
    i = pl.program_id(0)
    a = _dot(_load_split(i, n_p, oap_ref, oas_ref), wa_ref[...])
    b = _dot(_load_split(i, n_p, obp_ref, obs_ref), wb_ref[...])
    o_ref[...] = (ga_ref[...].astype(F32) * a + gb_ref[...].astype(F32) * b).astype(BF16)


def _merge(oa_p, oa_s, ob_p, ob_s, w_a, w_b, gates, tm):
    da = oa_p.shape[1]
    db = ob_p.shape[1]
    t = oa_p.shape[0] + oa_s.shape[0]
    n_p = oa_p.shape[0] // tm
    n = w_a.shape[1]
    tn = _pick_tile(n, COL_TILE, 128)
    nj = n // tn
    return pl.pallas_call(
        functools.partial(_merge_kernel, n_p=n_p),
        grid=(t // tm, nj),
        in_specs=[*_split_specs(tm, da, n_p),
                  *_split_specs(tm, db, n_p),
                  pl.BlockSpec((da, tn), lambda i, j: (0, j)),
                  pl.BlockSpec((db, tn), lambda i, j: (0, j)),
                  pl.BlockSpec((tm, tn), lambda i, j: (i, j)),
                  pl.BlockSpec((tm, tn), lambda i, j: (i, j + nj))],
        out_specs=pl.BlockSpec((tm, tn), lambda i, j: (i, j)),
        out_shape=jax.ShapeDtypeStruct((t, n), BF16),
        compiler_params=_cparams(("parallel", "parallel")),
        name="merge",
    )(oa_p, oa_s, ob_p, ob_s, w_a, w_b, gates, gates)


def _res_ln_finalize(z_ref, g_ref, b_ref, o_ref, ob_ref):
    n_tiles, _, tn = z_ref.shape
    width = n_tiles * tn
    total = z_ref[0].sum(axis=-1, keepdims=True)
    for c in range(1, n_tiles):
        total += z_ref[c].sum(axis=-1, keepdims=True)
    mu = total / width
    sq = jnp.square(z_ref[0] - mu).sum(axis=-1, keepdims=True)
    for c in range(1, n_tiles):
        sq += jnp.square(z_ref[c] - mu).sum(axis=-1, keepdims=True)
    rstd = lax.rsqrt(sq / width + LN_EPS)
    for c in range(n_tiles):
        cols = slice(c * tn, (c + 1) * tn)
        y = (z_ref[c] - mu) * rstd * g_ref[:, cols] + b_ref[:, cols]
        o_ref[:, cols] = y
        if ob_ref is not None:
            ob_ref[:, cols] = y.astype(BF16)


def _out_ln_kernel(m_ref, w_ref, x_ref, g_ref, b_ref, o_ref, ob_ref, z_ref, *, alpha):
    j = pl.program_id(1)
    z_ref[j] = alpha * x_ref[...] + _dot(m_ref[...], w_ref[...])

    @pl.when(j == pl.num_programs(1) - 1)
    def _():
        _res_ln_finalize(z_ref, g_ref, b_ref, o_ref, ob_ref)


def _out_ln(merged, w, x, g, b, alpha, tm):
    t, d = x.shape
    k = merged.shape[1]
    tn = _pick_tile(d, COL_TILE, 128)
    full_row = pl.BlockSpec((tm, d), lambda i, j: (i, 0))
    vec = pl.BlockSpec((1, d), lambda i, j: (0, 0))
    return pl.pallas_call(
        functools.partial(_out_ln_kernel, alpha=alpha),
        grid=(t // tm, d // tn),
        in_specs=[pl.BlockSpec((tm, k), lambda i, j: (i, 0)),
                  pl.BlockSpec((k, tn), lambda i, j: (0, j)),
                  pl.BlockSpec((tm, tn), lambda i, j: (i, j)),
                  vec, vec],
        out_specs=[full_row, full_row],
        out_shape=[jax.ShapeDtypeStruct((t, d), F32), jax.ShapeDtypeStruct((t, d), BF16)],
        scratch_shapes=[pltpu.VMEM((d // tn, tm, tn), F32)],
        compiler_params=_cparams(("parallel", "arbitrary")),
        name="out_ln",
    )(merged, w, x, g.reshape(1, d), b.reshape(1, d))


def _ffn_ln_kernel(xb_ref, wg_ref, wu_ref, wd_ref, x_ref, g_ref, b_ref, o_ref, ob_ref,
                   acc_ref, *, alpha):
    f = pl.program_id(1)

    @pl.when(f == 0)
    def _():
        acc_ref[...] = jnp.zeros(acc_ref.shape, F32)

    xb = xb_ref[...]
    gate = _dot(xb, wg_ref[...])
    up = _dot(xb, wu_ref[...])
    hidden = (gate * _sigmoid(gate) * up).astype(BF16)
    acc_ref[...] += _dot(hidden, wd_ref[...])

    @pl.when(f == pl.num_programs(1) - 1)
    def _():
        y = _layer_norm_rows(alpha * x_ref[...] + acc_ref[...], g_ref[...], b_ref[...])
        o_ref[...] = y
        ob_ref[...] = y.astype(BF16)


def _ffn_ln(xb, w_gu, w_down, x, g, b, alpha, tm):
    t, d = x.shape
    ff = w_down.shape[0]
    tf = _pick_tile(ff, COL_TILE, 128)
    nf = ff // tf
    full_row = pl.BlockSpec((tm, d), lambda i, f: (i, 0))
    vec = pl.BlockSpec((1, d), lambda i, f: (0, 0))
    return pl.pallas_call(
        functools.partial(_ffn_ln_kernel, alpha=alpha),
        grid=(t // tm, nf),
        in_specs=[full_row,
                  pl.BlockSpec((d, tf), lambda i, f: (0, f)),
                  pl.BlockSpec((d, tf), lambda i, f: (0, f + nf)),
                  pl.BlockSpec((tf, d), lambda i, f: (f, 0)),
                  full_row, vec, vec],
        out_specs=[full_row, full_row],
        out_shape=[jax.ShapeDtypeStruct((t, d), F32), jax.ShapeDtypeStruct((t, d), BF16)],
        scratch_shapes=[pltpu.VMEM((tm, d), F32)],
        compiler_params=_cparams(("parallel", "arbitrary")),
        name="ffn_ln",
    )(xb, w_gu, w_gu, w_down, x, g.reshape(1, d), b.reshape(1, d))


def _ple_ln_kernel(xb_ref, wg_ref, bg_ref, pp_ref, ps_ref, wp_ref, x_ref, g_ref, b_ref,
                   o1_ref, o2_ref, z_ref, *, alpha, n_p, last):
    i = pl.program_id(0)
    j = pl.program_id(1)
    gate = _sigmoid(_dot(xb_ref[...], wg_ref[...]) + bg_ref[...])
    p = _load_split(i, n_p, pp_ref, ps_ref).astype(BF16)
    z_ref[j] = alpha * x_ref[...] + gate * _dot(p, wp_ref[...])
    done = j == pl.num_programs(1) - 1

    if not last:
        @pl.when(done)
        def _():
            _res_ln_finalize(z_ref, g_ref, b_ref, o1_ref, o2_ref)
    else:
        @pl.when(jnp.logical_and(done, i < n_p))
        def _():
            _res_ln_finalize(z_ref, g_ref, b_ref, o1_ref, None)

        @pl.when(jnp.logical_and(done, i >= n_p))
        def _():
            _res_ln_finalize(z_ref, g_ref, b_ref, o2_ref, None)


def _ple_ln(xb, w_gate, b_gate, p_p, p_s, layer, w_proj, x, g, b, alpha, tm, last):
    t, d = x.shape
    pd = p_p.shape[2]
    tp = p_p.shape[1]
    n_p = tp // tm
    tn = _pick_tile(d, COL_TILE, 128)
    full_row = pl.BlockSpec((tm, d), lambda i, j: (i, 0))
    vec = pl.BlockSpec((1, d), lambda i, j: (0, 0))
    if last:
        out_specs = list(_split_specs(tm, d, n_p))
        out_shape = [jax.ShapeDtypeStruct((tp, d), F32), jax.ShapeDtypeStruct((t - tp, d), F32)]
    else:
        out_specs = [full_row, full_row]
        out_shape = [jax.ShapeDtypeStruct((t, d), F32), jax.ShapeDtypeStruct((t, d), BF16)]
    return pl.pallas_call(
        functools.partial(_ple_ln_kernel, alpha=alpha, n_p=n_p, last=last),
        grid=(t // tm, d // tn),
        in_specs=[full_row,
                  pl.BlockSpec((d, tn), lambda i, j: (0, j)),
                  pl.BlockSpec((1, tn), lambda i, j: (0, j)),
                  pl.BlockSpec((None, tm, pd), lambda i, j: (layer, jnp.minimum(i, n_p - 1), 0)),
                  pl.BlockSpec((None, tm, pd), lambda i, j: (layer, jnp.maximum(i - n_p, 0), 0)),
                  pl.BlockSpec((pd, tn), lambda i, j: (0, j)),
                  pl.BlockSpec((tm, tn), lambda i, j: (i, j)),
                  vec, vec],
        out_specs=out_specs,
        out_shape=out_shape,
        scratch_shapes=[pltpu.VMEM((d // tn, tm, tn), F32)],
        compiler_params=_cparams(("arbitrary", "arbitrary")),
        name="ple_ln",
    )(xb, w_gate, b_gate.reshape(1, d), p_p, p_s, w_proj, x, g.reshape(1, d), b.reshape(1, d))


def _rope_tables(pos):
    half = ROPE_DIM // 2
    inv_freq = 1.0 / (ROPE_THETA ** (jnp.arange(half, dtype=F32) * (2.0 / ROPE_DIM)))
    ang = pos.astype(F32)[:, None] * inv_freq[None, :]
    return jnp.cos(ang), jnp.sin(ang)


def kernel(x_prompt, x_sample, cache_mla_latent, cache_mla_krope, cache_sb_k, cache_sb_v,
           p_prompt, p_sample, ln_in_g, ln_in_b, w_in, b_gate, q_a_norm_g, w_q_b,
           kv_a_norm_g, w_kv_b, w_branch_a, w_branch_b, w_out, ln1_g, ln1_b,
           w_ffn_gu, w_ffn_down, ln2_g, ln2_b, w_ple_gate, b_ple_gate, w_ple_proj,
           ln3_g, ln3_b):
    batch, seq, d = x_prompt.shape
    dec_batch, dec_seq, _ = x_sample.shape
    depth = w_in.shape[0]
    n_past = cache_mla_latent.shape[2]
    tp = batch * seq
    ts = dec_batch * dec_seq
    sb_width = SB_HEADS * SB_DIM
    alpha = (2 * depth) ** 0.25

    pos = jnp.concatenate([jnp.tile(jnp.arange(seq, dtype=jnp.int32), batch),
                           jnp.tile(n_past + jnp.arange(dec_seq, dtype=jnp.int32), dec_batch)])
    cos, sin = _rope_tables(pos)

    tm = _pick_tile(math.gcd(tp, ts), ROW_TILE, 16)
    p_p = p_prompt.reshape(depth, tp, -1)
    p_s = p_sample.reshape(depth, ts, -1)

    x, xb = _ln_in(x_prompt.reshape(tp, d), x_sample.reshape(ts, d), ln_in_g, ln_in_b, tm)

    splits = np.cumsum([0, Q_LORA, KV_LORA, ROPE_DIM, sb_width, sb_width, sb_width, 2 * d])
    new_rows = []
    for l in range(depth):
        w_cols = [w_in[l][:, splits[i]:splits[i + 1]].astype(BF16) for i in range(7)]
        w_qa, w_lat, w_kr, w_sq, w_sk, w_sv, w_gates = w_cols
        wq = w_q_b[l].reshape(Q_LORA, MLA_HEADS, NOPE_DIM + ROPE_DIM)
        wq_nope = wq[..., :NOPE_DIM].reshape(Q_LORA, MLA_HEADS * NOPE_DIM).astype(BF16)
        wq_rope = wq[..., NOPE_DIM:].reshape(Q_LORA, MLA_HEADS * ROPE_DIM).astype(BF16)
        w_kv = w_kv_b[l].reshape(KV_LORA, MLA_HEADS, NOPE_DIM + V_DIM)
        wuk_h = w_kv[..., :NOPE_DIM].transpose(1, 2, 0).astype(BF16)
        wuv_h = w_kv[..., NOPE_DIM:].transpose(1, 0, 2).astype(BF16)

        qa_n = _proj_qa(xb, w_qa, q_a_norm_g[l], tm)
        lat_p, lat_s, kr_p, kr_s, lat_b, kr_b = _proj_kv(xb, w_lat, w_kr, kv_a_norm_g[l],
                                                        cos, sin, tm, tp)
        sbq, k_p, k_s, v_p, v_s, sbk_b, sbv_b = _proj_sb(xb, w_sq, w_sk, w_sv, tm, tp)
        (gates,) = _proj_tiled(_proj_gate_kernel, xb, w_gates, [b_gate[l].reshape(1, 2 * d)],
                               [BF16], "proj_gates", tm)

        q_lat, q_rope = _q_heads(qa_n, wq_nope, wq_rope, wuk_h, cos, sin, tm)
        oa_p = _mla_prompt(q_lat, q_rope, lat_b, kr_b, wuv_h, batch, seq)
        oa_s = _mla_sample(q_lat, q_rope, cache_mla_latent, cache_mla_krope, l, lat_b, kr_b,
                           wuv_h, tp, dec_batch, dec_seq)
        ob_p = _sb_prompt(sbq, sbk_b, sbv_b, batch, seq)
        ob_s = _sb_sample(sbq, cache_sb_k, cache_sb_v, l, sbk_b, sbv_b, tp, dec_batch, dec_seq)

        merged = _merge(oa_p, oa_s, ob_p, ob_s, w_branch_a[l].astype(BF16),
                        w_branch_b[l].astype(BF16), gates, tm)
        x, xb = _out_ln(merged, w_out[l].astype(BF16), x, ln1_g[l], ln1_b[l], alpha, tm)
        x, xb = _ffn_ln(xb, w_ffn_gu[l].astype(BF16), w_ffn_down[l].astype(BF16), x,
                        ln2_g[l], ln2_b[l], alpha, tm)
        x, xb = _ple_ln(xb, w_ple_gate[l].astype(BF16), b_ple_gate[l], p_p, p_s, l,
                        w_ple_proj[l].astype(BF16), x, ln3_g[l], ln3_b[l], alpha, tm,
                        last=(l == depth - 1))

        new_rows.append((lat_p, kr_p, k_p, v_p, lat_s, kr_s, k_s, v_s))

    lat_p, kr_p, k_p, v_p, lat_s, kr_s, k_s, v_s = (jnp.stack(a) for a in zip(*new_rows))
    head_dims = (SB_HEADS, SB_DIM)
    return (x.reshape(batch, seq, d), xb.reshape(dec_batch, dec_seq, d),
            lat_p.reshape(depth, batch, seq, KV_LORA), kr_p.reshape(depth, batch, seq, ROPE_DIM),
            k_p.reshape(depth, batch, seq, *head_dims), v_p.reshape(depth, batch, seq, *head_dims),
            lat_s.reshape(depth, dec_batch, dec_seq, KV_LORA),
            kr_s.reshape(depth, dec_batch, dec_seq, ROPE_DIM),
            k_s.reshape(depth, dec_batch, dec_seq, *head_dims),
            v_s.reshape(depth, dec_batch, dec_seq, *head_dims))
```

```python
import collections
import functools
import math

import numpy as np
import jax
import jax.numpy as jnp
from jax import lax
from jax.experimental import pallas as pl
from jax.experimental.pallas import tpu as pltpu

CHUNK = 64
MLA_HEADS = 16
Q_LORA = 512
KV_LORA = 512
NOPE_DIM = 128
ROPE_DIM = 64
V_DIM = 128
ROPE_THETA = 10000.0
MLA_SCALE = (NOPE_DIM + ROPE_DIM) ** -0.5
SB_HEADS = 8
SB_DIM = 128
SB_SCALE = SB_DIM ** -0.5
LN_EPS = 1e-5
RMS_EPS = 1e-6
LOG2E = 1.4426950408889634
MLA_QSCALE = MLA_SCALE * LOG2E
SB_QSCALE = SB_SCALE * LOG2E
SB_DEAD_LOG2 = -160.0

BF16 = jnp.bfloat16
F32 = jnp.float32

V7X_VMEM_LIMIT_BYTES = 56 * 1024 * 1024
ROW_TILE = 512
COL_TILE = 512
PROJ_COL_TILE = 1024
MLA_Q_TILE = 256
MLA_K_TILE = 512
MLA_SOFTMAX_COLS = 128
SB_TILE = 512
SB_SUB = 256
SB_ROWS = 128
NEG_BIG = -1e30


def _pick_tile(n, target, mult):
    best = None
    for t in range(mult, min(n, target) + 1, mult):
        if n % t == 0:
            best = t
    if best is None:
        return n
    return best


def _cparams(sem):
    return pltpu.CompilerParams(dimension_semantics=sem,
                                vmem_limit_bytes=V7X_VMEM_LIMIT_BYTES)


def _resident(shape):
    return pl.BlockSpec(shape, lambda *_: (0,) * len(shape), pipeline_mode=pl.Buffered(1))


def _dot(a, b):
    return jnp.dot(a, b, preferred_element_type=F32)


def _dot_nt(a, b):
    return lax.dot_general(a, b, (((1,), (1,)), ((), ())), preferred_element_type=F32)


def _layer_norm_rows(z, g, b):
    mu = jnp.mean(z, axis=-1, keepdims=True)
    zc = z - mu
    var = jnp.mean(zc * zc, axis=-1, keepdims=True)
    return zc * lax.rsqrt(var + LN_EPS) * g + b


def _rms_norm_rows(z, g):
    return z * lax.rsqrt(jnp.mean(z * z, axis=-1, keepdims=True) + RMS_EPS) * g


def _sigmoid(z):
    return 1.0 / (1.0 + jnp.exp(-z))


def _rope_rows(x, cos, sin):
    half = ROPE_DIM // 2
    x1 = x[:, :half]
    x2 = x[:, half:]
    return jnp.concatenate([x1 * cos - x2 * sin, x2 * cos + x1 * sin], axis=-1)


def _split_specs(tm, width, n_p):
    return (pl.BlockSpec((tm, width), lambda i, *_: (jnp.minimum(i, n_p - 1), 0)),
            pl.BlockSpec((tm, width), lambda i, *_: (jnp.maximum(i - n_p, 0), 0)))


def _load_split(i, n_p, p_ref, s_ref):
    return jnp.where(i < n_p, p_ref[...], s_ref[...])


def _store_split(i, n_p, p_ref, s_ref, value):
    @pl.when(i < n_p)
    def _():
        p_ref[...] = value

    @pl.when(i >= n_p)
    def _():
        s_ref[...] = value


def _ln_in_kernel(xp_ref, xs_ref, g_ref, b_ref, o_ref, ob_ref, *, n_p):
    x = _load_split(pl.program_id(0), n_p, xp_ref, xs_ref)
    y = _layer_norm_rows(x, g_ref[...], b_ref[...])
    o_ref[...] = y
    ob_ref[...] = y.astype(BF16)


def _ln_in(x_p, x_s, g, b, tm):
    d = x_p.shape[1]
    t = x_p.shape[0] + x_s.shape[0]
    n_p = x_p.shape[0] // tm
    row = pl.BlockSpec((tm, d), lambda i: (i, 0))
    vec = pl.BlockSpec((1, d), lambda i: (0, 0))
    return pl.pallas_call(
        functools.partial(_ln_in_kernel, n_p=n_p),
        grid=(t // tm,),
        in_specs=[*_split_specs(tm, d, n_p), vec, vec],
        out_specs=[row, row],
        out_shape=[jax.ShapeDtypeStruct((t, d), F32), jax.ShapeDtypeStruct((t, d), BF16)],
        compiler_params=_cparams(("parallel",)),
        name="ln_in",
    )(x_p, x_s, g.reshape(1, d), b.reshape(1, d))


def _proj_qa_kernel(x_ref, w_ref, g_ref, o_ref):
    o_ref[...] = _rms_norm_rows(_dot(x_ref[...], w_ref[...]), g_ref[...]).astype(BF16)


def _proj_kv_kernel(x_ref, wl_ref, wr_ref, g_ref, cos_ref, sin_ref,
                    latp_ref, lats_ref, krp_ref, krs_ref, latb_ref, krb_ref, *, n_p):
    i = pl.program_id(0)
    x = x_ref[...]
    lat = _rms_norm_rows(_dot(x, wl_ref[...]), g_ref[...])
    kr = _rope_rows(_dot(x, wr_ref[...]), cos_ref[...], sin_ref[...])
    _store_split(i, n_p, latp_ref, lats_ref, lat)
    _store_split(i, n_p, krp_ref, krs_ref, kr)
    latb_ref[...] = lat.astype(BF16)
    krb_ref[...] = kr.astype(BF16)


def _proj_sb_kernel(x_ref, wq_ref, wk_ref, wv_ref,
                    q_ref, kp_ref, ks_ref, vp_ref, vs_ref, kb_ref, vb_ref, *, n_p):
    i = pl.program_id(0)
    x = x_ref[...]
    q_ref[...] = (_dot(x, wq_ref[...]) * SB_QSCALE).astype(BF16)
    k = _dot(x, wk_ref[...])
    _store_split(i, n_p, kp_ref, ks_ref, k)
    kb_ref[...] = k.astype(BF16)
    v = _dot(x, wv_ref[...])
    _store_split(i, n_p, vp_ref, vs_ref, v)
    vb_ref[...] = v.astype(BF16)


def _proj_gate_kernel(x_ref, w_ref, b_ref, o_ref, *, tn):
    x = x_ref[...]
    for c in range(o_ref.shape[1] // tn):
        cols = slice(c * tn, (c + 1) * tn)
        o_ref[:, cols] = _sigmoid(_dot(x, w_ref[:, cols]) + b_ref[:, cols]).astype(BF16)


def _proj_qa(xb, w, g, tm):
    t, d = xb.shape
    n = w.shape[1]
    return pl.pallas_call(
        _proj_qa_kernel,
        grid=(t // tm,),
        in_specs=[pl.BlockSpec((tm, d), lambda i: (i, 0)),
                  pl.BlockSpec((d, n), lambda i: (0, 0)),
                  pl.BlockSpec((1, n), lambda i: (0, 0))],
        out_specs=pl.BlockSpec((tm, n), lambda i: (i, 0)),
        out_shape=jax.ShapeDtypeStruct((t, n), BF16),
        compiler_params=_cparams(("parallel",)),
        name="proj_qa",
    )(xb, w, g.reshape(1, n))


def _proj_kv(xb, w_lat, w_kr, g, cos, sin, tm, tp):
    t, d = xb.shape
    c = w_lat.shape[1]
    r = w_kr.shape[1]
    n_p = tp // tm
    rows = lambda w: pl.BlockSpec((tm, w), lambda i: (i, 0))
    full = lambda a, b: pl.BlockSpec((a, b), lambda i: (0, 0))
    f32 = lambda n, w: jax.ShapeDtypeStruct((n, w), F32)
    return pl.pallas_call(
        functools.partial(_proj_kv_kernel, n_p=n_p),
        grid=(t // tm,),
        in_specs=[rows(d), full(d, c), full(d, r), full(1, c), rows(r // 2), rows(r // 2)],
        out_specs=[*_split_specs(tm, c, n_p), *_split_specs(tm, r, n_p), rows(c), rows(r)],
        out_shape=[f32(tp, c), f32(t - tp, c), f32(tp, r), f32(t - tp, r),
                   jax.ShapeDtypeStruct((t, c), BF16), jax.ShapeDtypeStruct((t, r), BF16)],
        compiler_params=_cparams(("arbitrary",)),
        name="proj_kv",
    )(xb, w_lat, w_kr, g.reshape(1, c), cos, sin)


def _proj_sb(xb, w_q, w_k, w_v, tm, tp):
    t, d = xb.shape
    n = w_q.shape[1]
    n_p = tp // tm
    rows = lambda w: pl.BlockSpec((tm, w), lambda i: (i, 0))
    weight = _resident((d, n))
    f32 = lambda m: jax.ShapeDtypeStruct((m, n), F32)
    bf16 = jax.ShapeDtypeStruct((t, n), BF16)
    return pl.pallas_call(
        functools.partial(_proj_sb_kernel, n_p=n_p),
        grid=(t // tm,),
        in_specs=[rows(d), weight, weight, weight],
        out_specs=[rows(n), *_split_specs(tm, n, n_p), *_split_specs(tm, n, n_p), rows(n), rows(n)],
        out_shape=[bf16, f32(tp), f32(t - tp), f32(tp), f32(t - tp), bf16, bf16],
        compiler_params=_cparams(("arbitrary",)),
        name="proj_sb",
    )(xb, w_q, w_k, w_v)


def _proj_gates(xb, w, bias, tm):
    t, d = xb.shape
    n = w.shape[1]
    tn = _pick_tile(n, PROJ_COL_TILE, 128)
    return pl.pallas_call(
        functools.partial(_proj_gate_kernel, tn=tn),
        grid=(t // tm,),
        in_specs=[pl.BlockSpec((tm, d), lambda i: (i, 0)), _resident((d, n)),
                  pl.BlockSpec((1, n), lambda i: (0, 0))],
        out_specs=pl.BlockSpec((tm, n), lambda i: (i, 0)),
        out_shape=jax.ShapeDtypeStruct((t, n), BF16),
        compiler_params=_cparams(("parallel",)),
        name="proj_gates",
    )(xb, w, bias.reshape(1, n))


def _q_heads_kernel(qa_ref, wn_ref, wr_ref, wuk_ref, cos_ref, sin_ref, ql_ref, qr_ref):
    qa = qa_ref[...]
    q_nope = _dot(qa, wn_ref[...]).astype(BF16)
    q_rope = _dot(qa, wr_ref[...])
    cos = cos_ref[...]
    sin = sin_ref[...]
    for h in range(MLA_HEADS):
        nope = q_nope[:, h * NOPE_DIM:(h + 1) * NOPE_DIM]
        ql_ref[h] = (_dot(nope, wuk_ref[h]) * MLA_QSCALE).astype(BF16)
        qr = _rope_rows(q_rope[:, h * ROPE_DIM:(h + 1) * ROPE_DIM], cos, sin)
        qr_ref[h] = (qr * MLA_QSCALE).astype(BF16)


def _q_heads(qa_n, w_nope, w_rope, wuk_h, cos, sin, tm):
    t, ql = qa_n.shape
    h, _, c = wuk_h.shape
    half = ROPE_DIM // 2
    return pl.pallas_call(
        _q_heads_kernel,
        grid=(t // tm,),
        in_specs=[pl.BlockSpec((tm, ql), lambda i: (i, 0)),
                  pl.BlockSpec((ql, h * NOPE_DIM), lambda i: (0, 0)),
                  pl.BlockSpec((ql, h * ROPE_DIM), lambda i: (0, 0)),
                  pl.BlockSpec((h, NOPE_DIM, c), lambda i: (0, 0, 0)),
                  pl.BlockSpec((tm, half), lambda i: (i, 0)),
                  pl.BlockSpec((tm, half), lambda i: (i, 0))],
        out_specs=[pl.BlockSpec((h, tm, c), lambda i: (0, i, 0)),
                   pl.BlockSpec((h, tm, ROPE_DIM), lambda i: (0, i, 0))],
        out_shape=[jax.ShapeDtypeStruct((h, t, c), BF16),
                   jax.ShapeDtypeStruct((h, t, ROPE_DIM), BF16)],
        compiler_params=_cparams(("parallel",)),
        name="q_heads",
    )(qa_n, w_nope, w_rope, wuk_h, cos, sin)


def _mla_update(s, klat, m_ref, l_ref, acc_ref):
    m_prev = m_ref[...]
    m_new = jnp.maximum(m_prev, jnp.max(s, axis=-1, keepdims=True))
    alpha = jnp.exp2(m_prev - m_new)
    p = jnp.exp2(s - m_new)
    l_ref[...] = alpha * l_ref[...] + jnp.sum(p, axis=-1, keepdims=True)
    acc_ref[...] = alpha * acc_ref[...] + _dot(p.astype(BF16), klat)
    m_ref[...] = m_new


def _mla_init(m_ref, l_ref, acc_ref):
    m_ref[...] = jnp.full(m_ref.shape, NEG_BIG, F32)
    l_ref[...] = jnp.zeros(l_ref.shape, F32)
    acc_ref[...] = jnp.zeros(acc_ref.shape, F32)


def _mla_finalize(wuv_ref, o_ref, l_ref, acc_ref, tq):
    inv = 1.0 / l_ref[...]
    for h in range(MLA_HEADS):
        rows = slice(h * tq, (h + 1) * tq)
        ctx = (acc_ref[rows, :] * inv[rows, :]).astype(BF16)
        o_ref[:, h * V_DIM:(h + 1) * V_DIM] = _dot(ctx, wuv_ref[h]).astype(BF16)


def _chunk_visible(q_pos0, k_pos0, rows, tq, tk):
    t = lax.rem(lax.broadcasted_iota(jnp.int32, (rows, 1), 0), tq)
    q_chunk = (q_pos0 + t) // CHUNK
    k_chunk = (k_pos0 + lax.broadcasted_iota(jnp.int32, (1, tk), 1)) // CHUNK
    return k_chunk <= q_chunk


def _mla_prompt_tile(ql_ref, qr_ref, klat, kr, masked, q_pos0, k_pos0,
                     m_ref, l_ref, acc_ref, s_ref, p_ref, a_ref, tq, tk):
    rows = MLA_HEADS * tq
    q_lat = ql_ref[...].reshape(rows, ql_ref.shape[-1])
    q_rope = qr_ref[...].reshape(rows, qr_ref.shape[-1])
    s_ref[...] = _dot_nt(klat, q_lat) + _dot_nt(kr, q_rope)
    if masked:
        k_chunk = (k_pos0 + lax.broadcasted_iota(jnp.int32, (tk, 1), 0)) // CHUNK
    for c in range(rows // MLA_SOFTMAX_COLS):
        cols = slice(c * MLA_SOFTMAX_COLS, (c + 1) * MLA_SOFTMAX_COLS)
        s = s_ref[:, cols]
        if masked:
            t0 = (c * MLA_SOFTMAX_COLS) % tq
            q_chunk = (q_pos0 + t0
                       + lax.broadcasted_iota(jnp.int32, (1, MLA_SOFTMAX_COLS), 1)) // CHUNK
            s = jnp.where(k_chunk <= q_chunk, s, NEG_BIG)
        m_prev = m_ref[:, cols]
        m_new = jnp.maximum(m_prev, jnp.max(s, axis=0, keepdims=True))
        alpha = jnp.exp2(m_prev - m_new)
        p = jnp.exp2(s - m_new)
        l_ref[:, cols] = alpha * l_ref[:, cols] + jnp.sum(p, axis=0, keepdims=True)
        m_ref[:, cols] = m_new
        a_ref[:, cols] = alpha
        p_ref[:, cols] = p.astype(BF16)
    acc_ref[...] = a_ref[...] * acc_ref[...] + lax.dot_general(
        klat, p_ref[...], (((0,), (0,)), ((), ())), preferred_element_type=F32)


def _mla_prompt_finalize(wuv_ref, o_ref, l_ref, acc_ref, tq):
    inv = 1.0 / l_ref[...]
    for h in range(MLA_HEADS):
        cols = slice(h * tq, (h + 1) * tq)
        ctx_t = (acc_ref[:, cols] * inv[:, cols]).astype(BF16)
        out = lax.dot_general(ctx_t, wuv_ref[h], (((0,), (0,)), ((), ())),
                              preferred_element_type=F32)
        o_ref[:, h * V_DIM:(h + 1) * V_DIM] = out.astype(BF16)


def _mla_prompt_kernel(qb_ref, kb_ref, qp_ref, kp_ref, fl_ref,
                       ql_ref, qr_ref, kl_ref, kr_ref, wuv_ref, o_ref,
                       m_ref, l_ref, acc_ref, s_ref, p_ref, a_ref, *, tq, tk):
    n = pl.program_id(0)
    scratch = (m_ref, l_ref, acc_ref, s_ref, p_ref, a_ref)

    @pl.when(fl_ref[n] % 2 == 1)
    def _():
        _mla_init(m_ref, l_ref, acc_ref)

    @pl.when(fl_ref[n] < 2)
    def _():
        _mla_prompt_tile(ql_ref, qr_ref, kl_ref[...], kr_ref[...], False, 0, 0, *scratch, tq, tk)

    @pl.when(fl_ref[n] >= 2)
    def _():
        _mla_prompt_tile(ql_ref, qr_ref, kl_ref[...], kr_ref[...], True, qp_ref[n], kp_ref[n],
                         *scratch, tq, tk)
        _mla_prompt_finalize(wuv_ref, o_ref, l_ref, acc_ref, tq)


def _mla_prompt(q_lat, q_rope, k_lat, k_rope, w_uv, batch, seq):
    h, t, c = q_lat.shape
    tq = _pick_tile(seq, MLA_Q_TILE, CHUNK)
    tk = _pick_tile(seq, MLA_K_TILE, CHUNK)
    nq, nk = seq // tq, seq // tk
    qb, kb, qp, kp, fl = [], [], [], [], []
    for b in range(batch):
        for qi in range(nq):
            last = ((qi + 1) * tq - 1) // tk
            for kj in range(last + 1):
                qb.append(b * nq + qi)
                kb.append(b * nk + kj)
                qp.append(qi * tq)
                kp.append(kj * tk)
                fl.append((1 if kj == 0 else 0) + (2 if kj == last else 0))
    tabs = [jnp.asarray(np.asarray(a, np.int32)) for a in (qb, kb, qp, kp, fl)]
    rows = h * tq
    grid_spec = pltpu.PrefetchScalarGridSpec(
        num_scalar_prefetch=5,
        grid=(len(qb),),
        in_specs=[
            pl.BlockSpec((h, tq, c), lambda n, qb, kb, qp, kp, fl: (0, qb[n], 0)),
            pl.BlockSpec((h, tq, ROPE_DIM), lambda n, qb, kb, qp, kp, fl: (0, qb[n], 0)),
            pl.BlockSpec((tk, c), lambda n, qb, kb, qp, kp, fl: (kb[n], 0)),
            pl.BlockSpec((tk, ROPE_DIM), lambda n, qb, kb, qp, kp, fl: (kb[n], 0)),
            pl.BlockSpec((h, c, V_DIM), lambda n, qb, kb, qp, kp, fl: (0, 0, 0)),
        ],
        out_specs=pl.BlockSpec((tq, h * V_DIM), lambda n, qb, kb, qp, kp, fl: (qb[n], 0)),
        scratch_shapes=[pltpu.VMEM((1, rows), F32), pltpu.VMEM((1, rows), F32),
                        pltpu.VMEM((c, rows), F32), pltpu.VMEM((tk, rows), F32),
                        pltpu.VMEM((tk, rows), BF16), pltpu.VMEM((1, rows), F32)],
    )
    return pl.pallas_call(
        functools.partial(_mla_prompt_kernel, tq=tq, tk=tk),
        grid_spec=grid_spec,
        out_shape=jax.ShapeDtypeStruct((batch * seq, h * V_DIM), BF16),
        compiler_params=_cparams(("arbitrary",)),
        name="mla_prompt",
    )(*tabs, q_lat, q_rope, k_lat, k_rope, w_uv)


def _mla_sample_kernel(ql_ref, qr_ref, pl_ref, pr_ref, nl_ref, nr_ref, wuv_ref, o_ref,
                       m_ref, l_ref, acc_ref, *, tq, n_past, n_kt):
    j = pl.program_id(1)
    rows = MLA_HEADS * tq
    q_lat = ql_ref[...].reshape(rows, ql_ref.shape[-1])
    q_rope = qr_ref[...].reshape(rows, qr_ref.shape[-1])

    @pl.when(j == 0)
    def _():
        _mla_init(m_ref, l_ref, acc_ref)

    @pl.when(j < n_kt)
    def _():
        klat = pl_ref[...].astype(BF16)
        s = _dot_nt(q_lat, klat) + _dot_nt(q_rope, pr_ref[...].astype(BF16))
        _mla_update(s, klat, m_ref, l_ref, acc_ref)

    @pl.when(j == n_kt)
    def _():
        klat = nl_ref[...]
        s = _dot_nt(q_lat, klat) + _dot_nt(q_rope, nr_ref[...])
        s = jnp.where(_chunk_visible(n_past, n_past, rows, tq, tq), s, NEG_BIG)
        _mla_update(s, klat, m_ref, l_ref, acc_ref)
        _mla_finalize(wuv_ref, o_ref, l_ref, acc_ref, tq)


def _mla_sample(q_lat, q_rope, past_lat, past_kr, layer, k_lat, k_rope, w_uv,
                row0, dec_batch, dec_seq):
    h, t, c = q_lat.shape
    n_past = past_lat.shape[2]
    tk = _pick_tile(n_past, MLA_K_TILE, 128)
    n_kt = n_past // tk
    blk0 = row0 // dec_seq
    rows = h * dec_seq
    past_idx = lambda b, j: (layer, b, jnp.minimum(j, n_kt - 1), 0)
    return pl.pallas_call(
        functools.partial(_mla_sample_kernel, tq=dec_seq, n_past=n_past, n_kt=n_kt),
        grid=(dec_batch, n_kt + 1),
        in_specs=[
            pl.BlockSpec((h, dec_seq, c), lambda b, j: (0, blk0 + b, 0)),
            pl.BlockSpec((h, dec_seq, ROPE_DIM), lambda b, j: (0, blk0 + b, 0)),
            pl.BlockSpec((None, None, tk, c), past_idx),
            pl.BlockSpec((None, None, tk, ROPE_DIM), past_idx),
            pl.BlockSpec((dec_seq, c), lambda b, j: (blk0 + b, 0)),
            pl.BlockSpec((dec_seq, ROPE_DIM), lambda b, j: (blk0 + b, 0)),
            pl.BlockSpec((h, c, V_DIM), lambda b, j: (0, 0, 0)),
        ],
        out_specs=pl.BlockSpec((dec_seq, h * V_DIM), lambda b, j: (b, 0)),
        out_shape=jax.ShapeDtypeStruct((dec_batch * dec_seq, h * V_DIM), BF16),
        scratch_shapes=[pltpu.VMEM((rows, 1), F32), pltpu.VMEM((rows, 1), F32),
                        pltpu.VMEM((rows, c), F32)],
        compiler_params=_cparams(("parallel", "arbitrary")),
        name="mla_sample",
    )(q_lat, q_rope, past_lat, past_kr, k_lat, k_rope, w_uv)


def _scan_matrix(sub):
    j = np.arange(sub)[:, None]
    s = np.arange(sub)[None, :]
    u = (j > s).astype(np.float32)
    return jnp.asarray(np.concatenate([u, u], axis=0), dtype=BF16)


_SbChunk = collections.namedtuple("_SbChunk", "q k v before carry_idx acc_idx")


def _sb_scores(qc, kc, before):
    z = _dot_nt(qc, kc)
    softplus = jnp.log(1.0 + jnp.exp2(-jnp.abs(z))) * LOG2E
    log_beta = jnp.minimum(z, 0.0) - softplus
    log_rest = log_beta - z
    if before is not None:
        log_rest = jnp.where(before, log_rest, 0.0)
    hi = log_rest.astype(BF16)
    lo = (log_rest - hi.astype(F32)).astype(BF16)
    return (log_beta, jnp.concatenate([hi, lo], axis=-1),
            jnp.sum(log_rest, axis=-1, keepdims=True))


def _sb_weights(log_beta, hi_lo, u2, carry, before):
    a = jnp.exp2(log_beta + _dot(hi_lo, u2) + carry)
    if before is not None:
        a = jnp.where(before, a, 0.0)
    return a.astype(BF16)


def _sb_pipeline(chunks, u2, acc_ref, carry_ref):
    scored = None
    weighted = None
    for j in range(len(chunks) + 2):
        new_scored = None
        if j < len(chunks):
            ch = chunks[j]
            new_scored = (ch,) + _sb_scores(ch.q(), ch.k(), ch.before)
        new_weighted = None
        if scored is not None:
            ch, log_beta, hi_lo, row_sum = scored
            carry = carry_ref[ch.carry_idx]
            new_weighted = (ch, _sb_weights(log_beta, hi_lo, u2, carry, ch.before))
            carry_ref[ch.carry_idx] = carry + row_sum
        if weighted is not None:
            ch, a = weighted
            acc_ref[ch.acc_idx] += _dot(a, ch.v())
        scored, weighted = new_scored, new_weighted


def _sb_tile(q_ref, k_ref, v_ref, u2, acc_ref, carry_ref, diagonal, tq, tk, sub, rq):
    chunks = []
    for c in reversed(range(tk // sub)):
        k0, k1 = c * sub, (c + 1) * sub
        for r in range(tq // rq):
            r0, r1 = r * rq, (r + 1) * rq
            before = None
            if diagonal:
                if k0 >= r1 - 1:
                    continue
                if k1 > r0:
                    before = ((k0 + lax.broadcasted_iota(jnp.int32, (1, sub), 1))
                              < (r0 + lax.broadcasted_iota(jnp.int32, (rq, 1), 0)))
            for h in range(SB_HEADS):
                cols = slice(h * SB_DIM, (h + 1) * SB_DIM)
                rows, keys = slice(r0, r1), slice(k0, k1)
                chunks.append(_SbChunk(
                    q=functools.partial(lambda rr, cc: q_ref[rr, cc], rows, cols),
                    k=functools.partial(lambda kk, cc: k_ref[kk, cc], keys, cols),
                    v=functools.partial(lambda kk, cc: v_ref[kk, cc], keys, cols),
                    before=before, carry_idx=(h, rows, slice(None)), acc_idx=(rows, cols)))
    _sb_pipeline(chunks, u2, acc_ref, carry_ref)


def _sb_prompt_kernel(qb_ref, kb_ref, fl_ref,
                      q_ref, k_ref, v_ref, u_ref, o_ref, acc_ref, carry_ref, live_ref,
                      *, tq, tk, sub, rq):
    n = pl.program_id(0)

    @pl.when(fl_ref[n] % 2 == 1)
    def _():
        acc_ref[...] = jnp.zeros(acc_ref.shape, F32)
        carry_ref[...] = jnp.zeros(carry_ref.shape, F32)
        _sb_tile(q_ref, k_ref, v_ref, u_ref[...], acc_ref, carry_ref, True, tq, tk, sub, rq)
        live_ref[0] = (jnp.max(carry_ref[...]) > SB_DEAD_LOG2).astype(jnp.int32)

    @pl.when(jnp.logical_and(fl_ref[n] % 2 == 0, live_ref[0] == 1))
    def _():
        _sb_tile(q_ref, k_ref, v_ref, u_ref[...], acc_ref, carry_ref, False, tq, tk, sub, rq)
        live_ref[0] = (jnp.max(carry_ref[...]) > SB_DEAD_LOG2).astype(jnp.int32)

    @pl.when(fl_ref[n] >= 2)
    def _():
        o_ref[...] = acc_ref[...].astype(BF16)


def _sb_prompt(q, k, v, batch, seq):
    width = q.shape[1]
    tq = _pick_tile(seq, SB_TILE, 128)
    tk = tq
    sub = _pick_tile(tk, SB_SUB, 128)
    rq = _pick_tile(tq, SB_ROWS, 16)
    nq = seq // tq
    qb, kb, fl = [], [], []
    for b in range(batch):
        for qi in range(nq):
            for kj in range(qi, -1, -1):
                qb.append(b * nq + qi)
                kb.append(b * nq + kj)
                fl.append((1 if kj == qi else 0) + (2 if kj == 0 else 0))
    tabs = [jnp.asarray(np.asarray(a, np.int32)) for a in (qb, kb, fl)]
    grid_spec = pltpu.PrefetchScalarGridSpec(
        num_scalar_prefetch=3,
        grid=(len(qb),),
        in_specs=[
            pl.BlockSpec((tq, width), lambda n, qb, kb, fl: (qb[n], 0)),
            pl.BlockSpec((tk, width), lambda n, qb, kb, fl: (kb[n], 0)),
            pl.BlockSpec((tk, width), lambda n, qb, kb, fl: (kb[n], 0)),
            pl.BlockSpec((2 * sub, sub), lambda n, qb, kb, fl: (0, 0)),
        ],
        out_specs=pl.BlockSpec((tq, width), lambda n, qb, kb, fl: (qb[n], 0)),
        scratch_shapes=[pltpu.VMEM((tq, width), F32), pltpu.VMEM((SB_HEADS, tq, 1), F32),
                        pltpu.SMEM((1,), jnp.int32)],
    )
    return pl.pallas_call(
        functools.partial(_sb_prompt_kernel, tq=tq, tk=tk, sub=sub, rq=rq),
        grid_spec=grid_spec,
        out_shape=jax.ShapeDtypeStruct((batch * seq, width), BF16),
        compiler_params=_cparams(("arbitrary",)),
        name="sb_prompt",
    )(*tabs, q, k, v, _scan_matrix(sub))


def _sb_sample_kernel(q_ref, nk_ref, nv_ref, un_ref, up_ref, pk_hbm, pv_hbm, o_ref,
                      acc_ref, carry_ref, kbuf, vbuf, sem, *, layer, tq, tk, n_kt):
    b = pl.program_id(0)

    def tile_copies(i, slot):
        start = (n_kt - 1 - i) * tk
        copies = []
        for h in range(SB_HEADS):
            copies.append(pltpu.make_async_copy(
                pk_hbm.at[layer, b, pl.ds(start, tk), h, :], kbuf.at[slot, h], sem.at[0, slot]))
            copies.append(pltpu.make_async_copy(
                pv_hbm.at[layer, b, pl.ds(start, tk), h, :], vbuf.at[slot, h], sem.at[1, slot]))
        return copies

    def start_tile(i, slot):
        for copy in tile_copies(i, slot):
            copy.start()

    def wait_tile(i, slot):
        for copy in tile_copies(i, slot):
            copy.wait()

    def alive():
        return (jnp.max(carry_ref[...]) > SB_DEAD_LOG2).astype(jnp.int32)

    start_tile(0, 0)

    acc_ref[...] = jnp.zeros(acc_ref.shape, F32)
    carry_ref[...] = jnp.zeros(carry_ref.shape, F32)
    before = (lax.broadcasted_iota(jnp.int32, (1, tq), 1)
              < lax.broadcasted_iota(jnp.int32, (tq, 1), 0))
    chunks = []
    for h in range(SB_HEADS):
        cols = slice(h * SB_DIM, (h + 1) * SB_DIM)
        chunks.append(_SbChunk(
            q=functools.partial(lambda cc: q_ref[:, cc], cols),
            k=functools.partial(lambda cc: nk_ref[:, cc], cols),
            v=functools.partial(lambda cc: nv_ref[:, cc], cols),
            before=before, carry_idx=(h, slice(None), slice(None)), acc_idx=(slice(None), cols)))
    _sb_pipeline(chunks, un_ref[...], acc_ref, carry_ref)

    def tile_step(state):
        i, _ = state
        slot = lax.rem(i, 2)
        wait_tile(i, slot)

        @pl.when(i + 1 < n_kt)
        def _():
            start_tile(i + 1, 1 - slot)

        chunks = []
        for h in range(SB_HEADS):
            cols = slice(h * SB_DIM, (h + 1) * SB_DIM)
            chunks.append(_SbChunk(
                q=functools.partial(lambda cc: q_ref[:, cc], cols),
                k=functools.partial(lambda hh: kbuf[slot, hh].astype(BF16), h),
                v=functools.partial(lambda hh: vbuf[slot, hh].astype(BF16), h),
                before=None, carry_idx=(h, slice(None), slice(None)), acc_idx=(slice(None), cols)))
        _sb_pipeline(chunks, up_ref[...], acc_ref, carry_ref)
        return i + 1, alive()

    tiles_done, _ = lax.while_loop(
        lambda state: jnp.logical_and(state[0] < n_kt, state[1] == 1),
        tile_step, (jnp.int32(0), alive()))

    @pl.when(tiles_done < n_kt)
    def _():
        wait_tile(tiles_done, lax.rem(tiles_done, 2))

    o_ref[...] = acc_ref[...].astype(BF16)


def _sb_sample(q, past_k, past_v, layer, k, v, row0, dec_batch, dec_seq):
    width = q.shape[1]
    n_past = past_k.shape[2]
    tk = _pick_tile(n_past, SB_SUB, 128)
    n_kt = n_past // tk
    blk0 = row0 // dec_seq
    new_rows = pl.BlockSpec((dec_seq, width), lambda b: (blk0 + b, 0))
    in_hbm = pl.BlockSpec(memory_space=pl.ANY)
    return pl.pallas_call(
        functools.partial(_sb_sample_kernel, layer=layer, tq=dec_seq, tk=tk, n_kt=n_kt),
        grid=(dec_batch,),
        in_specs=[
            new_rows, new_rows, new_rows,
            pl.BlockSpec((2 * dec_seq, dec_seq), lambda b: (0, 0)),
            pl.BlockSpec((2 * tk, tk), lambda b: (0, 0)),
            in_hbm, in_hbm,
        ],
        out_specs=pl.BlockSpec((dec_seq, width), lambda b: (b, 0)),
        out_shape=jax.ShapeDtypeStruct((dec_batch * dec_seq, width), BF16),
        scratch_shapes=[pltpu.VMEM((dec_seq, width), F32),
                        pltpu.VMEM((SB_HEADS, dec_seq, 1), F32),
                        pltpu.VMEM((2, SB_HEADS, tk, SB_DIM), F32),
                        pltpu.VMEM((2, SB_HEADS, tk, SB_DIM), F32),
                        pltpu.SemaphoreType.DMA((2, 2))],
        compiler_params=_cparams(("arbitrary",)),
        name="sb_sample",
    )(q, k, v, _scan_matrix(dec_seq), _scan_matrix(tk), past_k, past_v)


def _merge_kernel(oap_ref, oas_ref, obp_ref, obs_ref, wa_ref, wb_ref, g_ref, o_ref,
                  *, n_p, tn):
    i = pl.program_id(0)
    o_a = _load_split(i, n_p, oap_ref, oas_ref)
    o_b = _load_split(i, n_p, obp_ref, obs_ref)
    n = o_ref.shape[1]
    for c in range(n // tn):
        cols = slice(c * tn, (c + 1) * tn)
        a = _dot(o_a, wa_ref[:, cols])
        b = _dot(o_b, wb_ref[:, cols])
        g_a = g_ref[:, cols].astype(F32)
        g_b = g_ref[:, n + c * tn:n + (c + 1) * tn].astype(F32)
        o_ref[:, cols] = (g_a * a + g_b * b).astype(BF16)


def _merge(oa_p, oa_s, ob_p, ob_s, w_a, w_b, gates, tm):
    da = oa_p.shape[1]
    db = ob_p.shape[1]
    t = oa_p.shape[0] + oa_s.shape[0]
    n_p = oa_p.shape[0] // tm
    n = w_a.shape[1]
    tn = _pick_tile(n, COL_TILE, 128)
    return pl.pallas_call(
        functools.partial(_merge_kernel, n_p=n_p, tn=tn),
        grid=(t // tm,),
        in_specs=[*_split_specs(tm, da, n_p),
                  *_split_specs(tm, db, n_p),
                  _resident((da, n)), _resident((db, n)),
                  pl.BlockSpec((tm, 2 * n), lambda i: (i, 0))],
        out_specs=pl.BlockSpec((tm, n), lambda i: (i, 0)),
        out_shape=jax.ShapeDtypeStruct((t, n), BF16),
        compiler_params=_cparams(("parallel",)),
        name="merge",
    )(oa_p, oa_s, ob_p, ob_s, w_a, w_b, gates)


def _res_ln_finalize(z_ref, g_ref, b_ref, o_ref, ob_ref):
    n_tiles, _, tn = z_ref.shape
    width = n_tiles * tn
    total = z_ref[0].sum(axis=-1, keepdims=True)
    for c in range(1, n_tiles):
        total += z_ref[c].sum(axis=-1, keepdims=True)
    mu = total / width
    sq = jnp.square(z_ref[0] - mu).sum(axis=-1, keepdims=True)
    for c in range(1, n_tiles):
        sq += jnp.square(z_ref[c] - mu).sum(axis=-1, keepdims=True)
    rstd = lax.rsqrt(sq / width + LN_EPS)
    for c in range(n_tiles):
        cols = slice(c * tn, (c + 1) * tn)
        y = (z_ref[c] - mu) * rstd * g_ref[:, cols] + b_ref[:, cols]
        o_ref[:, cols] = y
        if ob_ref is not None:
            ob_ref[:, cols] = y.astype(BF16)


def _out_ln_kernel(m_ref, w_ref, x_ref, g_ref, b_ref, o_ref, ob_ref, z_ref, *, alpha):
    m = m_ref[...]
    n_tiles, _, tn = z_ref.shape
    for c in range(n_tiles):
        cols = slice(c * tn, (c + 1) * tn)
        z_ref[c] = alpha * x_ref[:, cols] + _dot(m, w_ref[:, cols])
    _res_ln_finalize(z_ref, g_ref, b_ref, o_ref, ob_ref)


def _out_ln(merged, w, x, g, b, alpha, tm):
    t, d = x.shape
    k = merged.shape[1]
    tn = _pick_tile(d, COL_TILE, 128)
    full_row = pl.BlockSpec((tm, d), lambda i: (i, 0))
    vec = pl.BlockSpec((1, d), lambda i: (0, 0))
    return pl.pallas_call(
        functools.partial(_out_ln_kernel, alpha=alpha),
        grid=(t // tm,),
        in_specs=[pl.BlockSpec((tm, k), lambda i: (i, 0)), _resident((k, d)), full_row, vec, vec],
        out_specs=[full_row, full_row],
        out_shape=[jax.ShapeDtypeStruct((t, d), F32), jax.ShapeDtypeStruct((t, d), BF16)],
        scratch_shapes=[pltpu.VMEM((d // tn, tm, tn), F32)],
        compiler_params=_cparams(("parallel",)),
        name="out_ln",
    )(merged, w, x, g.reshape(1, d), b.reshape(1, d))


def _ffn_ln_kernel(xb_ref, wg_ref, wu_ref, wd_ref, x_ref, g_ref, b_ref, o_ref, ob_ref,
                   acc_ref, *, alpha):
    f = pl.program_id(1)

    @pl.when(f == 0)
    def _():
        acc_ref[...] = jnp.zeros(acc_ref.shape, F32)

    xb = xb_ref[...]
    gate = _dot(xb, wg_ref[...])
    up = _dot(xb, wu_ref[...])
    hidden = (gate * _sigmoid(gate) * up).astype(BF16)
    acc_ref[...] += _dot(hidden, wd_ref[...])

    @pl.when(f == pl.num_programs(1) - 1)
    def _():
        y = _layer_norm_rows(alpha * x_ref[...] + acc_ref[...], g_ref[...], b_ref[...])
        o_ref[...] = y
        ob_ref[...] = y.astype(BF16)


def _ffn_ln(xb, w_gu, w_down, x, g, b, alpha, tm):
    t, d = x.shape
    ff = w_down.shape[0]
    tf = _pick_tile(ff, COL_TILE, 128)
    nf = ff // tf
    full_row = pl.BlockSpec((tm, d), lambda i, f: (i, 0))
    vec = pl.BlockSpec((1, d), lambda i, f: (0, 0))
    return pl.pallas_call(
        functools.partial(_ffn_ln_kernel, alpha=alpha),
        grid=(t // tm, nf),
        in_specs=[full_row,
                  pl.BlockSpec((d, tf), lambda i, f: (0, f)),
                  pl.BlockSpec((d, tf), lambda i, f: (0, f + nf)),
                  pl.BlockSpec((tf, d), lambda i, f: (f, 0)),
                  full_row, vec, vec],
        out_specs=[full_row, full_row],
        out_shape=[jax.ShapeDtypeStruct((t, d), F32), jax.ShapeDtypeStruct((t, d), BF16)],
        scratch_shapes=[pltpu.VMEM((tm, d), F32)],
        compiler_params=_cparams(("parallel", "arbitrary")),
        name="ffn_ln",
    )(xb, w_gu, w_gu, w_down, x, g.reshape(1, d), b.reshape(1, d))


def _ple_ln_kernel(xb_ref, wg_ref, bg_ref, pp_ref, ps_ref, wp_ref, x_ref, g_ref, b_ref,
                   o1_ref, o2_ref, z_ref, *, alpha, n_p, last):
    i = pl.program_id(0)
    xb = xb_ref[...]
    p = _load_split(i, n_p, pp_ref, ps_ref).astype(BF16)
    n_tiles, _, tn = z_ref.shape
    for c in range(n_tiles):
        cols = slice(c * tn, (c + 1) * tn)
        gate = _sigmoid(_dot(xb, wg_ref[:, cols]) + bg_ref[:, cols])
        z_ref[c] = alpha * x_ref[:, cols] + gate * _dot(p, wp_ref[:, cols])

    if not last:
        _res_ln_finalize(z_ref, g_ref, b_ref, o1_ref, o2_ref)
    else:
        @pl.when(i < n_p)
        def _():
            _res_ln_finalize(z_ref, g_ref, b_ref, o1_ref, None)

        @pl.when(i >= n_p)
        def _():
            _res_ln_finalize(z_ref, g_ref, b_ref, o2_ref, None)


def _ple_ln(xb, w_gate, b_gate, p_p, p_s, layer, w_proj, x, g, b, alpha, tm, last):
    t, d = x.shape
    pd = p_p.shape[2]
    tp = p_p.shape[1]
    n_p = tp // tm
    tn = _pick_tile(d, COL_TILE, 128)
    full_row = pl.BlockSpec((tm, d), lambda i: (i, 0))
    vec = pl.BlockSpec((1, d), lambda i: (0, 0))
    if last:
        out_specs = list(_split_specs(tm, d, n_p))
        out_shape = [jax.ShapeDtypeStruct((tp, d), F32), jax.ShapeDtypeStruct((t - tp, d), F32)]
    else:
        out_specs = [full_row, full_row]
        out_shape = [jax.ShapeDtypeStruct((t, d), F32), jax.ShapeDtypeStruct((t, d), BF16)]
    return pl.pallas_call(
        functools.partial(_ple_ln_kernel, alpha=alpha, n_p=n_p, last=last),
        grid=(t // tm,),
        in_specs=[full_row, _resident((d, d)), vec,
                  pl.BlockSpec((None, tm, pd), lambda i: (layer, jnp.minimum(i, n_p - 1), 0)),
                  pl.BlockSpec((None, tm, pd), lambda i: (layer, jnp.maximum(i - n_p, 0), 0)),
                  _resident((pd, d)), full_row, vec, vec],
        out_specs=out_specs,
        out_shape=out_shape,
        scratch_shapes=[pltpu.VMEM((d // tn, tm, tn), F32)],
        compiler_params=_cparams(("arbitrary",)),
        name="ple_ln",
    )(xb, w_gate, b_gate.reshape(1, d), p_p, p_s, w_proj, x, g.reshape(1, d), b.reshape(1, d))


def _rope_tables(pos):
    half = ROPE_DIM // 2
    inv_freq = 1.0 / (ROPE_THETA ** (jnp.arange(half, dtype=F32) * (2.0 / ROPE_DIM)))
    ang = pos.astype(F32)[:, None] * inv_freq[None, :]
    return jnp.cos(ang), jnp.sin(ang)


def kernel(x_prompt, x_sample, cache_mla_latent, cache_mla_krope, cache_sb_k, cache_sb_v,
           p_prompt, p_sample, ln_in_g, ln_in_b, w_in, b_gate, q_a_norm_g, w_q_b,
           kv_a_norm_g, w_kv_b, w_branch_a, w_branch_b, w_out, ln1_g, ln1_b,
           w_ffn_gu, w_ffn_down, ln2_g, ln2_b, w_ple_gate, b_ple_gate, w_ple_proj,
           ln3_g, ln3_b):
    batch, seq, d = x_prompt.shape
    dec_batch, dec_seq, _ = x_sample.shape
    depth = w_in.shape[0]
    n_past = cache_mla_latent.shape[2]
    tp = batch * seq
    ts = dec_batch * dec_seq
    sb_width = SB_HEADS * SB_DIM
    alpha = (2 * depth) ** 0.25

    pos = jnp.concatenate([jnp.tile(jnp.arange(seq, dtype=jnp.int32), batch),
                           jnp.tile(n_past + jnp.arange(dec_seq, dtype=jnp.int32), dec_batch)])
    cos, sin = _rope_tables(pos)

    tm = _pick_tile(math.gcd(tp, ts), ROW_TILE, 16)
    p_p = p_prompt.reshape(depth, tp, -1)
    p_s = p_sample.reshape(depth, ts, -1)

    x, xb = _ln_in(x_prompt.reshape(tp, d), x_sample.reshape(ts, d), ln_in_g, ln_in_b, tm)

    splits = np.cumsum([0, Q_LORA, KV_LORA, ROPE_DIM, sb_width, sb_width, sb_width, 2 * d])
    new_rows = []
    for l in range(depth):
        w_cols = [w_in[l][:, splits[i]:splits[i + 1]].astype(BF16) for i in range(7)]
        w_qa, w_lat, w_kr, w_sq, w_sk, w_sv, w_gates = w_cols
        wq = w_q_b[l].reshape(Q_LORA, MLA_HEADS, NOPE_DIM + ROPE_DIM)
        wq_nope = wq[..., :NOPE_DIM].reshape(Q_LORA, MLA_HEADS * NOPE_DIM).astype(BF16)
        wq_rope = wq[..., NOPE_DIM:].reshape(Q_LORA, MLA_HEADS * ROPE_DIM).astype(BF16)
        w_kv = w_kv_b[l].reshape(KV_LORA, MLA_HEADS, NOPE_DIM + V_DIM)
        wuk_h = w_kv[..., :NOPE_DIM].transpose(1, 2, 0).astype(BF16)
        wuv_h = w_kv[..., NOPE_DIM:].transpose(1, 0, 2).astype(BF16)

        qa_n = _proj_qa(xb, w_qa, q_a_norm_g[l], tm)
        lat_p, lat_s, kr_p, kr_s, lat_b, kr_b = _proj_kv(xb, w_lat, w_kr, kv_a_norm_g[l],
                                                        cos, sin, tm, tp)
        sbq, k_p, k_s, v_p, v_s, sbk_b, sbv_b = _proj_sb(xb, w_sq, w_sk, w_sv, tm, tp)
        gates = _proj_gates(xb, w_gates, b_gate[l], tm)

        q_lat, q_rope = _q_heads(qa_n, wq_nope, wq_rope, wuk_h, cos, sin, tm)
        oa_p = _mla_prompt(q_lat, q_rope, lat_b, kr_b, wuv_h, batch, seq)
        oa_s = _mla_sample(q_lat, q_rope, cache_mla_latent, cache_mla_krope, l, lat_b, kr_b,
                           wuv_h, tp, dec_batch, dec_seq)
        ob_p = _sb_prompt(sbq, sbk_b, sbv_b, batch, seq)
        ob_s = _sb_sample(sbq, cache_sb_k, cache_sb_v, l, sbk_b, sbv_b, tp, dec_batch, dec_seq)

        merged = _merge(oa_p, oa_s, ob_p, ob_s, w_branch_a[l].astype(BF16),
                        w_branch_b[l].astype(BF16), gates, tm)
        x, xb = _out_ln(merged, w_out[l].astype(BF16), x, ln1_g[l], ln1_b[l], alpha, tm)
        x, xb = _ffn_ln(xb, w_ffn_gu[l].astype(BF16), w_ffn_down[l].astype(BF16), x,
                        ln2_g[l], ln2_b[l], alpha, tm)
        x, xb = _ple_ln(xb, w_ple_gate[l].astype(BF16), b_ple_gate[l], p_p, p_s, l,
                        w_ple_proj[l].astype(BF16), x, ln3_g[l], ln3_b[l], alpha, tm,
                        last=(l == depth - 1))

        new_rows.append((lat_p, kr_p, k_p, v_p, lat_s, kr_s, k_s, v_s))

    lat_p, kr_p, k_p, v_p, lat_s, kr_s, k_s, v_s = (jnp.stack(a) for a in zip(*new_rows))
    head_dims = (SB_HEADS, SB_DIM)
    return (x.reshape(batch, seq, d), xb.reshape(dec_batch, dec_seq, d),
            lat_p.reshape(depth, batch, seq, KV_LORA), kr_p.reshape(depth, batch, seq, ROPE_DIM),
            k_p.reshape(depth, batch, seq, *head_dims), v_p.reshape(depth, batch, seq, *head_dims),
            lat_s.reshape(depth, dec_batch, dec_seq, KV_LORA),
            kr_s.reshape(depth, dec_batch, dec_seq, ROPE_DIM),
            k_s.reshape(depth, dec_batch, dec_seq, *head_dims),
            v_s.reshape(depth, dec_batch, dec_seq, *head_dims))
```

```python
import collections
import functools
import math

import numpy as np
import jax
import jax.numpy as jnp
from jax import lax
from jax.experimental import pallas as pl
from jax.experimental.pallas import tpu as pltpu

CHUNK = 64
MLA_HEADS = 16
Q_LORA = 512
KV_LORA = 512
NOPE_DIM = 128
ROPE_DIM = 64
V_DIM = 128
ROPE_THETA = 10000.0
MLA_SCALE = (NOPE_DIM + ROPE_DIM) ** -0.5
SB_HEADS = 8
SB_DIM = 128
SB_SCALE = SB_DIM ** -0.5
LN_EPS = 1e-5
RMS_EPS = 1e-6
LOG2E = 1.4426950408889634
MLA_QSCALE = MLA_SCALE * LOG2E
SB_QSCALE = SB_SCALE * LOG2E
SB_DEAD_LOG2 = -160.0

BF16 = jnp.bfloat16
F32 = jnp.float32

V7X_VMEM_LIMIT_BYTES = 56 * 1024 * 1024
ROW_TILE = 512
COL_TILE = 512
PROJ_COL_TILE = 1024
MLA_Q_TILE = 256
MLA_K_TILE = 512
MLA_SOFTMAX_COLS = 128
SB_TILE = 512
SB_SUB = 256
SB_ROWS = 128
NEG_BIG = -1e30


def _pick_tile(n, target, mult):
    best = None
    for t in range(mult, min(n, target) + 1, mult):
        if n % t == 0:
            best = t
    if best is None:
        return n
    return best


def _cparams(sem):
    return pltpu.CompilerParams(dimension_semantics=sem,
                                vmem_limit_bytes=V7X_VMEM_LIMIT_BYTES)


def _resident(shape):
    return pl.BlockSpec(shape, lambda *_: (0,) * len(shape), pipeline_mode=pl.Buffered(1))


def _dot(a, b):
    return jnp.dot(a, b, preferred_element_type=F32)


def _dot_nt(a, b):
    return lax.dot_general(a, b, (((1,), (1,)), ((), ())), preferred_element_type=F32)


def _layer_norm_rows(z, g, b):
    mu = jnp.mean(z, axis=-1, keepdims=True)
    zc = z - mu
    var = jnp.mean(zc * zc, axis=-1, keepdims=True)
    return zc * lax.rsqrt(var + LN_EPS) * g + b


def _rms_norm_rows(z, g):
    return z * lax.rsqrt(jnp.mean(z * z, axis=-1, keepdims=True) + RMS_EPS) * g


def _sigmoid(z):
    return 1.0 / (1.0 + jnp.exp(-z))


def _rope_rows(x, cos, sin):
    half = ROPE_DIM // 2
    x1 = x[:, :half]
    x2 = x[:, half:]
    return jnp.concatenate([x1 * cos - x2 * sin, x2 * cos + x1 * sin], axis=-1)


def _split_specs(tm, width, n_p):
    return (pl.BlockSpec((tm, width), lambda i, *_: (jnp.minimum(i, n_p - 1), 0)),
            pl.BlockSpec((tm, width), lambda i, *_: (jnp.maximum(i - n_p, 0), 0)))


def _load_split(i, n_p, p_ref, s_ref):
    return jnp.where(i < n_p, p_ref[...], s_ref[...])


def _store_split(i, n_p, p_ref, s_ref, value):
    @pl.when(i < n_p)
    def _():
        p_ref[...] = value

    @pl.when(i >= n_p)
    def _():
        s_ref[...] = value


def _ln_in_kernel(xp_ref, xs_ref, g_ref, b_ref, o_ref, ob_ref, *, n_p):
    x = _load_split(pl.program_id(0), n_p, xp_ref, xs_ref)
    y = _layer_norm_rows(x, g_ref[...], b_ref[...])
    o_ref[...] = y
    ob_ref[...] = y.astype(BF16)


def _ln_in(x_p, x_s, g, b, tm):
    d = x_p.shape[1]
    t = x_p.shape[0] + x_s.shape[0]
    n_p = x_p.shape[0] // tm
    row = pl.BlockSpec((tm, d), lambda i: (i, 0))
    vec = pl.BlockSpec((1, d), lambda i: (0, 0))
    return pl.pallas_call(
        functools.partial(_ln_in_kernel, n_p=n_p),
        grid=(t // tm,),
        in_specs=[*_split_specs(tm, d, n_p), vec, vec],
        out_specs=[row, row],
        out_shape=[jax.ShapeDtypeStruct((t, d), F32), jax.ShapeDtypeStruct((t, d), BF16)],
        compiler_params=_cparams(("parallel",)),
        name="ln_in",
    )(x_p, x_s, g.reshape(1, d), b.reshape(1, d))


def _proj_qa_kernel(x_ref, w_ref, g_ref, o_ref):
    o_ref[...] = _rms_norm_rows(_dot(x_ref[...], w_ref[...]), g_ref[...]).astype(BF16)


def _proj_kv_kernel(x_ref, wl_ref, wr_ref, g_ref, cos_ref, sin_ref, *refs, n_p):
    latp_ref, lats_ref, krp_ref, krs_ref, latb_ref, krb_ref = refs[-6:]
    i = pl.program_id(0)
    x = x_ref[...]
    lat = _rms_norm_rows(_dot(x, wl_ref[...]), g_ref[...])
    kr = _rope_rows(_dot(x, wr_ref[...]), cos_ref[...], sin_ref[...])
    _store_split(i, n_p, latp_ref, lats_ref, lat)
    _store_split(i, n_p, krp_ref, krs_ref, kr)
    latb_ref[...] = lat.astype(BF16)
    krb_ref[...] = kr.astype(BF16)


def _proj_sb_kernel(x_ref, wq_ref, wk_ref, wv_ref, *refs, n_p):
    kp_ref, ks_ref, vp_ref, vs_ref, q_ref, kb_ref, vb_ref = refs[-7:]
    i = pl.program_id(0)
    x = x_ref[...]
    q_ref[...] = (_dot(x, wq_ref[...]) * SB_QSCALE).astype(BF16)
    k = _dot(x, wk_ref[...])
    _store_split(i, n_p, kp_ref, ks_ref, k)
    kb_ref[...] = k.astype(BF16)
    v = _dot(x, wv_ref[...])
    _store_split(i, n_p, vp_ref, vs_ref, v)
    vb_ref[...] = v.astype(BF16)


def _proj_gate_kernel(x_ref, w_ref, b_ref, o_ref, *, tn):
    x = x_ref[...]
    for c in range(o_ref.shape[1] // tn):
        cols = slice(c * tn, (c + 1) * tn)
        o_ref[:, cols] = _sigmoid(_dot(x, w_ref[:, cols]) + b_ref[:, cols]).astype(BF16)


def _proj_qa(xb, w, g, tm):
    t, d = xb.shape
    n = w.shape[1]
    return pl.pallas_call(
        _proj_qa_kernel,
        grid=(t // tm,),
        in_specs=[pl.BlockSpec((tm, d), lambda i: (i, 0)),
                  pl.BlockSpec((d, n), lambda i: (0, 0)),
                  pl.BlockSpec((1, n), lambda i: (0, 0))],
        out_specs=pl.BlockSpec((tm, n), lambda i: (i, 0)),
        out_shape=jax.ShapeDtypeStruct((t, n), BF16),
        compiler_params=_cparams(("parallel",)),
        name="proj_qa",
    )(xb, w, g.reshape(1, n))


def _cache_outputs(widths, tm, tp, ts, layer, depth, prev, n_inputs):
    n_p = tp // tm
    out_specs, out_shape = [], []
    for w in widths:
        out_specs += [
            pl.BlockSpec((None, tm, w), lambda i: (layer, jnp.minimum(i, n_p - 1), 0)),
            pl.BlockSpec((None, tm, w), lambda i: (layer, jnp.maximum(i - n_p, 0), 0))]
        out_shape += [jax.ShapeDtypeStruct((depth, tp, w), F32),
                      jax.ShapeDtypeStruct((depth, ts, w), F32)]
    extra = list(prev)
    extra_specs = [pl.BlockSpec(memory_space=pl.ANY) for _ in extra]
    aliases = {n_inputs + k: k for k in range(len(extra))}
    return extra, extra_specs, out_specs, out_shape, aliases


def _proj_kv(xb, w_lat, w_kr, g, cos, sin, tm, tp, layer, depth, prev):
    t, d = xb.shape
    c = w_lat.shape[1]
    r = w_kr.shape[1]
    rows = lambda w: pl.BlockSpec((tm, w), lambda i: (i, 0))
    full = lambda a, b: pl.BlockSpec((a, b), lambda i: (0, 0))
    extra, extra_specs, out_specs, out_shape, aliases = _cache_outputs(
        (c, r), tm, tp, t - tp, layer, depth, prev, 6)
    return pl.pallas_call(
        functools.partial(_proj_kv_kernel, n_p=tp // tm),
        grid=(t // tm,),
        in_specs=[rows(d), full(d, c), full(d, r), full(1, c), rows(r // 2), rows(r // 2)]
                 + extra_specs,
        out_specs=out_specs + [rows(c), rows(r)],
        out_shape=out_shape + [jax.ShapeDtypeStruct((t, c), BF16),
                               jax.ShapeDtypeStruct((t, r), BF16)],
        input_output_aliases=aliases,
        compiler_params=_cparams(("arbitrary",)),
        name="proj_kv",
    )(xb, w_lat, w_kr, g.reshape(1, c), cos, sin, *extra)


def _proj_sb(xb, w_q, w_k, w_v, tm, tp, layer, depth, prev):
    t, d = xb.shape
    n = w_q.shape[1]
    rows = lambda w: pl.BlockSpec((tm, w), lambda i: (i, 0))
    weight = _resident((d, n))
    bf16 = jax.ShapeDtypeStruct((t, n), BF16)
    extra, extra_specs, out_specs, out_shape, aliases = _cache_outputs(
        (n, n), tm, tp, t - tp, layer, depth, prev, 4)
    return pl.pallas_call(
        functools.partial(_proj_sb_kernel, n_p=tp // tm),
        grid=(t // tm,),
        in_specs=[rows(d), weight, weight, weight] + extra_specs,
        out_specs=out_specs + [rows(n), rows(n), rows(n)],
        out_shape=out_shape + [bf16, bf16, bf16],
        input_output_aliases=aliases,
        compiler_params=_cparams(("arbitrary",)),
        name="proj_sb",
    )(xb, w_q, w_k, w_v, *extra)


def _proj_gates(xb, w, bias, tm):
    t, d = xb.shape
    n = w.shape[1]
    tn = _pick_tile(n, PROJ_COL_TILE, 128)
    return pl.pallas_call(
        functools.partial(_proj_gate_kernel, tn=tn),
        grid=(t // tm,),
        in_specs=[pl.BlockSpec((tm, d), lambda i: (i, 0)), _resident((d, n)),
                  pl.BlockSpec((1, n), lambda i: (0, 0))],
        out_specs=pl.BlockSpec((tm, n), lambda i: (i, 0)),
        out_shape=jax.ShapeDtypeStruct((t, n), BF16),
        compiler_params=_cparams(("parallel",)),
        name="proj_gates",
    )(xb, w, bias.reshape(1, n))


def _q_heads_kernel(qa_ref, wn_ref, wr_ref, wuk_ref, cos_ref, sin_ref, ql_ref, qr_ref):
    qa = qa_ref[...]
    q_nope = _dot(qa, wn_ref[...]).astype(BF16)
    q_rope = _dot(qa, wr_ref[...])
    cos = cos_ref[...]
    sin = sin_ref[...]
    for h in range(MLA_HEADS):
        nope = q_nope[:, h * NOPE_DIM:(h + 1) * NOPE_DIM]
        ql_ref[h] = (_dot(nope, wuk_ref[h]) * MLA_QSCALE).astype(BF16)
        qr = _rope_rows(q_rope[:, h * ROPE_DIM:(h + 1) * ROPE_DIM], cos, sin)
        qr_ref[h] = (qr * MLA_QSCALE).astype(BF16)


def _q_heads(qa_n, w_nope, w_rope, wuk_h, cos, sin, tm):
    t, ql = qa_n.shape
    h, _, c = wuk_h.shape
    half = ROPE_DIM // 2
    return pl.pallas_call(
        _q_heads_kernel,
        grid=(t // tm,),
        in_specs=[pl.BlockSpec((tm, ql), lambda i: (i, 0)),
                  pl.BlockSpec((ql, h * NOPE_DIM), lambda i: (0, 0)),
                  pl.BlockSpec((ql, h * ROPE_DIM), lambda i: (0, 0)),
                  pl.BlockSpec((h, NOPE_DIM, c), lambda i: (0, 0, 0)),
                  pl.BlockSpec((tm, half), lambda i: (i, 0)),
                  pl.BlockSpec((tm, half), lambda i: (i, 0))],
        out_specs=[pl.BlockSpec((h, tm, c), lambda i: (0, i, 0)),
                   pl.BlockSpec((h, tm, ROPE_DIM), lambda i: (0, i, 0))],
        out_shape=[jax.ShapeDtypeStruct((h, t, c), BF16),
                   jax.ShapeDtypeStruct((h, t, ROPE_DIM), BF16)],
        compiler_params=_cparams(("parallel",)),
        name="q_heads",
    )(qa_n, w_nope, w_rope, wuk_h, cos, sin)


def _mla_init(m_ref, l_ref, acc_ref):
    m_ref[...] = jnp.full(m_ref.shape, NEG_BIG, F32)
    l_ref[...] = jnp.zeros(l_ref.shape, F32)
    acc_ref[...] = jnp.zeros(acc_ref.shape, F32)


def _mla_prompt_tile(ql_ref, qr_ref, klat, kr, masked, q_pos0, k_pos0,
                     m_ref, l_ref, acc_ref, s_ref, p_ref, a_ref, tq, tk):
    rows = MLA_HEADS * tq
    q_lat = ql_ref[...].reshape(rows, ql_ref.shape[-1])
    q_rope = qr_ref[...].reshape(rows, qr_ref.shape[-1])
    s_ref[...] = _dot_nt(klat, q_lat) + _dot_nt(kr, q_rope)
    if masked:
        k_chunk = (k_pos0 + lax.broadcasted_iota(jnp.int32, (tk, 1), 0)) // CHUNK
    for c in range(rows // MLA_SOFTMAX_COLS):
        cols = slice(c * MLA_SOFTMAX_COLS, (c + 1) * MLA_SOFTMAX_COLS)
        s = s_ref[:, cols]
        if masked:
            lane = c * MLA_SOFTMAX_COLS + lax.broadcasted_iota(jnp.int32, (1, MLA_SOFTMAX_COLS), 1)
            q_chunk = (q_pos0 + lax.rem(lane, tq)) // CHUNK
            s = jnp.where(k_chunk <= q_chunk, s, NEG_BIG)
        m_prev = m_ref[:, cols]
        m_new = jnp.maximum(m_prev, jnp.max(s, axis=0, keepdims=True))
        alpha = jnp.exp2(m_prev - m_new)
        p = jnp.exp2(s - m_new)
        l_ref[:, cols] = alpha * l_ref[:, cols] + jnp.sum(p, axis=0, keepdims=True)
        m_ref[:, cols] = m_new
        a_ref[:, cols] = alpha
        p_ref[:, cols] = p.astype(BF16)
    acc_ref[...] = a_ref[...] * acc_ref[...] + lax.dot_general(
        klat, p_ref[...], (((0,), (0,)), ((), ())), preferred_element_type=F32)


def _mla_prompt_finalize(wuv_ref, o_ref, l_ref, acc_ref, tq):
    inv = 1.0 / l_ref[...]
    for h in range(MLA_HEADS):
        cols = slice(h * tq, (h + 1) * tq)
        ctx_t = (acc_ref[:, cols] * inv[:, cols]).astype(BF16)
        out = lax.dot_general(ctx_t, wuv_ref[h], (((0,), (0,)), ((), ())),
                              preferred_element_type=F32)
        o_ref[:, h * V_DIM:(h + 1) * V_DIM] = out.astype(BF16)


def _mla_prompt_kernel(qb_ref, kb_ref, qp_ref, kp_ref, fl_ref,
                       ql_ref, qr_ref, kl_ref, kr_ref, wuv_ref, o_ref,
                       m_ref, l_ref, acc_ref, s_ref, p_ref, a_ref, *, tq, tk):
    n = pl.program_id(0)
    scratch = (m_ref, l_ref, acc_ref, s_ref, p_ref, a_ref)

    @pl.when(fl_ref[n] % 2 == 1)
    def _():
        _mla_init(m_ref, l_ref, acc_ref)

    @pl.when(fl_ref[n] < 2)
    def _():
        _mla_prompt_tile(ql_ref, qr_ref, kl_ref[...], kr_ref[...], False, 0, 0, *scratch, tq, tk)

    @pl.when(fl_ref[n] >= 2)
    def _():
        _mla_prompt_tile(ql_ref, qr_ref, kl_ref[...], kr_ref[...], True, qp_ref[n], kp_ref[n],
                         *scratch, tq, tk)
        _mla_prompt_finalize(wuv_ref, o_ref, l_ref, acc_ref, tq)


def _mla_prompt(q_lat, q_rope, k_lat, k_rope, w_uv, batch, seq):
    h, t, c = q_lat.shape
    tq = _pick_tile(seq, MLA_Q_TILE, CHUNK)
    tk = _pick_tile(seq, MLA_K_TILE, CHUNK)
    nq, nk = seq // tq, seq // tk
    qb, kb, qp, kp, fl = [], [], [], [], []
    for b in range(batch):
        for qi in range(nq):
            last = ((qi + 1) * tq - 1) // tk
            for kj in range(last + 1):
                qb.append(b * nq + qi)
                kb.append(b * nk + kj)
                qp.append(qi * tq)
                kp.append(kj * tk)
                fl.append((1 if kj == 0 else 0) + (2 if kj == last else 0))
    tabs = [jnp.asarray(np.asarray(a, np.int32)) for a in (qb, kb, qp, kp, fl)]
    rows = h * tq
    grid_spec = pltpu.PrefetchScalarGridSpec(
        num_scalar_prefetch=5,
        grid=(len(qb),),
        in_specs=[
            pl.BlockSpec((h, tq, c), lambda n, qb, kb, qp, kp, fl: (0, qb[n], 0)),
            pl.BlockSpec((h, tq, ROPE_DIM), lambda n, qb, kb, qp, kp, fl: (0, qb[n], 0)),
            pl.BlockSpec((tk, c), lambda n, qb, kb, qp, kp, fl: (kb[n], 0)),
            pl.BlockSpec((tk, ROPE_DIM), lambda n, qb, kb, qp, kp, fl: (kb[n], 0)),
            pl.BlockSpec((h, c, V_DIM), lambda n, qb, kb, qp, kp, fl: (0, 0, 0)),
        ],
        out_specs=pl.BlockSpec((tq, h * V_DIM), lambda n, qb, kb, qp, kp, fl: (qb[n], 0)),
        scratch_shapes=[pltpu.VMEM((1, rows), F32), pltpu.VMEM((1, rows), F32),
                        pltpu.VMEM((c, rows), F32), pltpu.VMEM((tk, rows), F32),
                        pltpu.VMEM((tk, rows), BF16), pltpu.VMEM((1, rows), F32)],
    )
    return pl.pallas_call(
        functools.partial(_mla_prompt_kernel, tq=tq, tk=tk),
        grid_spec=grid_spec,
        out_shape=jax.ShapeDtypeStruct((batch * seq, h * V_DIM), BF16),
        compiler_params=_cparams(("arbitrary",)),
        name="mla_prompt",
    )(*tabs, q_lat, q_rope, k_lat, k_rope, w_uv)


def _mla_sample_kernel(ql_ref, qr_ref, pl_ref, pr_ref, nl_ref, nr_ref, wuv_ref, o_ref,
                       m_ref, l_ref, acc_ref, a_ref, s_ref, p_ref, sn_ref, pn_ref,
                       *, tq, tk, n_past, n_kt):
    j = pl.program_id(1)

    @pl.when(j == 0)
    def _():
        _mla_init(m_ref, l_ref, acc_ref)

    @pl.when(j < n_kt)
    def _():
        _mla_prompt_tile(ql_ref, qr_ref, pl_ref[...].astype(BF16), pr_ref[...].astype(BF16),
                         False, 0, 0, m_ref, l_ref, acc_ref, s_ref, p_ref, a_ref, tq, tk)

    @pl.when(j == n_kt)
    def _():
        _mla_prompt_tile(ql_ref, qr_ref, nl_ref[...], nr_ref[...], True, n_past, n_past,
                         m_ref, l_ref, acc_ref, sn_ref, pn_ref, a_ref, tq, tq)
        _mla_prompt_finalize(wuv_ref, o_ref, l_ref, acc_ref, tq)


def _mla_sample(q_lat, q_rope, past_lat, past_kr, layer, k_lat, k_rope, w_uv,
                row0, dec_batch, dec_seq):
    h, t, c = q_lat.shape
    n_past = past_lat.shape[2]
    tk = _pick_tile(n_past, MLA_K_TILE, 128)
    n_kt = n_past // tk
    blk0 = row0 // dec_seq
    rows = h * dec_seq
    past_idx = lambda b, j: (layer, b, jnp.minimum(j, n_kt - 1), 0)
    return pl.pallas_call(
        functools.partial(_mla_sample_kernel, tq=dec_seq, tk=tk, n_past=n_past, n_kt=n_kt),
        grid=(dec_batch, n_kt + 1),
        in_specs=[
            pl.BlockSpec((h, dec_seq, c), lambda b, j: (0, blk0 + b, 0)),
            pl.BlockSpec((h, dec_seq, ROPE_DIM), lambda b, j: (0, blk0 + b, 0)),
            pl.BlockSpec((None, None, tk, c), past_idx),
            pl.BlockSpec((None, None, tk, ROPE_DIM), past_idx),
            pl.BlockSpec((dec_seq, c), lambda b, j: (blk0 + b, 0)),
            pl.BlockSpec((dec_seq, ROPE_DIM), lambda b, j: (blk0 + b, 0)),
            pl.BlockSpec((h, c, V_DIM), lambda b, j: (0, 0, 0)),
        ],
        out_specs=pl.BlockSpec((dec_seq, h * V_DIM), lambda b, j: (b, 0)),
        out_shape=jax.ShapeDtypeStruct((dec_batch * dec_seq, h * V_DIM), BF16),
        scratch_shapes=[pltpu.VMEM((1, rows), F32), pltpu.VMEM((1, rows), F32),
                        pltpu.VMEM((c, rows), F32), pltpu.VMEM((1, rows), F32),
                        pltpu.VMEM((tk, rows), F32), pltpu.VMEM((tk, rows), BF16),
                        pltpu.VMEM((dec_seq, rows), F32), pltpu.VMEM((dec_seq, rows), BF16)],
        compiler_params=_cparams(("parallel", "arbitrary")),
        name="mla_sample",
    )(q_lat, q_rope, past_lat, past_kr, k_lat, k_rope, w_uv)


def _scan_matrix(sub):
    j = np.arange(sub)[:, None]
    s = np.arange(sub)[None, :]
    u = (j > s).astype(np.float32)
    return jnp.asarray(np.concatenate([u, u], axis=0), dtype=BF16)


_SbChunk = collections.namedtuple("_SbChunk", "q k v before carry_idx acc_idx")


def _sb_scores(qc, kc, before):
    z = _dot_nt(qc, kc)
    softplus = jnp.log(1.0 + jnp.exp2(-jnp.abs(z))) * LOG2E
    log_beta = jnp.minimum(z, 0.0) - softplus
    log_rest = log_beta - z
    if before is not None:
        log_rest = jnp.where(before, log_rest, 0.0)
    hi = log_rest.astype(BF16)
    lo = (log_rest - hi.astype(F32)).astype(BF16)
    return (log_beta, jnp.concatenate([hi, lo], axis=-1),
            jnp.sum(log_rest, axis=-1, keepdims=True))


def _sb_weights(log_beta, hi_lo, u2, carry, before):
    a = jnp.exp2(log_beta + _dot(hi_lo, u2) + carry)
    if before is not None:
        a = jnp.where(before, a, 0.0)
    return a.astype(BF16)


def _sb_pipeline(chunks, u2, acc_ref, carry_ref):
    scored = None
    weighted = None
    for j in range(len(chunks) + 2):
        new_scored = None
        if j < len(chunks):
            ch = chunks[j]
            new_scored = (ch,) + _sb_scores(ch.q(), ch.k(), ch.before)
        new_weighted = None
        if scored is not None:
            ch, log_beta, hi_lo, row_sum = scored
            carry = carry_ref[ch.carry_idx]
            new_weighted = (ch, _sb_weights(log_beta, hi_lo, u2, carry, ch.before))
            carry_ref[ch.carry_idx] = carry + row_sum
        if weighted is not None:
            ch, a = weighted
            acc_ref[ch.acc_idx] += _dot(a, ch.v())
        scored, weighted = new_scored, new_weighted


def _sb_tile(q_ref, k_ref, v_ref, u2, acc_ref, carry_ref, diagonal, tq, tk, sub, rq):
    chunks = []
    for c in reversed(range(tk // sub)):
        k0, k1 = c * sub, (c + 1) * sub
        for r in range(tq // rq):
            r0, r1 = r * rq, (r + 1) * rq
            before = None
            if diagonal:
                if k0 >= r1 - 1:
                    continue
                if k1 > r0:
                    before = ((k0 + lax.broadcasted_iota(jnp.int32, (1, sub), 1))
                              < (r0 + lax.broadcasted_iota(jnp.int32, (rq, 1), 0)))
            for h in range(SB_HEADS):
                cols = slice(h * SB_DIM, (h + 1) * SB_DIM)
                rows, keys = slice(r0, r1), slice(k0, k1)
                chunks.append(_SbChunk(
                    q=functools.partial(lambda rr, cc: q_ref[rr, cc], rows, cols),
                    k=functools.partial(lambda kk, cc: k_ref[kk, cc], keys, cols),
                    v=functools.partial(lambda kk, cc: v_ref[kk, cc], keys, cols),
                    before=before, carry_idx=(h, rows, slice(None)), acc_idx=(rows, cols)))
    _sb_pipeline(chunks, u2, acc_ref, carry_ref)


def _sb_prompt_kernel(q_ref, k_ref, v_ref, u_ref, k_hbm, v_hbm, o_ref,
                      acc_ref, carry_ref, kbuf, vbuf, sem, *, nq, tq, sub, rq):
    g = pl.program_id(0)
    n_old = lax.rem(g, nq) * (tq // sub)

    def tile_copies(i, slot):
        start = g * tq - (i + 1) * sub
        return (pltpu.make_async_copy(k_hbm.at[pl.ds(start, sub), :], kbuf.at[slot], sem.at[0, slot]),
                pltpu.make_async_copy(v_hbm.at[pl.ds(start, sub), :], vbuf.at[slot], sem.at[1, slot]))

    def start_tile(i, slot):
        for copy in tile_copies(i, slot):
            copy.start()

    def wait_tile(i, slot):
        for copy in tile_copies(i, slot):
            copy.wait()

    def any_alive():
        return (jnp.max(carry_ref[...]) > SB_DEAD_LOG2).astype(jnp.int32)

    @pl.when(n_old > 0)
    def _():
        start_tile(0, 0)

    acc_ref[...] = jnp.zeros(acc_ref.shape, F32)
    carry_ref[...] = jnp.zeros(carry_ref.shape, F32)
    _sb_tile(q_ref, k_ref, v_ref, u_ref[...], acc_ref, carry_ref, True, tq, tq, sub, rq)

    def tile_step(state):
        i, _ = state
        slot = lax.rem(i, 2)
        wait_tile(i, slot)

        @pl.when(i + 1 < n_old)
        def _():
            start_tile(i + 1, 1 - slot)

        for r in range(tq // rq):
            rows = slice(r * rq, (r + 1) * rq)

            @pl.when(jnp.max(carry_ref[:, rows, :]) > SB_DEAD_LOG2)
            def _(rows=rows):
                chunks = []
                for h in range(SB_HEADS):
                    cols = slice(h * SB_DIM, (h + 1) * SB_DIM)
                    chunks.append(_SbChunk(
                        q=functools.partial(lambda rr, cc: q_ref[rr, cc], rows, cols),
                        k=functools.partial(lambda cc: kbuf[slot, :, cc], cols),
                        v=functools.partial(lambda cc: vbuf[slot, :, cc], cols),
                        before=None, carry_idx=(h, rows, slice(None)), acc_idx=(rows, cols)))
                _sb_pipeline(chunks, u_ref[...], acc_ref, carry_ref)
        return i + 1, any_alive()

    tiles_done, _ = lax.while_loop(
        lambda state: jnp.logical_and(state[0] < n_old, state[1] == 1),
        tile_step, (jnp.int32(0), any_alive()))

    @pl.when(tiles_done < n_old)
    def _():
        wait_tile(tiles_done, lax.rem(tiles_done, 2))

    o_ref[...] = acc_ref[...].astype(BF16)


def _sb_prompt(q, k, v, batch, seq):
    width = q.shape[1]
    tq = _pick_tile(seq, SB_TILE, 128)
    sub = _pick_tile(tq, SB_SUB, 128)
    rq = _pick_tile(tq, SB_ROWS, 16)
    nq = seq // tq
    rows = pl.BlockSpec((tq, width), lambda g: (g, 0))
    in_hbm = pl.BlockSpec(memory_space=pl.ANY)
    return pl.pallas_call(
        functools.partial(_sb_prompt_kernel, nq=nq, tq=tq, sub=sub, rq=rq),
        grid=(batch * nq,),
        in_specs=[rows, rows, rows, pl.BlockSpec((2 * sub, sub), lambda g: (0, 0)),
                  in_hbm, in_hbm],
        out_specs=rows,
        out_shape=jax.ShapeDtypeStruct((batch * seq, width), BF16),
        scratch_shapes=[pltpu.VMEM((tq, width), F32), pltpu.VMEM((SB_HEADS, tq, 1), F32),
                        pltpu.VMEM((2, sub, width), BF16), pltpu.VMEM((2, sub, width), BF16),
                        pltpu.SemaphoreType.DMA((2, 2))],
        compiler_params=_cparams(("arbitrary",)),
        name="sb_prompt",
    )(q, k, v, _scan_matrix(sub), k, v)


def _sb_sample_kernel(q_ref, nk_ref, nv_ref, un_ref, up_ref, pk_hbm, pv_hbm, o_ref,
                      acc_ref, carry_ref, kbuf, vbuf, sem, *, layer, tq, tk, n_kt):
    b = pl.program_id(0)

    def tile_copies(i, slot):
        start = (n_kt - 1 - i) * tk
        copies = []
        for h in range(SB_HEADS):
            copies.append(pltpu.make_async_copy(
                pk_hbm.at[layer, b, pl.ds(start, tk), h, :], kbuf.at[slot, h], sem.at[0, slot]))
            copies.append(pltpu.make_async_copy(
                pv_hbm.at[layer, b, pl.ds(start, tk), h, :], vbuf.at[slot, h], sem.at[1, slot]))
        return copies

    def start_tile(i, slot):
        for copy in tile_copies(i, slot):
            copy.start()

    def wait_tile(i, slot):
        for copy in tile_copies(i, slot):
            copy.wait()

    def alive():
        return (jnp.max(carry_ref[...]) > SB_DEAD_LOG2).astype(jnp.int32)

    start_tile(0, 0)

    acc_ref[...] = jnp.zeros(acc_ref.shape, F32)
    carry_ref[...] = jnp.zeros(carry_ref.shape, F32)
    before = (lax.broadcasted_iota(jnp.int32, (1, tq), 1)
              < lax.broadcasted_iota(jnp.int32, (tq, 1), 0))
    chunks = []
    for h in range(SB_HEADS):
        cols = slice(h * SB_DIM, (h + 1) * SB_DIM)
        chunks.append(_SbChunk(
            q=functools.partial(lambda cc: q_ref[:, cc], cols),
            k=functools.partial(lambda cc: nk_ref[:, cc], cols),
            v=functools.partial(lambda cc: nv_ref[:, cc], cols),
            before=before, carry_idx=(h, slice(None), slice(None)), acc_idx=(slice(None), cols)))
    _sb_pipeline(chunks, un_ref[...], acc_ref, carry_ref)

    def tile_step(state):
        i, _ = state
        slot = lax.rem(i, 2)
        wait_tile(i, slot)

        @pl.when(i + 1 < n_kt)
        def _():
            start_tile(i + 1, 1 - slot)

        chunks = []
        for h in range(SB_HEADS):
            cols = slice(h * SB_DIM, (h + 1) * SB_DIM)
            chunks.append(_SbChunk(
                q=functools.partial(lambda cc: q_ref[:, cc], cols),
                k=functools.partial(lambda hh: kbuf[slot, hh].astype(BF16), h),
                v=functools.partial(lambda hh: vbuf[slot, hh].astype(BF16), h),
                before=None, carry_idx=(h, slice(None), slice(None)), acc_idx=(slice(None), cols)))
        _sb_pipeline(chunks, up_ref[...], acc_ref, carry_ref)
        return i + 1, alive()

    tiles_done, _ = lax.while_loop(
        lambda state: jnp.logical_and(state[0] < n_kt, state[1] == 1),
        tile_step, (jnp.int32(0), alive()))

    @pl.when(tiles_done < n_kt)
    def _():
        wait_tile(tiles_done, lax.rem(tiles_done, 2))

    o_ref[...] = acc_ref[...].astype(BF16)


def _sb_sample(q, past_k, past_v, layer, k, v, row0, dec_batch, dec_seq):
    width = q.shape[1]
    n_past = past_k.shape[2]
    tk = _pick_tile(n_past, SB_SUB, 128)
    n_kt = n_past // tk
    blk0 = row0 // dec_seq
    new_rows = pl.BlockSpec((dec_seq, width), lambda b: (blk0 + b, 0))
    in_hbm = pl.BlockSpec(memory_space=pl.ANY)
    return pl.pallas_call(
        functools.partial(_sb_sample_kernel, layer=layer, tq=dec_seq, tk=tk, n_kt=n_kt),
        grid=(dec_batch,),
        in_specs=[
            new_rows, new_rows, new_rows,
            pl.BlockSpec((2 * dec_seq, dec_seq), lambda b: (0, 0)),
            pl.BlockSpec((2 * tk, tk), lambda b: (0, 0)),
            in_hbm, in_hbm,
        ],
        out_specs=pl.BlockSpec((dec_seq, width), lambda b: (b, 0)),
        out_shape=jax.ShapeDtypeStruct((dec_batch * dec_seq, width), BF16),
        scratch_shapes=[pltpu.VMEM((dec_seq, width), F32),
                        pltpu.VMEM((SB_HEADS, dec_seq, 1), F32),
                        pltpu.VMEM((2, SB_HEADS, tk, SB_DIM), F32),
                        pltpu.VMEM((2, SB_HEADS, tk, SB_DIM), F32),
                        pltpu.SemaphoreType.DMA((2, 2))],
        compiler_params=_cparams(("arbitrary",)),
        name="sb_sample",
    )(q, k, v, _scan_matrix(dec_seq), _scan_matrix(tk), past_k, past_v)


def _merge_kernel(oap_ref, oas_ref, obp_ref, obs_ref, wa_ref, wb_ref, g_ref, o_ref,
                  *, n_p, tn):
    i = pl.program_id(0)
    o_a = _load_split(i, n_p, oap_ref, oas_ref)
    o_b = _load_split(i, n_p, obp_ref, obs_ref)
    n = o_ref.shape[1]
    for c in range(n // tn):
        cols = slice(c * tn, (c + 1) * tn)
        a = _dot(o_a, wa_ref[:, cols])
        b = _dot(o_b, wb_ref[:, cols])
        g_a = g_ref[:, cols].astype(F32)
        g_b = g_ref[:, n + c * tn:n + (c + 1) * tn].astype(F32)
        o_ref[:, cols] = (g_a * a + g_b * b).astype(BF16)


def _merge(oa_p, oa_s, ob_p, ob_s, w_a, w_b, gates, tm):
    da = oa_p.shape[1]
    db = ob_p.shape[1]
    t = oa_p.shape[0] + oa_s.shape[0]
    n_p = oa_p.shape[0] // tm
    n = w_a.shape[1]
    tn = _pick_tile(n, COL_TILE, 128)
    return pl.pallas_call(
        functools.partial(_merge_kernel, n_p=n_p, tn=tn),
        grid=(t // tm,),
        in_specs=[*_split_specs(tm, da, n_p),
                  *_split_specs(tm, db, n_p),
                  _resident((da, n)), _resident((db, n)),
                  pl.BlockSpec((tm, 2 * n), lambda i: (i, 0))],
        out_specs=pl.BlockSpec((tm, n), lambda i: (i, 0)),
        out_shape=jax.ShapeDtypeStruct((t, n), BF16),
        compiler_params=_cparams(("parallel",)),
        name="merge",
    )(oa_p, oa_s, ob_p, ob_s, w_a, w_b, gates)


def _res_ln_finalize(z_ref, g_ref, b_ref, o_ref, ob_ref):
    n_tiles, _, tn = z_ref.shape
    width = n_tiles * tn
    total = z_ref[0].sum(axis=-1, keepdims=True)
    for c in range(1, n_tiles):
        total += z_ref[c].sum(axis=-1, keepdims=True)
    mu = total / width
    sq = jnp.square(z_ref[0] - mu).sum(axis=-1, keepdims=True)
    for c in range(1, n_tiles):
        sq += jnp.square(z_ref[c] - mu).sum(axis=-1, keepdims=True)
    rstd = lax.rsqrt(sq / width + LN_EPS)
    for c in range(n_tiles):
        cols = slice(c * tn, (c + 1) * tn)
        y = (z_ref[c] - mu) * rstd * g_ref[:, cols] + b_ref[:, cols]
        o_ref[:, cols] = y
        if ob_ref is not None:
            ob_ref[:, cols] = y.astype(BF16)


def _out_ln_kernel(m_ref, w_ref, x_ref, g_ref, b_ref, o_ref, ob_ref, z_ref, *, alpha):
    m = m_ref[...]
    n_tiles, _, tn = z_ref.shape
    for c in range(n_tiles):
        cols = slice(c * tn, (c + 1) * tn)
        z_ref[c] = alpha * x_ref[:, cols] + _dot(m, w_ref[:, cols])
    _res_ln_finalize(z_ref, g_ref, b_ref, o_ref, ob_ref)


def _out_ln(merged, w, x, g, b, alpha, tm):
    t, d = x.shape
    k = merged.shape[1]
    tn = _pick_tile(d, COL_TILE, 128)
    full_row = pl.BlockSpec((tm, d), lambda i: (i, 0))
    vec = pl.BlockSpec((1, d), lambda i: (0, 0))
    return pl.pallas_call(
        functools.partial(_out_ln_kernel, alpha=alpha),
        grid=(t // tm,),
        in_specs=[pl.BlockSpec((tm, k), lambda i: (i, 0)), _resident((k, d)), full_row, vec, vec],
        out_specs=[full_row, full_row],
        out_shape=[jax.ShapeDtypeStruct((t, d), F32), jax.ShapeDtypeStruct((t, d), BF16)],
        scratch_shapes=[pltpu.VMEM((d // tn, tm, tn), F32)],
        compiler_params=_cparams(("parallel",)),
        name="out_ln",
    )(merged, w, x, g.reshape(1, d), b.reshape(1, d))


def _ffn_ln_kernel(xb_ref, wg_ref, wu_ref, wd_ref, x_ref, g_ref, b_ref, o_ref, ob_ref,
                   acc_ref, *, alpha):
    f = pl.program_id(1)

    @pl.when(f == 0)
    def _():
        acc_ref[...] = jnp.zeros(acc_ref.shape, F32)

    xb = xb_ref[...]
    gate = _dot(xb, wg_ref[...])
    up = _dot(xb, wu_ref[...])
    hidden = (gate * _sigmoid(gate) * up).astype(BF16)
    acc_ref[...] += _dot(hidden, wd_ref[...])

    @pl.when(f == pl.num_programs(1) - 1)
    def _():
        y = _layer_norm_rows(alpha * x_ref[...] + acc_ref[...], g_ref[...], b_ref[...])
        o_ref[...] = y
        ob_ref[...] = y.astype(BF16)


def _ffn_ln(xb, w_gu, w_down, x, g, b, alpha, tm):
    t, d = x.shape
    ff = w_down.shape[0]
    tf = _pick_tile(ff, COL_TILE, 128)
    nf = ff // tf
    full_row = pl.BlockSpec((tm, d), lambda i, f: (i, 0))
    vec = pl.BlockSpec((1, d), lambda i, f: (0, 0))
    return pl.pallas_call(
        functools.partial(_ffn_ln_kernel, alpha=alpha),
        grid=(t // tm, nf),
        in_specs=[full_row,
                  pl.BlockSpec((d, tf), lambda i, f: (0, f)),
                  pl.BlockSpec((d, tf), lambda i, f: (0, f + nf)),
                  pl.BlockSpec((tf, d), lambda i, f: (f, 0)),
                  full_row, vec, vec],
        out_specs=[full_row, full_row],
        out_shape=[jax.ShapeDtypeStruct((t, d), F32), jax.ShapeDtypeStruct((t, d), BF16)],
        scratch_shapes=[pltpu.VMEM((tm, d), F32)],
        compiler_params=_cparams(("parallel", "arbitrary")),
        name="ffn_ln",
    )(xb, w_gu, w_gu, w_down, x, g.reshape(1, d), b.reshape(1, d))


def _ple_ln_kernel(xb_ref, wg_ref, bg_ref, pp_ref, ps_ref, wp_ref, x_ref, g_ref, b_ref,
                   o1_ref, o2_ref, z_ref, *, alpha, n_p, last):
    i = pl.program_id(0)
    xb = xb_ref[...]
    p = _load_split(i, n_p, pp_ref, ps_ref).astype(BF16)
    n_tiles, _, tn = z_ref.shape
    for c in range(n_tiles):
        cols = slice(c * tn, (c + 1) * tn)
        gate = _sigmoid(_dot(xb, wg_ref[:, cols]) + bg_ref[:, cols])
        z_ref[c] = alpha * x_ref[:, cols] + gate * _dot(p, wp_ref[:, cols])

    if not last:
        _res_ln_finalize(z_ref, g_ref, b_ref, o1_ref, o2_ref)
    else:
        @pl.when(i < n_p)
        def _():
            _res_ln_finalize(z_ref, g_ref, b_ref, o1_ref, None)

        @pl.when(i >= n_p)
        def _():
            _res_ln_finalize(z_ref, g_ref, b_ref, o2_ref, None)


def _ple_ln(xb, w_gate, b_gate, p_p, p_s, layer, w_proj, x, g, b, alpha, tm, last):
    t, d = x.shape
    pd = p_p.shape[2]
    tp = p_p.shape[1]
    n_p = tp // tm
    tn = _pick_tile(d, COL_TILE, 128)
    full_row = pl.BlockSpec((tm, d), lambda i: (i, 0))
    vec = pl.BlockSpec((1, d), lambda i: (0, 0))
    if last:
        out_specs = list(_split_specs(tm, d, n_p))
        out_shape = [jax.ShapeDtypeStruct((tp, d), F32), jax.ShapeDtypeStruct((t - tp, d), F32)]
    else:
        out_specs = [full_row, full_row]
        out_shape = [jax.ShapeDtypeStruct((t, d), F32), jax.ShapeDtypeStruct((t, d), BF16)]
    return pl.pallas_call(
        functools.partial(_ple_ln_kernel, alpha=alpha, n_p=n_p, last=last),
        grid=(t // tm,),
        in_specs=[full_row, _resident((d, d)), vec,
                  pl.BlockSpec((None, tm, pd), lambda i: (layer, jnp.minimum(i, n_p - 1), 0)),
                  pl.BlockSpec((None, tm, pd), lambda i: (layer, jnp.maximum(i - n_p, 0), 0)),
                  _resident((pd, d)), full_row, vec, vec],
        out_specs=out_specs,
        out_shape=out_shape,
        scratch_shapes=[pltpu.VMEM((d // tn, tm, tn), F32)],
        compiler_params=_cparams(("arbitrary",)),
        name="ple_ln",
    )(xb, w_gate, b_gate.reshape(1, d), p_p, p_s, w_proj, x, g.reshape(1, d), b.reshape(1, d))


def _rope_tables(pos):
    half = ROPE_DIM // 2
    inv_freq = 1.0 / (ROPE_THETA ** (jnp.arange(half, dtype=F32) * (2.0 / ROPE_DIM)))
    ang = pos.astype(F32)[:, None] * inv_freq[None, :]
    return jnp.cos(ang), jnp.sin(ang)


def kernel(x_prompt, x_sample, cache_mla_latent, cache_mla_krope, cache_sb_k, cache_sb_v,
           p_prompt, p_sample, ln_in_g, ln_in_b, w_in, b_gate, q_a_norm_g, w_q_b,
           kv_a_norm_g, w_kv_b, w_branch_a, w_branch_b, w_out, ln1_g, ln1_b,
           w_ffn_gu, w_ffn_down, ln2_g, ln2_b, w_ple_gate, b_ple_gate, w_ple_proj,
           ln3_g, ln3_b):
    batch, seq, d = x_prompt.shape
    dec_batch, dec_seq, _ = x_sample.shape
    depth = w_in.shape[0]
    n_past = cache_mla_latent.shape[2]
    tp = batch * seq
    ts = dec_batch * dec_seq
    sb_width = SB_HEADS * SB_DIM
    alpha = (2 * depth) ** 0.25

    pos = jnp.concatenate([jnp.tile(jnp.arange(seq, dtype=jnp.int32), batch),
                           jnp.tile(n_past + jnp.arange(dec_seq, dtype=jnp.int32), dec_batch)])
    cos, sin = _rope_tables(pos)

    tm = _pick_tile(math.gcd(tp, ts), ROW_TILE, 16)
    p_p = p_prompt.reshape(depth, tp, -1)
    p_s = p_sample.reshape(depth, ts, -1)

    x, xb = _ln_in(x_prompt.reshape(tp, d), x_sample.reshape(ts, d), ln_in_g, ln_in_b, tm)

    splits = np.cumsum([0, Q_LORA, KV_LORA, ROPE_DIM, sb_width, sb_width, sb_width, 2 * d])
    def cache_arrays(widths):
        return [jnp.zeros((depth, n, w), F32) for w in widths for n in (tp, ts)]

    kv_cache = cache_arrays((KV_LORA, ROPE_DIM))
    sb_cache = cache_arrays((sb_width, sb_width))
    for l in range(depth):
        w_cols = [w_in[l][:, splits[i]:splits[i + 1]].astype(BF16) for i in range(7)]
        w_qa, w_lat, w_kr, w_sq, w_sk, w_sv, w_gates = w_cols
        wq = w_q_b[l].reshape(Q_LORA, MLA_HEADS, NOPE_DIM + ROPE_DIM)
        wq_nope = wq[..., :NOPE_DIM].reshape(Q_LORA, MLA_HEADS * NOPE_DIM).astype(BF16)
        wq_rope = wq[..., NOPE_DIM:].reshape(Q_LORA, MLA_HEADS * ROPE_DIM).astype(BF16)
        w_kv = w_kv_b[l].reshape(KV_LORA, MLA_HEADS, NOPE_DIM + V_DIM)
        wuk_h = w_kv[..., :NOPE_DIM].transpose(1, 2, 0).astype(BF16)
        wuv_h = w_kv[..., NOPE_DIM:].transpose(1, 0, 2).astype(BF16)

        qa_n = _proj_qa(xb, w_qa, q_a_norm_g[l], tm)
        *kv_cache, lat_b, kr_b = _proj_kv(xb, w_lat, w_kr, kv_a_norm_g[l], cos, sin,
                                          tm, tp, l, depth, kv_cache)
        *sb_cache, sbq, sbk_b, sbv_b = _proj_sb(xb, w_sq, w_sk, w_sv, tm, tp, l, depth, sb_cache)
        gates = _proj_gates(xb, w_gates, b_gate[l], tm)

        q_lat, q_rope = _q_heads(qa_n, wq_nope, wq_rope, wuk_h, cos, sin, tm)
        oa_p = _mla_prompt(q_lat, q_rope, lat_b, kr_b, wuv_h, batch, seq)
        oa_s = _mla_sample(q_lat, q_rope, cache_mla_latent, cache_mla_krope, l, lat_b, kr_b,
                           wuv_h, tp, dec_batch, dec_seq)
        ob_p = _sb_prompt(sbq, sbk_b, sbv_b, batch, seq)
        ob_s = _sb_sample(sbq, cache_sb_k, cache_sb_v, l, sbk_b, sbv_b, tp, dec_batch, dec_seq)

        merged = _merge(oa_p, oa_s, ob_p, ob_s, w_branch_a[l].astype(BF16),
                        w_branch_b[l].astype(BF16), gates, tm)
        x, xb = _out_ln(merged, w_out[l].astype(BF16), x, ln1_g[l], ln1_b[l], alpha, tm)
        x, xb = _ffn_ln(xb, w_ffn_gu[l].astype(BF16), w_ffn_down[l].astype(BF16), x,
                        ln2_g[l], ln2_b[l], alpha, tm)
        x, xb = _ple_ln(xb, w_ple_gate[l].astype(BF16), b_ple_gate[l], p_p, p_s, l,
                        w_ple_proj[l].astype(BF16), x, ln3_g[l], ln3_b[l], alpha, tm,
                        last=(l == depth - 1))

    lat_p, lat_s, kr_p, kr_s = kv_cache
    k_p, k_s, v_p, v_s = sb_cache
    head_dims = (SB_HEADS, SB_DIM)
    return (x.reshape(batch, seq, d), xb.reshape(dec_batch, dec_seq, d),
            lat_p.reshape(depth, batch, seq, KV_LORA), kr_p.reshape(depth, batch, seq, ROPE_DIM),
            k_p.reshape(depth, batch, seq, *head_dims), v_p.reshape(depth, batch, seq, *head_dims),
            lat_s.reshape(depth, dec_batch, dec_seq, KV_LORA),
            kr_s.reshape(depth, dec_batch, dec_seq, ROPE_DIM),
            k_s.reshape(depth, dec_batch, dec_seq, *head_dims),
            v_s.reshape(depth, dec_batch, dec_seq, *head_dims))
```

```python
import collections
import functools
import math

import numpy as np
import jax
import jax.numpy as jnp
from jax import lax
from jax.experimental import pallas as pl
from jax.experimental.pallas import tpu as pltpu

CHUNK = 64
MLA_HEADS = 16
Q_LORA = 512
KV_LORA = 512
NOPE_DIM = 128
ROPE_DIM = 64
V_DIM = 128
ROPE_THETA = 10000.0
MLA_SCALE = (NOPE_DIM + ROPE_DIM) ** -0.5
SB_HEADS = 8
SB_DIM = 128
SB_SCALE = SB_DIM ** -0.5
LN_EPS = 1e-5
RMS_EPS = 1e-6
LOG2E = 1.4426950408889634
MLA_QSCALE = MLA_SCALE * LOG2E
SB_QSCALE = SB_SCALE * LOG2E
SB_DEAD_LOG2 = -160.0

BF16 = jnp.bfloat16
F32 = jnp.float32

V7X_VMEM_LIMIT_BYTES = 56 * 1024 * 1024
ROW_TILE = 512
COL_TILE = 512
PROJ_COL_TILE = 1024
MLA_Q_TILE = 256
MLA_K_TILE = 512
MLA_SOFTMAX_COLS = 128
SB_TILE = 512
SB_SUB = 256
SB_ROWS = 128
NEG_BIG = -1e30


def _pick_tile(n, target, mult):
    best = None
    for t in range(mult, min(n, target) + 1, mult):
        if n % t == 0:
            best = t
    if best is None:
        return n
    return best


def _cparams(sem):
    return pltpu.CompilerParams(dimension_semantics=sem,
                                vmem_limit_bytes=V7X_VMEM_LIMIT_BYTES)


def _resident(shape):
    return pl.BlockSpec(shape, lambda *_: (0,) * len(shape), pipeline_mode=pl.Buffered(1))


def _dot(a, b):
    return jnp.dot(a, b, preferred_element_type=F32)


def _dot_nt(a, b):
    return lax.dot_general(a, b, (((1,), (1,)), ((), ())), preferred_element_type=F32)


def _layer_norm_rows(z, g, b):
    mu = jnp.mean(z, axis=-1, keepdims=True)
    zc = z - mu
    var = jnp.mean(zc * zc, axis=-1, keepdims=True)
    return zc * lax.rsqrt(var + LN_EPS) * g + b


def _rms_norm_rows(z, g):
    return z * lax.rsqrt(jnp.mean(z * z, axis=-1, keepdims=True) + RMS_EPS) * g


def _sigmoid(z):
    return 1.0 / (1.0 + jnp.exp(-z))


def _rope_rows(x, cos, sin):
    half = ROPE_DIM // 2
    x1 = x[:, :half]
    x2 = x[:, half:]
    return jnp.concatenate([x1 * cos - x2 * sin, x2 * cos + x1 * sin], axis=-1)


def _split_specs(tm, width, n_p):
    return (pl.BlockSpec((tm, width), lambda i, *_: (jnp.minimum(i, n_p - 1), 0)),
            pl.BlockSpec((tm, width), lambda i, *_: (jnp.maximum(i - n_p, 0), 0)))


def _load_split(i, n_p, p_ref, s_ref):
    return jnp.where(i < n_p, p_ref[...], s_ref[...])


def _store_split(i, n_p, p_ref, s_ref, value):
    @pl.when(i < n_p)
    def _():
        p_ref[...] = value

    @pl.when(i >= n_p)
    def _():
        s_ref[...] = value


def _ln_in_kernel(xp_ref, xs_ref, g_ref, b_ref, o_ref, ob_ref, *, n_p):
    x = _load_split(pl.program_id(0), n_p, xp_ref, xs_ref)
    y = _layer_norm_rows(x, g_ref[...], b_ref[...])
    o_ref[...] = y
    ob_ref[...] = y.astype(BF16)


def _ln_in(x_p, x_s, g, b, tm):
    d = x_p.shape[1]
    t = x_p.shape[0] + x_s.shape[0]
    n_p = x_p.shape[0] // tm
    row = pl.BlockSpec((tm, d), lambda i: (i, 0))
    vec = pl.BlockSpec((1, d), lambda i: (0, 0))
    return pl.pallas_call(
        functools.partial(_ln_in_kernel, n_p=n_p),
        grid=(t // tm,),
        in_specs=[*_split_specs(tm, d, n_p), vec, vec],
        out_specs=[row, row],
        out_shape=[jax.ShapeDtypeStruct((t, d), F32), jax.ShapeDtypeStruct((t, d), BF16)],
        compiler_params=_cparams(("parallel",)),
        name="ln_in",
    )(x_p, x_s, g.reshape(1, d), b.reshape(1, d))


def _proj_qa_kernel(x_ref, w_ref, g_ref, o_ref):
    o_ref[...] = _rms_norm_rows(_dot(x_ref[...], w_ref[...]), g_ref[...]).astype(BF16)


def _proj_kv_kernel(x_ref, wl_ref, wr_ref, g_ref, cos_ref, sin_ref, *refs, n_p):
    latp_ref, lats_ref, krp_ref, krs_ref, latb_ref, krb_ref = refs[-6:]
    i = pl.program_id(0)
    x = x_ref[...]
    lat = _rms_norm_rows(_dot(x, wl_ref[...]), g_ref[...])
    kr = _rope_rows(_dot(x, wr_ref[...]), cos_ref[...], sin_ref[...])
    _store_split(i, n_p, latp_ref, lats_ref, lat)
    _store_split(i, n_p, krp_ref, krs_ref, kr)
    latb_ref[...] = lat.astype(BF16)
    krb_ref[...] = kr.astype(BF16)


def _proj_sb_kernel(x_ref, wq_ref, wk_ref, wv_ref, *refs, n_p):
    kp_ref, ks_ref, vp_ref, vs_ref, q_ref, kb_ref, vb_ref = refs[-7:]
    i = pl.program_id(0)
    x = x_ref[...]
    q_ref[...] = (_dot(x, wq_ref[...]) * SB_QSCALE).astype(BF16)
    k = _dot(x, wk_ref[...])
    _store_split(i, n_p, kp_ref, ks_ref, k)
    kb_ref[...] = k.astype(BF16)
    v = _dot(x, wv_ref[...])
    _store_split(i, n_p, vp_ref, vs_ref, v)
    vb_ref[...] = v.astype(BF16)


def _proj_gate_kernel(x_ref, w_ref, b_ref, o_ref, *, tn):
    x = x_ref[...]
    for c in range(o_ref.shape[1] // tn):
        cols = slice(c * tn, (c + 1) * tn)
        o_ref[:, cols] = _sigmoid(_dot(x, w_ref[:, cols]) + b_ref[:, cols]).astype(BF16)


def _proj_qa(xb, w, g, tm):
    t, d = xb.shape
    n = w.shape[1]
    return pl.pallas_call(
        _proj_qa_kernel,
        grid=(t // tm,),
        in_specs=[pl.BlockSpec((tm, d), lambda i: (i, 0)),
                  pl.BlockSpec((d, n), lambda i: (0, 0)),
                  pl.BlockSpec((1, n), lambda i: (0, 0))],
        out_specs=pl.BlockSpec((tm, n), lambda i: (i, 0)),
        out_shape=jax.ShapeDtypeStruct((t, n), BF16),
        compiler_params=_cparams(("parallel",)),
        name="proj_qa",
    )(xb, w, g.reshape(1, n))


def _cache_outputs(widths, tm, tp, ts, layer, depth, prev, n_inputs):
    n_p = tp // tm
    out_specs, out_shape = [], []
    for w in widths:
        out_specs += [
            pl.BlockSpec((None, tm, w), lambda i: (layer, jnp.minimum(i, n_p - 1), 0)),
            pl.BlockSpec((None, tm, w), lambda i: (layer, jnp.maximum(i - n_p, 0), 0))]
        out_shape += [jax.ShapeDtypeStruct((depth, tp, w), F32),
                      jax.ShapeDtypeStruct((depth, ts, w), F32)]
    extra = list(prev)
    extra_specs = [pl.BlockSpec(memory_space=pl.ANY) for _ in extra]
    aliases = {n_inputs + k: k for k in range(len(extra))}
    return extra, extra_specs, out_specs, out_shape, aliases


def _proj_kv(xb, w_lat, w_kr, g, cos, sin, tm, tp, layer, depth, prev):
    t, d = xb.shape
    c = w_lat.shape[1]
    r = w_kr.shape[1]
    rows = lambda w: pl.BlockSpec((tm, w), lambda i: (i, 0))
    full = lambda a, b: pl.BlockSpec((a, b), lambda i: (0, 0))
    extra, extra_specs, out_specs, out_shape, aliases = _cache_outputs(
        (c, r), tm, tp, t - tp, layer, depth, prev, 6)
    return pl.pallas_call(
        functools.partial(_proj_kv_kernel, n_p=tp // tm),
        grid=(t // tm,),
        in_specs=[rows(d), full(d, c), full(d, r), full(1, c), rows(r // 2), rows(r // 2)]
                 + extra_specs,
        out_specs=out_specs + [rows(c), rows(r)],
        out_shape=out_shape + [jax.ShapeDtypeStruct((t, c), BF16),
                               jax.ShapeDtypeStruct((t, r), BF16)],
        input_output_aliases=aliases,
        compiler_params=_cparams(("arbitrary",)),
        name="proj_kv",
    )(xb, w_lat, w_kr, g.reshape(1, c), cos, sin, *extra)


def _proj_sb(xb, w_q, w_k, w_v, tm, tp, layer, depth, prev):
    t, d = xb.shape
    n = w_q.shape[1]
    rows = lambda w: pl.BlockSpec((tm, w), lambda i: (i, 0))
    weight = _resident((d, n))
    bf16 = jax.ShapeDtypeStruct((t, n), BF16)
    extra, extra_specs, out_specs, out_shape, aliases = _cache_outputs(
        (n, n), tm, tp, t - tp, layer, depth, prev, 4)
    return pl.pallas_call(
        functools.partial(_proj_sb_kernel, n_p=tp // tm),
        grid=(t // tm,),
        in_specs=[rows(d), weight, weight, weight] + extra_specs,
        out_specs=out_specs + [rows(n), rows(n), rows(n)],
        out_shape=out_shape + [bf16, bf16, bf16],
        input_output_aliases=aliases,
        compiler_params=_cparams(("arbitrary",)),
        name="proj_sb",
    )(xb, w_q, w_k, w_v, *extra)


def _proj_gates(xb, w, bias, tm):
    t, d = xb.shape
    n = w.shape[1]
    tn = _pick_tile(n, PROJ_COL_TILE, 128)
    return pl.pallas_call(
        functools.partial(_proj_gate_kernel, tn=tn),
        grid=(t // tm,),
        in_specs=[pl.BlockSpec((tm, d), lambda i: (i, 0)), _resident((d, n)),
                  pl.BlockSpec((1, n), lambda i: (0, 0))],
        out_specs=pl.BlockSpec((tm, n), lambda i: (i, 0)),
        out_shape=jax.ShapeDtypeStruct((t, n), BF16),
        compiler_params=_cparams(("parallel",)),
        name="proj_gates",
    )(xb, w, bias.reshape(1, n))


def _q_heads_kernel(qa_ref, wn_ref, wr_ref, wuk_ref, cos_ref, sin_ref, qn_ref, ql_ref, qr_ref,
                    *, n_p):
    i = pl.program_id(0)
    qa = qa_ref[...]
    q_nope = _dot(qa, wn_ref[...])
    q_rope = _dot(qa, wr_ref[...])
    cos = cos_ref[...]
    sin = sin_ref[...]
    for h in range(MLA_HEADS):
        qr = _rope_rows(q_rope[:, h * ROPE_DIM:(h + 1) * ROPE_DIM], cos, sin)
        qr_ref[h] = (qr * MLA_QSCALE).astype(BF16)

    @pl.when(i < n_p)
    def _():
        for h in range(MLA_HEADS):
            qn_ref[h] = (q_nope[:, h * NOPE_DIM:(h + 1) * NOPE_DIM] * MLA_QSCALE).astype(BF16)

    @pl.when(i >= n_p)
    def _():
        for h in range(MLA_HEADS):
            nope = q_nope[:, h * NOPE_DIM:(h + 1) * NOPE_DIM].astype(BF16)
            ql_ref[h] = (_dot(nope, wuk_ref[h]) * MLA_QSCALE).astype(BF16)


def _q_heads(qa_n, w_nope, w_rope, wuk_h, cos, sin, tm, tp):
    t, ql = qa_n.shape
    h, _, c = wuk_h.shape
    half = ROPE_DIM // 2
    n_p = tp // tm
    return pl.pallas_call(
        functools.partial(_q_heads_kernel, n_p=n_p),
        grid=(t // tm,),
        in_specs=[pl.BlockSpec((tm, ql), lambda i: (i, 0)),
                  pl.BlockSpec((ql, h * NOPE_DIM), lambda i: (0, 0)),
                  pl.BlockSpec((ql, h * ROPE_DIM), lambda i: (0, 0)),
                  pl.BlockSpec((h, NOPE_DIM, c), lambda i: (0, 0, 0)),
                  pl.BlockSpec((tm, half), lambda i: (i, 0)),
                  pl.BlockSpec((tm, half), lambda i: (i, 0))],
        out_specs=[pl.BlockSpec((h, tm, NOPE_DIM), lambda i: (0, jnp.minimum(i, n_p - 1), 0)),
                   pl.BlockSpec((h, tm, c), lambda i: (0, jnp.maximum(i - n_p, 0), 0)),
                   pl.BlockSpec((h, tm, ROPE_DIM), lambda i: (0, i, 0))],
        out_shape=[jax.ShapeDtypeStruct((h, tp, NOPE_DIM), BF16),
                   jax.ShapeDtypeStruct((h, t - tp, c), BF16),
                   jax.ShapeDtypeStruct((h, t, ROPE_DIM), BF16)],
        compiler_params=_cparams(("arbitrary",)),
        name="q_heads",
    )(qa_n, w_nope, w_rope, wuk_h, cos, sin)


def _kv_heads_kernel(lat_ref, wk_ref, wv_ref, k_ref, v_ref):
    lat = lat_ref[...]
    k_ref[...] = _dot(lat, wk_ref[...]).astype(BF16)
    v_ref[...] = _dot(lat, wv_ref[...]).astype(BF16)


def _kv_heads(lat_b, w_uk, w_uv, tm, tp):
    c = lat_b.shape[1]
    n = w_uk.shape[1]
    rows = lambda w: pl.BlockSpec((tm, w), lambda i: (i, 0))
    return pl.pallas_call(
        _kv_heads_kernel,
        grid=(tp // tm,),
        in_specs=[rows(c), _resident((c, n)), _resident((c, n))],
        out_specs=[rows(n), rows(n)],
        out_shape=[jax.ShapeDtypeStruct((tp, n), BF16), jax.ShapeDtypeStruct((tp, n), BF16)],
        compiler_params=_cparams(("parallel",)),
        name="kv_heads",
    )(lat_b, w_uk, w_uv)


def _mla_init(m_ref, l_ref, acc_ref):
    m_ref[...] = jnp.full(m_ref.shape, NEG_BIG, F32)
    l_ref[...] = jnp.zeros(l_ref.shape, F32)
    acc_ref[...] = jnp.zeros(acc_ref.shape, F32)


def _mla_softmax(masked, q_pos0, k_pos0, m_ref, l_ref, s_ref, p_ref, a_ref, tq, tk):
    rows = MLA_HEADS * tq
    if masked:
        k_chunk = (k_pos0 + lax.broadcasted_iota(jnp.int32, (tk, 1), 0)) // CHUNK
    for c in range(rows // MLA_SOFTMAX_COLS):
        cols = slice(c * MLA_SOFTMAX_COLS, (c + 1) * MLA_SOFTMAX_COLS)
        s = s_ref[:, cols]
        if masked:
            lane = c * MLA_SOFTMAX_COLS + lax.broadcasted_iota(jnp.int32, (1, MLA_SOFTMAX_COLS), 1)
            q_chunk = (q_pos0 + lax.rem(lane, tq)) // CHUNK
            s = jnp.where(k_chunk <= q_chunk, s, NEG_BIG)
        m_prev = m_ref[:, cols]
        m_new = jnp.maximum(m_prev, jnp.max(s, axis=0, keepdims=True))
        alpha = jnp.exp2(m_prev - m_new)
        p = jnp.exp2(s - m_new)
        l_ref[:, cols] = alpha * l_ref[:, cols] + jnp.sum(p, axis=0, keepdims=True)
        m_ref[:, cols] = m_new
        a_ref[:, cols] = alpha
        p_ref[:, cols] = p.astype(BF16)


def _dot_tn(a, b):
    return lax.dot_general(a, b, (((0,), (0,)), ((), ())), preferred_element_type=F32)


def _mla_latent_tile(ql_ref, qr_ref, klat, kr, masked, q_pos0, k_pos0,
                     m_ref, l_ref, acc_ref, s_ref, p_ref, a_ref, tq, tk):
    rows = MLA_HEADS * tq
    q_lat = ql_ref[...].reshape(rows, ql_ref.shape[-1])
    q_rope = qr_ref[...].reshape(rows, qr_ref.shape[-1])
    s_ref[...] = _dot_nt(klat, q_lat) + _dot_nt(kr, q_rope)
    _mla_softmax(masked, q_pos0, k_pos0, m_ref, l_ref, s_ref, p_ref, a_ref, tq, tk)
    acc_ref[...] = a_ref[...] * acc_ref[...] + _dot_tn(klat, p_ref[...])


def _mla_latent_finalize(wuv_ref, o_ref, l_ref, acc_ref, tq):
    inv = 1.0 / l_ref[...]
    for h in range(MLA_HEADS):
        cols = slice(h * tq, (h + 1) * tq)
        ctx_t = (acc_ref[:, cols] * inv[:, cols]).astype(BF16)
        o_ref[:, h * V_DIM:(h + 1) * V_DIM] = _dot_tn(ctx_t, wuv_ref[h]).astype(BF16)


def _mla_heads_tile(qn_ref, qr_ref, kn_ref, kr, v_ref, masked, q_pos0, k_pos0,
                    m_ref, l_ref, acc_ref, s_ref, p_ref, a_ref, tq, tk):
    for h in range(MLA_HEADS):
        cols = slice(h * tq, (h + 1) * tq)
        s_ref[:, cols] = (_dot_nt(kn_ref[:, h * NOPE_DIM:(h + 1) * NOPE_DIM], qn_ref[h])
                          + _dot_nt(kr, qr_ref[h]))
    _mla_softmax(masked, q_pos0, k_pos0, m_ref, l_ref, s_ref, p_ref, a_ref, tq, tk)
    for h in range(MLA_HEADS):
        cols = slice(h * tq, (h + 1) * tq)
        pv = _dot_tn(v_ref[:, h * V_DIM:(h + 1) * V_DIM], p_ref[:, cols])
        acc_ref[h] = a_ref[:, cols] * acc_ref[h] + pv


def _mla_heads_finalize(o_ref, l_ref, acc_ref, tq):
    inv = 1.0 / l_ref[...]
    for h in range(MLA_HEADS):
        cols = slice(h * tq, (h + 1) * tq)
        o_ref[:, h * V_DIM:(h + 1) * V_DIM] = (acc_ref[h] * inv[:, cols]).T.astype(BF16)


def _mla_prompt_kernel(qb_ref, kb_ref, qp_ref, kp_ref, fl_ref,
                       qn_ref, qr_ref, kn_ref, kr_ref, v_ref, o_ref,
                       m_ref, l_ref, acc_ref, s_ref, p_ref, a_ref, *, tq, tk):
    n = pl.program_id(0)
    scratch = (m_ref, l_ref, acc_ref, s_ref, p_ref, a_ref)

    @pl.when(fl_ref[n] % 2 == 1)
    def _():
        _mla_init(m_ref, l_ref, acc_ref)

    @pl.when(fl_ref[n] < 2)
    def _():
        _mla_heads_tile(qn_ref, qr_ref, kn_ref, kr_ref[...], v_ref, False, 0, 0, *scratch, tq, tk)

    @pl.when(fl_ref[n] >= 2)
    def _():
        _mla_heads_tile(qn_ref, qr_ref, kn_ref, kr_ref[...], v_ref, True, qp_ref[n], kp_ref[n],
                        *scratch, tq, tk)
        _mla_heads_finalize(o_ref, l_ref, acc_ref, tq)


def _mla_prompt(q_nope, q_rope, k_nope, k_rope, v, batch, seq):
    h = q_nope.shape[0]
    width = k_nope.shape[1]
    tq = _pick_tile(seq, MLA_Q_TILE, CHUNK)
    tk = _pick_tile(seq, MLA_K_TILE, CHUNK)
    nq, nk = seq // tq, seq // tk
    qb, kb, qp, kp, fl = [], [], [], [], []
    for b in range(batch):
        for qi in range(nq):
            last = ((qi + 1) * tq - 1) // tk
            for kj in range(last + 1):
                qb.append(b * nq + qi)
                kb.append(b * nk + kj)
                qp.append(qi * tq)
                kp.append(kj * tk)
                fl.append((1 if kj == 0 else 0) + (2 if kj == last else 0))
    tabs = [jnp.asarray(np.asarray(a, np.int32)) for a in (qb, kb, qp, kp, fl)]
    rows = h * tq
    grid_spec = pltpu.PrefetchScalarGridSpec(
        num_scalar_prefetch=5,
        grid=(len(qb),),
        in_specs=[
            pl.BlockSpec((h, tq, NOPE_DIM), lambda n, qb, kb, qp, kp, fl: (0, qb[n], 0)),
            pl.BlockSpec((h, tq, ROPE_DIM), lambda n, qb, kb, qp, kp, fl: (0, qb[n], 0)),
            pl.BlockSpec((tk, width), lambda n, qb, kb, qp, kp, fl: (kb[n], 0)),
            pl.BlockSpec((tk, ROPE_DIM), lambda n, qb, kb, qp, kp, fl: (kb[n], 0)),
            pl.BlockSpec((tk, h * V_DIM), lambda n, qb, kb, qp, kp, fl: (kb[n], 0)),
        ],
        out_specs=pl.BlockSpec((tq, h * V_DIM), lambda n, qb, kb, qp, kp, fl: (qb[n], 0)),
        scratch_shapes=[pltpu.VMEM((1, rows), F32), pltpu.VMEM((1, rows), F32),
                        pltpu.VMEM((h, V_DIM, tq), F32), pltpu.VMEM((tk, rows), F32),
                        pltpu.VMEM((tk, rows), BF16), pltpu.VMEM((1, rows), F32)],
    )
    return pl.pallas_call(
        functools.partial(_mla_prompt_kernel, tq=tq, tk=tk),
        grid_spec=grid_spec,
        out_shape=jax.ShapeDtypeStruct((batch * seq, h * V_DIM), BF16),
        compiler_params=_cparams(("arbitrary",)),
        name="mla_prompt",
    )(*tabs, q_nope, q_rope, k_nope, k_rope, v)


def _mla_sample_kernel(ql_ref, qr_ref, pl_ref, pr_ref, nl_ref, nr_ref, wuv_ref, o_ref,
                       m_ref, l_ref, acc_ref, a_ref, s_ref, p_ref, sn_ref, pn_ref,
                       *, tq, tk, n_past, n_kt):
    j = pl.program_id(1)

    @pl.when(j == 0)
    def _():
        _mla_init(m_ref, l_ref, acc_ref)

    @pl.when(j < n_kt)
    def _():
        _mla_latent_tile(ql_ref, qr_ref, pl_ref[...].astype(BF16), pr_ref[...].astype(BF16),
                         False, 0, 0, m_ref, l_ref, acc_ref, s_ref, p_ref, a_ref, tq, tk)

    @pl.when(j == n_kt)
    def _():
        _mla_latent_tile(ql_ref, qr_ref, nl_ref[...], nr_ref[...], True, n_past, n_past,
                         m_ref, l_ref, acc_ref, sn_ref, pn_ref, a_ref, tq, tq)
        _mla_latent_finalize(wuv_ref, o_ref, l_ref, acc_ref, tq)


def _mla_sample(q_lat, q_rope, past_lat, past_kr, layer, k_lat, k_rope, w_uv,
                row0, dec_batch, dec_seq):
    h, _, c = q_lat.shape
    n_past = past_lat.shape[2]
    tk = _pick_tile(n_past, MLA_K_TILE, 128)
    n_kt = n_past // tk
    blk0 = row0 // dec_seq
    rows = h * dec_seq
    past_idx = lambda b, j: (layer, b, jnp.minimum(j, n_kt - 1), 0)
    return pl.pallas_call(
        functools.partial(_mla_sample_kernel, tq=dec_seq, tk=tk, n_past=n_past, n_kt=n_kt),
        grid=(dec_batch, n_kt + 1),
        in_specs=[
            pl.BlockSpec((h, dec_seq, c), lambda b, j: (0, b, 0)),
            pl.BlockSpec((h, dec_seq, ROPE_DIM), lambda b, j: (0, blk0 + b, 0)),
            pl.BlockSpec((None, None, tk, c), past_idx),
            pl.BlockSpec((None, None, tk, ROPE_DIM), past_idx),
            pl.BlockSpec((dec_seq, c), lambda b, j: (blk0 + b, 0)),
            pl.BlockSpec((dec_seq, ROPE_DIM), lambda b, j: (blk0 + b, 0)),
            pl.BlockSpec((h, c, V_DIM), lambda b, j: (0, 0, 0)),
        ],
        out_specs=pl.BlockSpec((dec_seq, h * V_DIM), lambda b, j: (b, 0)),
        out_shape=jax.ShapeDtypeStruct((dec_batch * dec_seq, h * V_DIM), BF16),
        scratch_shapes=[pltpu.VMEM((1, rows), F32), pltpu.VMEM((1, rows), F32),
                        pltpu.VMEM((c, rows), F32), pltpu.VMEM((1, rows), F32),
                        pltpu.VMEM((tk, rows), F32), pltpu.VMEM((tk, rows), BF16),
                        pltpu.VMEM((dec_seq, rows), F32), pltpu.VMEM((dec_seq, rows), BF16)],
        compiler_params=_cparams(("parallel", "arbitrary")),
        name="mla_sample",
    )(q_lat, q_rope, past_lat, past_kr, k_lat, k_rope, w_uv)


def _scan_matrix(sub):
    j = np.arange(sub)[:, None]
    s = np.arange(sub)[None, :]
    u = (j > s).astype(np.float32)
    return jnp.asarray(np.concatenate([u, u], axis=0), dtype=BF16)


_SbChunk = collections.namedtuple("_SbChunk", "q k v before carry_idx acc_idx")


def _sb_scores(qc, kc, before):
    z = _dot_nt(qc, kc)
    softplus = jnp.log(1.0 + jnp.exp2(-jnp.abs(z))) * LOG2E
    log_beta = jnp.minimum(z, 0.0) - softplus
    log_rest = log_beta - z
    if before is not None:
        log_rest = jnp.where(before, log_rest, 0.0)
    hi = log_rest.astype(BF16)
    lo = (log_rest - hi.astype(F32)).astype(BF16)
    return (log_beta, jnp.concatenate([hi, lo], axis=-1),
            jnp.sum(log_rest, axis=-1, keepdims=True))


def _sb_weights(log_beta, hi_lo, u2, carry, before):
    a = jnp.exp2(log_beta + _dot(hi_lo, u2) + carry)
    if before is not None:
        a = jnp.where(before, a, 0.0)
    return a.astype(BF16)


def _sb_pipeline(chunks, u2, acc_ref, carry_ref):
    scored = None
    weighted = None
    for j in range(len(chunks) + 2):
        new_scored = None
        if j < len(chunks):
            ch = chunks[j]
            new_scored = (ch,) + _sb_scores(ch.q(), ch.k(), ch.before)
        new_weighted = None
        if scored is not None:
            ch, log_beta, hi_lo, row_sum = scored
            carry = carry_ref[ch.carry_idx]
            new_weighted = (ch, _sb_weights(log_beta, hi_lo, u2, carry, ch.before))
            carry_ref[ch.carry_idx] = carry + row_sum
        if weighted is not None:
            ch, a = weighted
            acc_ref[ch.acc_idx] += _dot(a, ch.v())
        scored, weighted = new_scored, new_weighted


def _sb_tile(q_ref, k_ref, v_ref, u2, acc_ref, carry_ref, diagonal, tq, tk, sub, rq):
    chunks = []
    for c in reversed(range(tk // sub)):
        k0, k1 = c * sub, (c + 1) * sub
        for r in range(tq // rq):
            r0, r1 = r * rq, (r + 1) * rq
            before = None
            if diagonal:
                if k0 >= r1 - 1:
                    continue
                if k1 > r0:
                    before = ((k0 + lax.broadcasted_iota(jnp.int32, (1, sub), 1))
                              < (r0 + lax.broadcasted_iota(jnp.int32, (rq, 1), 0)))
            for h in range(SB_HEADS):
                cols = slice(h * SB_DIM, (h + 1) * SB_DIM)
                rows, keys = slice(r0, r1), slice(k0, k1)
                chunks.append(_SbChunk(
                    q=functools.partial(lambda rr, cc: q_ref[rr, cc], rows, cols),
                    k=functools.partial(lambda kk, cc: k_ref[kk, cc], keys, cols),
                    v=functools.partial(lambda kk, cc: v_ref[kk, cc], keys, cols),
                    before=before, carry_idx=(h, rows, slice(None)), acc_idx=(rows, cols)))
    _sb_pipeline(chunks, u2, acc_ref, carry_ref)


def _sb_prompt_kernel(q_ref, k_ref, v_ref, u_ref, k_hbm, v_hbm, o_ref,
                      acc_ref, carry_ref, kbuf, vbuf, sem, *, nq, tq, sub, rq):
    g = pl.program_id(0)
    n_old = lax.rem(g, nq) * (tq // sub)

    def tile_copies(i, slot):
        start = g * tq - (i + 1) * sub
        return (pltpu.make_async_copy(k_hbm.at[pl.ds(start, sub), :], kbuf.at[slot], sem.at[0, slot]),
                pltpu.make_async_copy(v_hbm.at[pl.ds(start, sub), :], vbuf.at[slot], sem.at[1, slot]))

    def start_tile(i, slot):
        for copy in tile_copies(i, slot):
            copy.start()

    def wait_tile(i, slot):
        for copy in tile_copies(i, slot):
            copy.wait()

    def any_alive():
        return (jnp.max(carry_ref[...]) > SB_DEAD_LOG2).astype(jnp.int32)

    @pl.when(n_old > 0)
    def _():
        start_tile(0, 0)

    acc_ref[...] = jnp.zeros(acc_ref.shape, F32)
    carry_ref[...] = jnp.zeros(carry_ref.shape, F32)
    _sb_tile(q_ref, k_ref, v_ref, u_ref[...], acc_ref, carry_ref, True, tq, tq, sub, rq)

    def tile_step(state):
        i, _ = state
        slot = lax.rem(i, 2)
        wait_tile(i, slot)

        @pl.when(i + 1 < n_old)
        def _():
            start_tile(i + 1, 1 - slot)

        for r in range(tq // rq):
            rows = slice(r * rq, (r + 1) * rq)

            @pl.when(jnp.max(carry_ref[:, rows, :]) > SB_DEAD_LOG2)
            def _(rows=rows):
                chunks = []
                for h in range(SB_HEADS):
                    cols = slice(h * SB_DIM, (h + 1) * SB_DIM)
                    chunks.append(_SbChunk(
                        q=functools.partial(lambda rr, cc: q_ref[rr, cc], rows, cols),
                        k=functools.partial(lambda cc: kbuf[slot, :, cc], cols),
                        v=functools.partial(lambda cc: vbuf[slot, :, cc], cols),
                        before=None, carry_idx=(h, rows, slice(None)), acc_idx=(rows, cols)))
                _sb_pipeline(chunks, u_ref[...], acc_ref, carry_ref)
        return i + 1, any_alive()

    tiles_done, _ = lax.while_loop(
        lambda state: jnp.logical_and(state[0] < n_old, state[1] == 1),
        tile_step, (jnp.int32(0), any_alive()))

    @pl.when(tiles_done < n_old)
    def _():
        wait_tile(tiles_done, lax.rem(tiles_done, 2))

    o_ref[...] = acc_ref[...].astype(BF16)


def _sb_prompt(q, k, v, batch, seq):
    width = q.shape[1]
    tq = _pick_tile(seq, SB_TILE, 128)
    sub = _pick_tile(tq, SB_SUB, 128)
    rq = _pick_tile(tq, SB_ROWS, 16)
    nq = seq // tq
    rows = pl.BlockSpec((tq, width), lambda g: (g, 0))
    in_hbm = pl.BlockSpec(memory_space=pl.ANY)
    return pl.pallas_call(
        functools.partial(_sb_prompt_kernel, nq=nq, tq=tq, sub=sub, rq=rq),
        grid=(batch * nq,),
        in_specs=[rows, rows, rows, pl.BlockSpec((2 * sub, sub), lambda g: (0, 0)),
                  in_hbm, in_hbm],
        out_specs=rows,
        out_shape=jax.ShapeDtypeStruct((batch * seq, width), BF16),
        scratch_shapes=[pltpu.VMEM((tq, width), F32), pltpu.VMEM((SB_HEADS, tq, 1), F32),
                        pltpu.VMEM((2, sub, width), BF16), pltpu.VMEM((2, sub, width), BF16),
                        pltpu.SemaphoreType.DMA((2, 2))],
        compiler_params=_cparams(("arbitrary",)),
        name="sb_prompt",
    )(q, k, v, _scan_matrix(sub), k, v)


def _sb_sample_kernel(q_ref, nk_ref, nv_ref, un_ref, up_ref, pk_hbm, pv_hbm, o_ref,
                      acc_ref, carry_ref, kbuf, vbuf, sem, *, layer, tq, tk, n_kt):
    b = pl.program_id(0)

    def tile_copies(i, slot):
        start = (n_kt - 1 - i) * tk
        copies = []
        for h in range(SB_HEADS):
            copies.append(pltpu.make_async_copy(
                pk_hbm.at[layer, b, pl.ds(start, tk), h, :], kbuf.at[slot, h], sem.at[0, slot]))
            copies.append(pltpu.make_async_copy(
                pv_hbm.at[layer, b, pl.ds(start, tk), h, :], vbuf.at[slot, h], sem.at[1, slot]))
        return copies

    def start_tile(i, slot):
        for copy in tile_copies(i, slot):
            copy.start()

    def wait_tile(i, slot):
        for copy in tile_copies(i, slot):
            copy.wait()

    def alive():
        return (jnp.max(carry_ref[...]) > SB_DEAD_LOG2).astype(jnp.int32)

    start_tile(0, 0)

    acc_ref[...] = jnp.zeros(acc_ref.shape, F32)
    carry_ref[...] = jnp.zeros(carry_ref.shape, F32)
    before = (lax.broadcasted_iota(jnp.int32, (1, tq), 1)
              < lax.broadcasted_iota(jnp.int32, (tq, 1), 0))
    chunks = []
    for h in range(SB_HEADS):
        cols = slice(h * SB_DIM, (h + 1) * SB_DIM)
        chunks.append(_SbChunk(
            q=functools.partial(lambda cc: q_ref[:, cc], cols),
            k=functools.partial(lambda cc: nk_ref[:, cc], cols),
            v=functools.partial(lambda cc: nv_ref[:, cc], cols),
            before=before, carry_idx=(h, slice(None), slice(None)), acc_idx=(slice(None), cols)))
    _sb_pipeline(chunks, un_ref[...], acc_ref, carry_ref)

    def tile_step(state):
        i, _ = state
        slot = lax.rem(i, 2)
        wait_tile(i, slot)

        @pl.when(i + 1 < n_kt)
        def _():
            start_tile(i + 1, 1 - slot)

        chunks = []
        for h in range(SB_HEADS):
            cols = slice(h * SB_DIM, (h + 1) * SB_DIM)
            chunks.append(_SbChunk(
                q=functools.partial(lambda cc: q_ref[:, cc], cols),
                k=functools.partial(lambda hh: kbuf[slot, hh].astype(BF16), h),
                v=functools.partial(lambda hh: vbuf[slot, hh].astype(BF16), h),
                before=None, carry_idx=(h, slice(None), slice(None)), acc_idx=(slice(None), cols)))
        _sb_pipeline(chunks, up_ref[...], acc_ref, carry_ref)
        return i + 1, alive()

    tiles_done, _ = lax.while_loop(
        lambda state: jnp.logical_and(state[0] < n_kt, state[1] == 1),
        tile_step, (jnp.int32(0), alive()))

    @pl.when(tiles_done < n_kt)
    def _():
        wait_tile(tiles_done, lax.rem(tiles_done, 2))

    o_ref[...] = acc_ref[...].astype(BF16)


def _sb_sample(q, past_k, past_v, layer, k, v, row0, dec_batch, dec_seq):
    width = q.shape[1]
    n_past = past_k.shape[2]
    tk = _pick_tile(n_past, SB_SUB, 128)
    n_kt = n_past // tk
    blk0 = row0 // dec_seq
    new_rows = pl.BlockSpec((dec_seq, width), lambda b: (blk0 + b, 0))
    in_hbm = pl.BlockSpec(memory_space=pl.ANY)
    return pl.pallas_call(
        functools.partial(_sb_sample_kernel, layer=layer, tq=dec_seq, tk=tk, n_kt=n_kt),
        grid=(dec_batch,),
        in_specs=[
            new_rows, new_rows, new_rows,
            pl.BlockSpec((2 * dec_seq, dec_seq), lambda b: (0, 0)),
            pl.BlockSpec((2 * tk, tk), lambda b: (0, 0)),
            in_hbm, in_hbm,
        ],
        out_specs=pl.BlockSpec((dec_seq, width), lambda b: (b, 0)),
        out_shape=jax.ShapeDtypeStruct((dec_batch * dec_seq, width), BF16),
        scratch_shapes=[pltpu.VMEM((dec_seq, width), F32),
                        pltpu.VMEM((SB_HEADS, dec_seq, 1), F32),
                        pltpu.VMEM((2, SB_HEADS, tk, SB_DIM), F32),
                        pltpu.VMEM((2, SB_HEADS, tk, SB_DIM), F32),
                        pltpu.SemaphoreType.DMA((2, 2))],
        compiler_params=_cparams(("arbitrary",)),
        name="sb_sample",
    )(q, k, v, _scan_matrix(dec_seq), _scan_matrix(tk), past_k, past_v)


def _merge_kernel(oap_ref, oas_ref, obp_ref, obs_ref, wa_ref, wb_ref, g_ref, o_ref,
                  *, n_p, tn):
    i = pl.program_id(0)
    o_a = _load_split(i, n_p, oap_ref, oas_ref)
    o_b = _load_split(i, n_p, obp_ref, obs_ref)
    n = o_ref.shape[1]
    for c in range(n // tn):
        cols = slice(c * tn, (c + 1) * tn)
        a = _dot(o_a, wa_ref[:, cols])
        b = _dot(o_b, wb_ref[:, cols])
        g_a = g_ref[:, cols].astype(F32)
        g_b = g_ref[:, n + c * tn:n + (c + 1) * tn].astype(F32)
        o_ref[:, cols] = (g_a * a + g_b * b).astype(BF16)


def _merge(oa_p, oa_s, ob_p, ob_s, w_a, w_b, gates, tm):
    da = oa_p.shape[1]
    db = ob_p.shape[1]
    t = oa_p.shape[0] + oa_s.shape[0]
    n_p = oa_p.shape[0] // tm
    n = w_a.shape[1]
    tn = _pick_tile(n, COL_TILE, 128)
    return pl.pallas_call(
        functools.partial(_merge_kernel, n_p=n_p, tn=tn),
        grid=(t // tm,),
        in_specs=[*_split_specs(tm, da, n_p),
                  *_split_specs(tm, db, n_p),
                  _resident((da, n)), _resident((db, n)),
                  pl.BlockSpec((tm, 2 * n), lambda i: (i, 0))],
        out_specs=pl.BlockSpec((tm, n), lambda i: (i, 0)),
        out_shape=jax.ShapeDtypeStruct((t, n), BF16),
        compiler_params=_cparams(("parallel",)),
        name="merge",
    )(oa_p, oa_s, ob_p, ob_s, w_a, w_b, gates)


def _res_ln_finalize(z_ref, g_ref, b_ref, o_ref, ob_ref):
    n_tiles, _, tn = z_ref.shape
    width = n_tiles * tn
    total = z_ref[0].sum(axis=-1, keepdims=True)
    for c in range(1, n_tiles):
        total += z_ref[c].sum(axis=-1, keepdims=True)
    mu = total / width
    sq = jnp.square(z_ref[0] - mu).sum(axis=-1, keepdims=True)
    for c in range(1, n_tiles):
        sq += jnp.square(z_ref[c] - mu).sum(axis=-1, keepdims=True)
    rstd = lax.rsqrt(sq / width + LN_EPS)
    for c in range(n_tiles):
        cols = slice(c * tn, (c + 1) * tn)
        y = (z_ref[c] - mu) * rstd * g_ref[:, cols] + b_ref[:, cols]
        o_ref[:, cols] = y
        if ob_ref is not None:
            ob_ref[:, cols] = y.astype(BF16)


def _out_ln_kernel(m_ref, w_ref, x_ref, g_ref, b_ref, o_ref, ob_ref, z_ref, *, alpha):
    m = m_ref[...]
    n_tiles, _, tn = z_ref.shape
    for c in range(n_tiles):
        cols = slice(c * tn, (c + 1) * tn)
        z_ref[c] = alpha * x_ref[:, cols] + _dot(m, w_ref[:, cols])
    _res_ln_finalize(z_ref, g_ref, b_ref, o_ref, ob_ref)


def _out_ln(merged, w, x, g, b, alpha, tm):
    t, d = x.shape
    k = merged.shape[1]
    tn = _pick_tile(d, COL_TILE, 128)
    full_row = pl.BlockSpec((tm, d), lambda i: (i, 0))
    vec = pl.BlockSpec((1, d), lambda i: (0, 0))
    return pl.pallas_call(
        functools.partial(_out_ln_kernel, alpha=alpha),
        grid=(t // tm,),
        in_specs=[pl.BlockSpec((tm, k), lambda i: (i, 0)), _resident((k, d)), full_row, vec, vec],
        out_specs=[full_row, full_row],
        out_shape=[jax.ShapeDtypeStruct((t, d), F32), jax.ShapeDtypeStruct((t, d), BF16)],
        scratch_shapes=[pltpu.VMEM((d // tn, tm, tn), F32)],
        compiler_params=_cparams(("parallel",)),
        name="out_ln",
    )(merged, w, x, g.reshape(1, d), b.reshape(1, d))


def _ffn_ln_kernel(xb_ref, wg_ref, wu_ref, wd_ref, x_ref, g_ref, b_ref, o_ref, ob_ref,
                   acc_ref, *, alpha):
    f = pl.program_id(1)

    @pl.when(f == 0)
    def _():
        acc_ref[...] = jnp.zeros(acc_ref.shape, F32)

    xb = xb_ref[...]
    gate = _dot(xb, wg_ref[...])
    up = _dot(xb, wu_ref[...])
    hidden = (gate * _sigmoid(gate) * up).astype(BF16)
    acc_ref[...] += _dot(hidden, wd_ref[...])

    @pl.when(f == pl.num_programs(1) - 1)
    def _():
        y = _layer_norm_rows(alpha * x_ref[...] + acc_ref[...], g_ref[...], b_ref[...])
        o_ref[...] = y
        ob_ref[...] = y.astype(BF16)


def _ffn_ln(xb, w_gu, w_down, x, g, b, alpha, tm):
    t, d = x.shape
    ff = w_down.shape[0]
    tf = _pick_tile(ff, COL_TILE, 128)
    nf = ff // tf
    full_row = pl.BlockSpec((tm, d), lambda i, f: (i, 0))
    vec = pl.BlockSpec((1, d), lambda i, f: (0, 0))
    return pl.pallas_call(
        functools.partial(_ffn_ln_kernel, alpha=alpha),
        grid=(t // tm, nf),
        in_specs=[full_row,
                  pl.BlockSpec((d, tf), lambda i, f: (0, f)),
                  pl.BlockSpec((d, tf), lambda i, f: (0, f + nf)),
                  pl.BlockSpec((tf, d), lambda i, f: (f, 0)),
                  full_row, vec, vec],
        out_specs=[full_row, full_row],
        out_shape=[jax.ShapeDtypeStruct((t, d), F32), jax.ShapeDtypeStruct((t, d), BF16)],
        scratch_shapes=[pltpu.VMEM((tm, d), F32)],
        compiler_params=_cparams(("parallel", "arbitrary")),
        name="ffn_ln",
    )(xb, w_gu, w_gu, w_down, x, g.reshape(1, d), b.reshape(1, d))


def _ple_ln_kernel(xb_ref, wg_ref, bg_ref, pp_ref, ps_ref, wp_ref, x_ref, g_ref, b_ref,
                   o1_ref, o2_ref, z_ref, *, alpha, n_p, last):
    i = pl.program_id(0)
    xb = xb_ref[...]
    p = _load_split(i, n_p, pp_ref, ps_ref).astype(BF16)
    n_tiles, _, tn = z_ref.shape
    for c in range(n_tiles):
        cols = slice(c * tn, (c + 1) * tn)
        gate = _sigmoid(_dot(xb, wg_ref[:, cols]) + bg_ref[:, cols])
        z_ref[c] = alpha * x_ref[:, cols] + gate * _dot(p, wp_ref[:, cols])

    if not last:
        _res_ln_finalize(z_ref, g_ref, b_ref, o1_ref, o2_ref)
    else:
        @pl.when(i < n_p)
        def _():
            _res_ln_finalize(z_ref, g_ref, b_ref, o1_ref, None)

        @pl.when(i >= n_p)
        def _():
            _res_ln_finalize(z_ref, g_ref, b_ref, o2_ref, None)


def _ple_ln(xb, w_gate, b_gate, p_p, p_s, layer, w_proj, x, g, b, alpha, tm, last):
    t, d = x.shape
    pd = p_p.shape[2]
    tp = p_p.shape[1]
    n_p = tp // tm
    tn = _pick_tile(d, COL_TILE, 128)
    full_row = pl.BlockSpec((tm, d), lambda i: (i, 0))
    vec = pl.BlockSpec((1, d), lambda i: (0, 0))
    if last:
        out_specs = list(_split_specs(tm, d, n_p))
        out_shape = [jax.ShapeDtypeStruct((tp, d), F32), jax.ShapeDtypeStruct((t - tp, d), F32)]
    else:
        out_specs = [full_row, full_row]
        out_shape = [jax.ShapeDtypeStruct((t, d), F32), jax.ShapeDtypeStruct((t, d), BF16)]
    return pl.pallas_call(
        functools.partial(_ple_ln_kernel, alpha=alpha, n_p=n_p, last=last),
        grid=(t // tm,),
        in_specs=[full_row, _resident((d, d)), vec,
                  pl.BlockSpec((None, tm, pd), lambda i: (layer, jnp.minimum(i, n_p - 1), 0)),
                  pl.BlockSpec((None, tm, pd), lambda i: (layer, jnp.maximum(i - n_p, 0), 0)),
                  _resident((pd, d)), full_row, vec, vec],
        out_specs=out_specs,
        out_shape=out_shape,
        scratch_shapes=[pltpu.VMEM((d // tn, tm, tn), F32)],
        compiler_params=_cparams(("arbitrary",)),
        name="ple_ln",
    )(xb, w_gate, b_gate.reshape(1, d), p_p, p_s, w_proj, x, g.reshape(1, d), b.reshape(1, d))


def _rope_tables(pos):
    half = ROPE_DIM // 2
    inv_freq = 1.0 / (ROPE_THETA ** (jnp.arange(half, dtype=F32) * (2.0 / ROPE_DIM)))
    ang = pos.astype(F32)[:, None] * inv_freq[None, :]
    return jnp.cos(ang), jnp.sin(ang)


def kernel(x_prompt, x_sample, cache_mla_latent, cache_mla_krope, cache_sb_k, cache_sb_v,
           p_prompt, p_sample, ln_in_g, ln_in_b, w_in, b_gate, q_a_norm_g, w_q_b,
           kv_a_norm_g, w_kv_b, w_branch_a, w_branch_b, w_out, ln1_g, ln1_b,
           w_ffn_gu, w_ffn_down, ln2_g, ln2_b, w_ple_gate, b_ple_gate, w_ple_proj,
           ln3_g, ln3_b):
    batch, seq, d = x_prompt.shape
    dec_batch, dec_seq, _ = x_sample.shape
    depth = w_in.shape[0]
    n_past = cache_mla_latent.shape[2]
    tp = batch * seq
    ts = dec_batch * dec_seq
    sb_width = SB_HEADS * SB_DIM
    alpha = (2 * depth) ** 0.25

    pos = jnp.concatenate([jnp.tile(jnp.arange(seq, dtype=jnp.int32), batch),
                           jnp.tile(n_past + jnp.arange(dec_seq, dtype=jnp.int32), dec_batch)])
    cos, sin = _rope_tables(pos)

    tm = _pick_tile(math.gcd(tp, ts), ROW_TILE, 16)
    p_p = p_prompt.reshape(depth, tp, -1)
    p_s = p_sample.reshape(depth, ts, -1)

    x, xb = _ln_in(x_prompt.reshape(tp, d), x_sample.reshape(ts, d), ln_in_g, ln_in_b, tm)

    splits = np.cumsum([0, Q_LORA, KV_LORA, ROPE_DIM, sb_width, sb_width, sb_width, 2 * d])
    def cache_arrays(widths):
        return [jnp.zeros((depth, n, w), F32) for w in widths for n in (tp, ts)]

    kv_cache = cache_arrays((KV_LORA, ROPE_DIM))
    sb_cache = cache_arrays((sb_width, sb_width))
    for l in range(depth):
        w_cols = [w_in[l][:, splits[i]:splits[i + 1]].astype(BF16) for i in range(7)]
        w_qa, w_lat, w_kr, w_sq, w_sk, w_sv, w_gates = w_cols
        wq = w_q_b[l].reshape(Q_LORA, MLA_HEADS, NOPE_DIM + ROPE_DIM)
        wq_nope = wq[..., :NOPE_DIM].reshape(Q_LORA, MLA_HEADS * NOPE_DIM).astype(BF16)
        wq_rope = wq[..., NOPE_DIM:].reshape(Q_LORA, MLA_HEADS * ROPE_DIM).astype(BF16)
        w_kv = w_kv_b[l].reshape(KV_LORA, MLA_HEADS, NOPE_DIM + V_DIM)
        wuk_h = w_kv[..., :NOPE_DIM].transpose(1, 2, 0).astype(BF16)
        wuv_h = w_kv[..., NOPE_DIM:].transpose(1, 0, 2).astype(BF16)
        wuk_all = w_kv[..., :NOPE_DIM].reshape(KV_LORA, MLA_HEADS * NOPE_DIM).astype(BF16)
        wuv_all = w_kv[..., NOPE_DIM:].reshape(KV_LORA, MLA_HEADS * V_DIM).astype(BF16)

        qa_n = _proj_qa(xb, w_qa, q_a_norm_g[l], tm)
        *kv_cache, lat_b, kr_b = _proj_kv(xb, w_lat, w_kr, kv_a_norm_g[l], cos, sin,
                                          tm, tp, l, depth, kv_cache)
        *sb_cache, sbq, sbk_b, sbv_b = _proj_sb(xb, w_sq, w_sk, w_sv, tm, tp, l, depth, sb_cache)
        gates = _proj_gates(xb, w_gates, b_gate[l], tm)

        q_nope, q_lat, q_rope = _q_heads(qa_n, wq_nope, wq_rope, wuk_h, cos, sin, tm, tp)
        k_nope, v_heads = _kv_heads(lat_b, wuk_all, wuv_all, tm, tp)
        oa_p = _mla_prompt(q_nope, q_rope, k_nope, kr_b, v_heads, batch, seq)
        oa_s = _mla_sample(q_lat, q_rope, cache_mla_latent, cache_mla_krope, l, lat_b, kr_b,
                           wuv_h, tp, dec_batch, dec_seq)
        ob_p = _sb_prompt(sbq, sbk_b, sbv_b, batch, seq)
        ob_s = _sb_sample(sbq, cache_sb_k, cache_sb_v, l, sbk_b, sbv_b, tp, dec_batch, dec_seq)

        merged = _merge(oa_p, oa_s, ob_p, ob_s, w_branch_a[l].astype(BF16),
                        w_branch_b[l].astype(BF16), gates, tm)
        x, xb = _out_ln(merged, w_out[l].astype(BF16), x, ln1_g[l], ln1_b[l], alpha, tm)
        x, xb = _ffn_ln(xb, w_ffn_gu[l].astype(BF16), w_ffn_down[l].astype(BF16), x,
                        ln2_g[l], ln2_b[l], alpha, tm)
        x, xb = _ple_ln(xb, w_ple_gate[l].astype(BF16), b_ple_gate[l], p_p, p_s, l,
                        w_ple_proj[l].astype(BF16), x, ln3_g[l], ln3_b[l], alpha, tm,
                        last=(l == depth - 1))

    lat_p, lat_s, kr_p, kr_s = kv_cache
    k_p, k_s, v_p, v_s = sb_cache
    head_dims = (SB_HEADS, SB_DIM)
    return (x.reshape(batch, seq, d), xb.reshape(dec_batch, dec_seq, d),
            lat_p.reshape(depth, batch, seq, KV_LORA), kr_p.reshape(depth, batch, seq, ROPE_DIM),
            k_p.reshape(depth, batch, seq, *head_dims), v_p.reshape(depth, batch, seq, *head_dims),
            lat_s.reshape(depth, dec_batch, dec_seq, KV_LORA),
            kr_s.reshape(depth, dec_batch, dec_seq, ROPE_DIM),
            k_s.reshape(depth, dec_batch, dec_seq, *head_dims),
            v_s.reshape(depth, dec_batch, dec_seq, *head_dims))
```

```python
import collections
import functools
import math

import numpy as np
import jax
import jax.numpy as jnp
from jax import lax
from jax.experimental import pallas as pl
from jax.experimental.pallas import tpu as pltpu

CHUNK = 64
MLA_HEADS = 16
Q_LORA = 512
KV_LORA = 512
NOPE_DIM = 128
ROPE_DIM = 64
V_DIM = 128
ROPE_THETA = 10000.0
MLA_SCALE = (NOPE_DIM + ROPE_DIM) ** -0.5
SB_HEADS = 8
SB_DIM = 128
SB_SCALE = SB_DIM ** -0.5
LN_EPS = 1e-5
RMS_EPS = 1e-6
LOG2E = 1.4426950408889634
MLA_QSCALE = MLA_SCALE * LOG2E
SB_QSCALE = SB_SCALE * LOG2E
SB_DEAD_LOG2 = -160.0

BF16 = jnp.bfloat16
F32 = jnp.float32

V7X_VMEM_LIMIT_BYTES = 56 * 1024 * 1024
ROW_TILE = 512
COL_TILE = 512
PROJ_COL_TILE = 1024
MLA_Q_TILE = 256
MLA_K_TILE = 512
MLA_SOFTMAX_COLS = 128
SB_TILE = 512
SB_SUB = 256
SB_ROWS = 128
CAST_BLOCK_BYTES = 6 * 1024 * 1024
NEG_BIG = -1e30


def _pick_tile(n, target, mult):
    best = None
    for t in range(mult, min(n, target) + 1, mult):
        if n % t == 0:
            best = t
    if best is None:
        return n
    return best


def _cparams(sem):
    return pltpu.CompilerParams(dimension_semantics=sem,
                                vmem_limit_bytes=V7X_VMEM_LIMIT_BYTES)


def _resident(shape):
    return pl.BlockSpec(shape, lambda *_: (0,) * len(shape), pipeline_mode=pl.Buffered(1))


def _dot(a, b):
    return jnp.dot(a, b, preferred_element_type=F32)


def _dot_nt(a, b):
    return lax.dot_general(a, b, (((1,), (1,)), ((), ())), preferred_element_type=F32)


def _layer_norm_rows(z, g, b):
    mu = jnp.mean(z, axis=-1, keepdims=True)
    zc = z - mu
    var = jnp.mean(zc * zc, axis=-1, keepdims=True)
    return zc * lax.rsqrt(var + LN_EPS) * g + b


def _rms_norm_rows(z, g):
    return z * lax.rsqrt(jnp.mean(z * z, axis=-1, keepdims=True) + RMS_EPS) * g


def _sigmoid(z):
    return 1.0 / (1.0 + jnp.exp(-z))


def _rope_rows(x, cos, sin):
    half = ROPE_DIM // 2
    x1 = x[:, :half]
    x2 = x[:, half:]
    return jnp.concatenate([x1 * cos - x2 * sin, x2 * cos + x1 * sin], axis=-1)


def _split_specs(tm, width, n_p):
    return (pl.BlockSpec((tm, width), lambda i, *_: (jnp.minimum(i, n_p - 1), 0)),
            pl.BlockSpec((tm, width), lambda i, *_: (jnp.maximum(i - n_p, 0), 0)))


def _load_split(i, n_p, p_ref, s_ref):
    return jnp.where(i < n_p, p_ref[...], s_ref[...])


def _store_split(i, n_p, p_ref, s_ref, value):
    @pl.when(i < n_p)
    def _():
        p_ref[...] = value

    @pl.when(i >= n_p)
    def _():
        s_ref[...] = value


def _ln_in_kernel(xp_ref, xs_ref, g_ref, b_ref, o_ref, ob_ref, *, n_p):
    x = _load_split(pl.program_id(0), n_p, xp_ref, xs_ref)
    y = _layer_norm_rows(x, g_ref[...], b_ref[...])
    o_ref[...] = y
    ob_ref[...] = y.astype(BF16)


def _ln_in(x_p, x_s, g, b, tm):
    d = x_p.shape[1]
    t = x_p.shape[0] + x_s.shape[0]
    n_p = x_p.shape[0] // tm
    row = pl.BlockSpec((tm, d), lambda i: (i, 0))
    vec = pl.BlockSpec((1, d), lambda i: (0, 0))
    return pl.pallas_call(
        functools.partial(_ln_in_kernel, n_p=n_p),
        grid=(t // tm,),
        in_specs=[*_split_specs(tm, d, n_p), vec, vec],
        out_specs=[row, row],
        out_shape=[jax.ShapeDtypeStruct((t, d), F32), jax.ShapeDtypeStruct((t, d), BF16)],
        compiler_params=_cparams(("parallel",)),
        name="ln_in",
    )(x_p, x_s, g.reshape(1, d), b.reshape(1, d))


def _proj_qa_kernel(x_ref, w_ref, g_ref, o_ref):
    o_ref[...] = _rms_norm_rows(_dot(x_ref[...], w_ref[...]), g_ref[...]).astype(BF16)


def _proj_kv_kernel(x_ref, wl_ref, wr_ref, g_ref, cos_ref, sin_ref, *refs, n_p):
    latp_ref, lats_ref, krp_ref, krs_ref, latb_ref, krb_ref = refs[-6:]
    i = pl.program_id(0)
    x = x_ref[...]
    lat = _rms_norm_rows(_dot(x, wl_ref[...]), g_ref[...])
    kr = _rope_rows(_dot(x, wr_ref[...]), cos_ref[...], sin_ref[...])
    _store_split(i, n_p, latp_ref, lats_ref, lat)
    _store_split(i, n_p, krp_ref, krs_ref, kr)
    latb_ref[...] = lat.astype(BF16)
    krb_ref[...] = kr.astype(BF16)


def _proj_sb_kernel(x_ref, wq_ref, wk_ref, wv_ref, *refs, n_p):
    kp_ref, ks_ref, vp_ref, vs_ref, q_ref, kb_ref, vb_ref = refs[-7:]
    i = pl.program_id(0)
    x = x_ref[...]
    q_ref[...] = (_dot(x, wq_ref[...]) * SB_QSCALE).astype(BF16)
    k = _dot(x, wk_ref[...])
    _store_split(i, n_p, kp_ref, ks_ref, k)
    kb_ref[...] = k.astype(BF16)
    v = _dot(x, wv_ref[...])
    _store_split(i, n_p, vp_ref, vs_ref, v)
    vb_ref[...] = v.astype(BF16)


def _proj_gate_kernel(x_ref, w_ref, b_ref, o_ref, *, tn):
    x = x_ref[...]
    for c in range(o_ref.shape[1] // tn):
        cols = slice(c * tn, (c + 1) * tn)
        o_ref[:, cols] = _sigmoid(_dot(x, w_ref[:, cols]) + b_ref[:, cols]).astype(BF16)


def _proj_qa(xb, w, g, tm):
    t, d = xb.shape
    n = w.shape[1]
    return pl.pallas_call(
        _proj_qa_kernel,
        grid=(t // tm,),
        in_specs=[pl.BlockSpec((tm, d), lambda i: (i, 0)),
                  pl.BlockSpec((d, n), lambda i: (0, 0)),
                  pl.BlockSpec((1, n), lambda i: (0, 0))],
        out_specs=pl.BlockSpec((tm, n), lambda i: (i, 0)),
        out_shape=jax.ShapeDtypeStruct((t, n), BF16),
        compiler_params=_cparams(("parallel",)),
        name="proj_qa",
    )(xb, w, g.reshape(1, n))


def _cache_outputs(widths, tm, tp, ts, layer, depth, prev, n_inputs):
    n_p = tp // tm
    out_specs, out_shape = [], []
    for w in widths:
        out_specs += [
            pl.BlockSpec((None, tm, w), lambda i: (layer, jnp.minimum(i, n_p - 1), 0)),
            pl.BlockSpec((None, tm, w), lambda i: (layer, jnp.maximum(i - n_p, 0), 0))]
        out_shape += [jax.ShapeDtypeStruct((depth, tp, w), F32),
                      jax.ShapeDtypeStruct((depth, ts, w), F32)]
    extra = list(prev)
    extra_specs = [pl.BlockSpec(memory_space=pl.ANY) for _ in extra]
    aliases = {n_inputs + k: k for k in range(len(extra))}
    return extra, extra_specs, out_specs, out_shape, aliases


def _proj_kv(xb, w_lat, w_kr, g, cos, sin, tm, tp, layer, depth, prev):
    t, d = xb.shape
    c = w_lat.shape[1]
    r = w_kr.shape[1]
    rows = lambda w: pl.BlockSpec((tm, w), lambda i: (i, 0))
    full = lambda a, b: pl.BlockSpec((a, b), lambda i: (0, 0))
    extra, extra_specs, out_specs, out_shape, aliases = _cache_outputs(
        (c, r), tm, tp, t - tp, layer, depth, prev, 6)
    return pl.pallas_call(
        functools.partial(_proj_kv_kernel, n_p=tp // tm),
        grid=(t // tm,),
        in_specs=[rows(d), full(d, c), full(d, r), full(1, c), rows(r // 2), rows(r // 2)]
                 + extra_specs,
        out_specs=out_specs + [rows(c), rows(r)],
        out_shape=out_shape + [jax.ShapeDtypeStruct((t, c), BF16),
                               jax.ShapeDtypeStruct((t, r), BF16)],
        input_output_aliases=aliases,
        compiler_params=_cparams(("arbitrary",)),
        name="proj_kv",
    )(xb, w_lat, w_kr, g.reshape(1, c), cos, sin, *extra)


def _proj_sb(xb, w_q, w_k, w_v, tm, tp, layer, depth, prev):
    t, d = xb.shape
    n = w_q.shape[1]
    rows = lambda w: pl.BlockSpec((tm, w), lambda i: (i, 0))
    weight = _resident((d, n))
    bf16 = jax.ShapeDtypeStruct((t, n), BF16)
    extra, extra_specs, out_specs, out_shape, aliases = _cache_outputs(
        (n, n), tm, tp, t - tp, layer, depth, prev, 4)
    return pl.pallas_call(
        functools.partial(_proj_sb_kernel, n_p=tp // tm),
        grid=(t // tm,),
        in_specs=[rows(d), weight, weight, weight] + extra_specs,
        out_specs=out_specs + [rows(n), rows(n), rows(n)],
        out_shape=out_shape + [bf16, bf16, bf16],
        input_output_aliases=aliases,
        compiler_params=_cparams(("arbitrary",)),
        name="proj_sb",
    )(xb, w_q, w_k, w_v, *extra)


def _proj_gates(xb, w, bias, tm):
    t, d = xb.shape
    n = w.shape[1]
    tn = _pick_tile(n, PROJ_COL_TILE, 128)
    return pl.pallas_call(
        functools.partial(_proj_gate_kernel, tn=tn),
        grid=(t // tm,),
        in_specs=[pl.BlockSpec((tm, d), lambda i: (i, 0)), _resident((d, n)),
                  pl.BlockSpec((1, n), lambda i: (0, 0))],
        out_specs=pl.BlockSpec((tm, n), lambda i: (i, 0)),
        out_shape=jax.ShapeDtypeStruct((t, n), BF16),
        compiler_params=_cparams(("parallel",)),
        name="proj_gates",
    )(xb, w, bias.reshape(1, n))


def _q_heads_kernel(qa_ref, wn_ref, wr_ref, wuk_ref, cos_ref, sin_ref, qn_ref, ql_ref, qr_ref,
                    *, n_p):
    i = pl.program_id(0)
    qa = qa_ref[...]
    q_nope = _dot(qa, wn_ref[...])
    q_rope = _dot(qa, wr_ref[...])
    cos = cos_ref[...]
    sin = sin_ref[...]
    for h in range(MLA_HEADS):
        qr = _rope_rows(q_rope[:, h * ROPE_DIM:(h + 1) * ROPE_DIM], cos, sin)
        qr_ref[h] = (qr * MLA_QSCALE).astype(BF16)

    @pl.when(i < n_p)
    def _():
        for h in range(MLA_HEADS):
            qn_ref[h] = (q_nope[:, h * NOPE_DIM:(h + 1) * NOPE_DIM] * MLA_QSCALE).astype(BF16)

    @pl.when(i >= n_p)
    def _():
        for h in range(MLA_HEADS):
            nope = q_nope[:, h * NOPE_DIM:(h + 1) * NOPE_DIM].astype(BF16)
            ql_ref[h] = (_dot(nope, wuk_ref[h]) * MLA_QSCALE).astype(BF16)


def _q_heads(qa_n, w_nope, w_rope, wuk_h, cos, sin, tm, tp):
    t, ql = qa_n.shape
    h, _, c = wuk_h.shape
    half = ROPE_DIM // 2
    n_p = tp // tm
    return pl.pallas_call(
        functools.partial(_q_heads_kernel, n_p=n_p),
        grid=(t // tm,),
        in_specs=[pl.BlockSpec((tm, ql), lambda i: (i, 0)),
                  pl.BlockSpec((ql, h * NOPE_DIM), lambda i: (0, 0)),
                  pl.BlockSpec((ql, h * ROPE_DIM), lambda i: (0, 0)),
                  pl.BlockSpec((h, NOPE_DIM, c), lambda i: (0, 0, 0)),
                  pl.BlockSpec((tm, half), lambda i: (i, 0)),
                  pl.BlockSpec((tm, half), lambda i: (i, 0))],
        out_specs=[pl.BlockSpec((h, tm, NOPE_DIM), lambda i: (0, jnp.minimum(i, n_p - 1), 0)),
                   pl.BlockSpec((h, tm, c), lambda i: (0, jnp.maximum(i - n_p, 0), 0)),
                   pl.BlockSpec((h, tm, ROPE_DIM), lambda i: (0, i, 0))],
        out_shape=[jax.ShapeDtypeStruct((h, tp, NOPE_DIM), BF16),
                   jax.ShapeDtypeStruct((h, t - tp, c), BF16),
                   jax.ShapeDtypeStruct((h, t, ROPE_DIM), BF16)],
        compiler_params=_cparams(("arbitrary",)),
        name="q_heads",
    )(qa_n, w_nope, w_rope, wuk_h, cos, sin)


def _kv_heads_kernel(lat_ref, wk_ref, wv_ref, k_ref, v_ref):
    lat = lat_ref[...]
    k_ref[...] = _dot(lat, wk_ref[...]).astype(BF16)
    v_ref[...] = _dot(lat, wv_ref[...]).astype(BF16)


def _kv_heads(lat_b, w_uk, w_uv, tm, tp):
    c = lat_b.shape[1]
    n = w_uk.shape[1]
    rows = lambda w: pl.BlockSpec((tm, w), lambda i: (i, 0))
    return pl.pallas_call(
        _kv_heads_kernel,
        grid=(tp // tm,),
        in_specs=[rows(c), _resident((c, n)), _resident((c, n))],
        out_specs=[rows(n), rows(n)],
        out_shape=[jax.ShapeDtypeStruct((tp, n), BF16), jax.ShapeDtypeStruct((tp, n), BF16)],
        compiler_params=_cparams(("parallel",)),
        name="kv_heads",
    )(lat_b, w_uk, w_uv)


def _mla_init(m_ref, l_ref, acc_ref):
    m_ref[...] = jnp.full(m_ref.shape, NEG_BIG, F32)
    l_ref[...] = jnp.zeros(l_ref.shape, F32)
    acc_ref[...] = jnp.zeros(acc_ref.shape, F32)


def _mla_softmax(masked, q_pos0, k_pos0, m_ref, l_ref, s_ref, p_ref, a_ref, tq, tk):
    rows = MLA_HEADS * tq
    if masked:
        k_chunk = (k_pos0 + lax.broadcasted_iota(jnp.int32, (tk, 1), 0)) // CHUNK
    for c in range(rows // MLA_SOFTMAX_COLS):
        cols = slice(c * MLA_SOFTMAX_COLS, (c + 1) * MLA_SOFTMAX_COLS)
        s = s_ref[:, cols]
        if masked:
            lane = c * MLA_SOFTMAX_COLS + lax.broadcasted_iota(jnp.int32, (1, MLA_SOFTMAX_COLS), 1)
            q_chunk = (q_pos0 + lax.rem(lane, tq)) // CHUNK
            s = jnp.where(k_chunk <= q_chunk, s, NEG_BIG)
        m_prev = m_ref[:, cols]
        m_new = jnp.maximum(m_prev, jnp.max(s, axis=0, keepdims=True))
        alpha = jnp.exp2(m_prev - m_new)
        p = jnp.exp2(s - m_new)
        l_ref[:, cols] = alpha * l_ref[:, cols] + jnp.sum(p, axis=0, keepdims=True)
        m_ref[:, cols] = m_new
        a_ref[:, cols] = alpha
        p_ref[:, cols] = p.astype(BF16)


def _dot_tn(a, b):
    return lax.dot_general(a, b, (((0,), (0,)), ((), ())), preferred_element_type=F32)


def _mla_latent_tile(ql_ref, qr_ref, klat, kr, masked, q_pos0, k_pos0,
                     m_ref, l_ref, acc_ref, s_ref, p_ref, a_ref, tq, tk):
    rows = MLA_HEADS * tq
    q_lat = ql_ref[...].reshape(rows, ql_ref.shape[-1])
    q_rope = qr_ref[...].reshape(rows, qr_ref.shape[-1])
    s_ref[...] = _dot_nt(klat, q_lat) + _dot_nt(kr, q_rope)
    _mla_softmax(masked, q_pos0, k_pos0, m_ref, l_ref, s_ref, p_ref, a_ref, tq, tk)
    acc_ref[...] = a_ref[...] * acc_ref[...] + _dot_tn(klat, p_ref[...])


def _mla_latent_finalize(wuv_ref, o_ref, l_ref, acc_ref, tq):
    inv = 1.0 / l_ref[...]
    for h in range(MLA_HEADS):
        cols = slice(h * tq, (h + 1) * tq)
        ctx_t = (acc_ref[:, cols] * inv[:, cols]).astype(BF16)
        o_ref[:, h * V_DIM:(h + 1) * V_DIM] = _dot_tn(ctx_t, wuv_ref[h]).astype(BF16)


def _mla_heads_tile(qn_ref, qr_ref, kn_ref, kr, v_ref, masked, q_pos0, k_pos0,
                    m_ref, l_ref, acc_ref, s_ref, p_ref, a_ref, tq, tk):
    for h in range(MLA_HEADS):
        cols = slice(h * tq, (h + 1) * tq)
        s_ref[:, cols] = (_dot_nt(kn_ref[:, h * NOPE_DIM:(h + 1) * NOPE_DIM], qn_ref[h])
                          + _dot_nt(kr, qr_ref[h]))
    _mla_softmax(masked, q_pos0, k_pos0, m_ref, l_ref, s_ref, p_ref, a_ref, tq, tk)
    for h in range(MLA_HEADS):
        cols = slice(h * tq, (h + 1) * tq)
        pv = _dot_tn(v_ref[:, h * V_DIM:(h + 1) * V_DIM], p_ref[:, cols])
        acc_ref[h] = a_ref[:, cols] * acc_ref[h] + pv


def _mla_heads_finalize(o_ref, l_ref, acc_ref, tq):
    inv = 1.0 / l_ref[...]
    for h in range(MLA_HEADS):
        cols = slice(h * tq, (h + 1) * tq)
        o_ref[:, h * V_DIM:(h + 1) * V_DIM] = (acc_ref[h] * inv[:, cols]).T.astype(BF16)


def _mla_prompt_kernel(qb_ref, kb_ref, qp_ref, kp_ref, fl_ref,
                       qn_ref, qr_ref, kn_ref, kr_ref, v_ref, o_ref,
                       m_ref, l_ref, acc_ref, s_ref, p_ref, a_ref, *, tq, tk):
    n = pl.program_id(0)
    scratch = (m_ref, l_ref, acc_ref, s_ref, p_ref, a_ref)

    @pl.when(fl_ref[n] % 2 == 1)
    def _():
        _mla_init(m_ref, l_ref, acc_ref)

    @pl.when(fl_ref[n] < 2)
    def _():
        _mla_heads_tile(qn_ref, qr_ref, kn_ref, kr_ref[...], v_ref, False, 0, 0, *scratch, tq, tk)

    @pl.when(fl_ref[n] >= 2)
    def _():
        _mla_heads_tile(qn_ref, qr_ref, kn_ref, kr_ref[...], v_ref, True, qp_ref[n], kp_ref[n],
                        *scratch, tq, tk)
        _mla_heads_finalize(o_ref, l_ref, acc_ref, tq)


def _mla_prompt(q_nope, q_rope, k_nope, k_rope, v, batch, seq):
    h = q_nope.shape[0]
    width = k_nope.shape[1]
    tq = _pick_tile(seq, MLA_Q_TILE, CHUNK)
    tk = _pick_tile(seq, MLA_K_TILE, CHUNK)
    nq, nk = seq // tq, seq // tk
    qb, kb, qp, kp, fl = [], [], [], [], []
    for b in range(batch):
        for qi in range(nq):
            last = ((qi + 1) * tq - 1) // tk
            for kj in range(last + 1):
                qb.append(b * nq + qi)
                kb.append(b * nk + kj)
                qp.append(qi * tq)
                kp.append(kj * tk)
                fl.append((1 if kj == 0 else 0) + (2 if kj == last else 0))
    tabs = [jnp.asarray(np.asarray(a, np.int32)) for a in (qb, kb, qp, kp, fl)]
    rows = h * tq
    grid_spec = pltpu.PrefetchScalarGridSpec(
        num_scalar_prefetch=5,
        grid=(len(qb),),
        in_specs=[
            pl.BlockSpec((h, tq, NOPE_DIM), lambda n, qb, kb, qp, kp, fl: (0, qb[n], 0)),
            pl.BlockSpec((h, tq, ROPE_DIM), lambda n, qb, kb, qp, kp, fl: (0, qb[n], 0)),
            pl.BlockSpec((tk, width), lambda n, qb, kb, qp, kp, fl: (kb[n], 0)),
            pl.BlockSpec((tk, ROPE_DIM), lambda n, qb, kb, qp, kp, fl: (kb[n], 0)),
            pl.BlockSpec((tk, h * V_DIM), lambda n, qb, kb, qp, kp, fl: (kb[n], 0)),
        ],
        out_specs=pl.BlockSpec((tq, h * V_DIM), lambda n, qb, kb, qp, kp, fl: (qb[n], 0)),
        scratch_shapes=[pltpu.VMEM((1, rows), F32), pltpu.VMEM((1, rows), F32),
                        pltpu.VMEM((h, V_DIM, tq), F32), pltpu.VMEM((tk, rows), F32),
                        pltpu.VMEM((tk, rows), BF16), pltpu.VMEM((1, rows), F32)],
    )
    return pl.pallas_call(
        functools.partial(_mla_prompt_kernel, tq=tq, tk=tk),
        grid_spec=grid_spec,
        out_shape=jax.ShapeDtypeStruct((batch * seq, h * V_DIM), BF16),
        compiler_params=_cparams(("arbitrary",)),
        name="mla_prompt",
    )(*tabs, q_nope, q_rope, k_nope, k_rope, v)


def _mla_sample_kernel(ql_ref, qr_ref, pl_ref, pr_ref, nl_ref, nr_ref, wuv_ref, o_ref,
                       m_ref, l_ref, acc_ref, a_ref, s_ref, p_ref, sn_ref, pn_ref,
                       *, tq, tk, n_past, n_kt):
    j = pl.program_id(1)

    @pl.when(j == 0)
    def _():
        _mla_init(m_ref, l_ref, acc_ref)

    @pl.when(j < n_kt)
    def _():
        _mla_latent_tile(ql_ref, qr_ref, pl_ref[...].astype(BF16), pr_ref[...].astype(BF16),
                         False, 0, 0, m_ref, l_ref, acc_ref, s_ref, p_ref, a_ref, tq, tk)

    @pl.when(j == n_kt)
    def _():
        _mla_latent_tile(ql_ref, qr_ref, nl_ref[...], nr_ref[...], True, n_past, n_past,
                         m_ref, l_ref, acc_ref, sn_ref, pn_ref, a_ref, tq, tq)
        _mla_latent_finalize(wuv_ref, o_ref, l_ref, acc_ref, tq)


def _mla_sample(q_lat, q_rope, past_lat, past_kr, layer, k_lat, k_rope, w_uv,
                row0, dec_batch, dec_seq):
    h, _, c = q_lat.shape
    n_past = past_lat.shape[2]
    tk = _pick_tile(n_past, MLA_K_TILE, 128)
    n_kt = n_past // tk
    blk0 = row0 // dec_seq
    rows = h * dec_seq
    past_idx = lambda b, j: (layer, b, jnp.minimum(j, n_kt - 1), 0)
    return pl.pallas_call(
        functools.partial(_mla_sample_kernel, tq=dec_seq, tk=tk, n_past=n_past, n_kt=n_kt),
        grid=(dec_batch, n_kt + 1),
        in_specs=[
            pl.BlockSpec((h, dec_seq, c), lambda b, j: (0, b, 0)),
            pl.BlockSpec((h, dec_seq, ROPE_DIM), lambda b, j: (0, blk0 + b, 0)),
            pl.BlockSpec((None, None, tk, c), past_idx),
            pl.BlockSpec((None, None, tk, ROPE_DIM), past_idx),
            pl.BlockSpec((dec_seq, c), lambda b, j: (blk0 + b, 0)),
            pl.BlockSpec((dec_seq, ROPE_DIM), lambda b, j: (blk0 + b, 0)),
            pl.BlockSpec((h, c, V_DIM), lambda b, j: (0, 0, 0)),
        ],
        out_specs=pl.BlockSpec((dec_seq, h * V_DIM), lambda b, j: (b, 0)),
        out_shape=jax.ShapeDtypeStruct((dec_batch * dec_seq, h * V_DIM), BF16),
        scratch_shapes=[pltpu.VMEM((1, rows), F32), pltpu.VMEM((1, rows), F32),
                        pltpu.VMEM((c, rows), F32), pltpu.VMEM((1, rows), F32),
                        pltpu.VMEM((tk, rows), F32), pltpu.VMEM((tk, rows), BF16),
                        pltpu.VMEM((dec_seq, rows), F32), pltpu.VMEM((dec_seq, rows), BF16)],
        compiler_params=_cparams(("parallel", "arbitrary")),
        name="mla_sample",
    )(q_lat, q_rope, past_lat, past_kr, k_lat, k_rope, w_uv)


def _scan_matrix(sub):
    j = np.arange(sub)[:, None]
    s = np.arange(sub)[None, :]
    u = (j > s).astype(np.float32)
    return jnp.asarray(np.concatenate([u, u], axis=0), dtype=BF16)


_SbChunk = collections.namedtuple("_SbChunk", "q k v before carry_idx acc_idx")


def _sb_scores(qc, kc, before):
    z = _dot_nt(qc, kc)
    softplus = jnp.log(1.0 + jnp.exp2(-jnp.abs(z))) * LOG2E
    log_beta = jnp.minimum(z, 0.0) - softplus
    log_rest = log_beta - z
    if before is not None:
        log_rest = jnp.where(before, log_rest, 0.0)
    hi = log_rest.astype(BF16)
    lo = (log_rest - hi.astype(F32)).astype(BF16)
    return (log_beta, jnp.concatenate([hi, lo], axis=-1),
            jnp.sum(log_rest, axis=-1, keepdims=True))


def _sb_weights(log_beta, hi_lo, u2, carry, before):
    a = jnp.exp2(log_beta + _dot(hi_lo, u2) + carry)
    if before is not None:
        a = jnp.where(before, a, 0.0)
    return a.astype(BF16)


def _sb_pipeline(chunks, u2, acc_ref, carry_ref):
    scored = None
    weighted = None
    for j in range(len(chunks) + 2):
        new_scored = None
        if j < len(chunks):
            ch = chunks[j]
            new_scored = (ch,) + _sb_scores(ch.q(), ch.k(), ch.before)
        new_weighted = None
        if scored is not None:
            ch, log_beta, hi_lo, row_sum = scored
            carry = carry_ref[ch.carry_idx]
            new_weighted = (ch, _sb_weights(log_beta, hi_lo, u2, carry, ch.before))
            carry_ref[ch.carry_idx] = carry + row_sum
        if weighted is not None:
            ch, a = weighted
            acc_ref[ch.acc_idx] += _dot(a, ch.v())
        scored, weighted = new_scored, new_weighted


def _sb_tile(q_ref, k_ref, v_ref, u2, acc_ref, carry_ref, diagonal, tq, tk, sub, rq):
    chunks = []
    for c in reversed(range(tk // sub)):
        k0, k1 = c * sub, (c + 1) * sub
        for r in range(tq // rq):
            r0, r1 = r * rq, (r + 1) * rq
            before = None
            if diagonal:
                if k0 >= r1 - 1:
                    continue
                if k1 > r0:
                    before = ((k0 + lax.broadcasted_iota(jnp.int32, (1, sub), 1))
                              < (r0 + lax.broadcasted_iota(jnp.int32, (rq, 1), 0)))
            for h in range(SB_HEADS):
                cols = slice(h * SB_DIM, (h + 1) * SB_DIM)
                rows, keys = slice(r0, r1), slice(k0, k1)
                chunks.append(_SbChunk(
                    q=functools.partial(lambda rr, cc: q_ref[rr, cc], rows, cols),
                    k=functools.partial(lambda kk, cc: k_ref[kk, cc], keys, cols),
                    v=functools.partial(lambda kk, cc: v_ref[kk, cc], keys, cols),
                    before=before, carry_idx=(h, rows, slice(None)), acc_idx=(rows, cols)))
    _sb_pipeline(chunks, u2, acc_ref, carry_ref)


def _sb_prompt_kernel(q_ref, k_ref, v_ref, u_ref, k_hbm, v_hbm, o_ref,
                      acc_ref, carry_ref, kbuf, vbuf, sem, *, nq, tq, sub, rq):
    g = pl.program_id(0)
    n_old = lax.rem(g, nq) * (tq // sub)

    def tile_copies(i, slot):
        start = g * tq - (i + 1) * sub
        return (pltpu.make_async_copy(k_hbm.at[pl.ds(start, sub), :], kbuf.at[slot], sem.at[0, slot]),
                pltpu.make_async_copy(v_hbm.at[pl.ds(start, sub), :], vbuf.at[slot], sem.at[1, slot]))

    def start_tile(i, slot):
        for copy in tile_copies(i, slot):
            copy.start()

    def wait_tile(i, slot):
        for copy in tile_copies(i, slot):
            copy.wait()

    def any_alive():
        return (jnp.max(carry_ref[...]) > SB_DEAD_LOG2).astype(jnp.int32)

    @pl.when(n_old > 0)
    def _():
        start_tile(0, 0)

    acc_ref[...] = jnp.zeros(acc_ref.shape, F32)
    carry_ref[...] = jnp.zeros(carry_ref.shape, F32)
    _sb_tile(q_ref, k_ref, v_ref, u_ref[...], acc_ref, carry_ref, True, tq, tq, sub, rq)

    def tile_step(state):
        i, _ = state
        slot = lax.rem(i, 2)
        wait_tile(i, slot)

        @pl.when(i + 1 < n_old)
        def _():
            start_tile(i + 1, 1 - slot)

        for r in range(tq // rq):
            rows = slice(r * rq, (r + 1) * rq)

            @pl.when(jnp.max(carry_ref[:, rows, :]) > SB_DEAD_LOG2)
            def _(rows=rows):
                chunks = []
                for h in range(SB_HEADS):
                    cols = slice(h * SB_DIM, (h + 1) * SB_DIM)
                    chunks.append(_SbChunk(
                        q=functools.partial(lambda rr, cc: q_ref[rr, cc], rows, cols),
                        k=functools.partial(lambda cc: kbuf[slot, :, cc], cols),
                        v=functools.partial(lambda cc: vbuf[slot, :, cc], cols),
                        before=None, carry_idx=(h, rows, slice(None)), acc_idx=(rows, cols)))
                _sb_pipeline(chunks, u_ref[...], acc_ref, carry_ref)
        return i + 1, any_alive()

    tiles_done, _ = lax.while_loop(
        lambda state: jnp.logical_and(state[0] < n_old, state[1] == 1),
        tile_step, (jnp.int32(0), any_alive()))

    @pl.when(tiles_done < n_old)
    def _():
        wait_tile(tiles_done, lax.rem(tiles_done, 2))

    o_ref[...] = acc_ref[...].astype(BF16)


def _sb_prompt(q, k, v, batch, seq):
    width = q.shape[1]
    tq = _pick_tile(seq, SB_TILE, 128)
    sub = _pick_tile(tq, SB_SUB, 128)
    rq = _pick_tile(tq, SB_ROWS, 16)
    nq = seq // tq
    rows = pl.BlockSpec((tq, width), lambda g: (g, 0))
    in_hbm = pl.BlockSpec(memory_space=pl.ANY)
    return pl.pallas_call(
        functools.partial(_sb_prompt_kernel, nq=nq, tq=tq, sub=sub, rq=rq),
        grid=(batch * nq,),
        in_specs=[rows, rows, rows, pl.BlockSpec((2 * sub, sub), lambda g: (0, 0)),
                  in_hbm, in_hbm],
        out_specs=rows,
        out_shape=jax.ShapeDtypeStruct((batch * seq, width), BF16),
        scratch_shapes=[pltpu.VMEM((tq, width), F32), pltpu.VMEM((SB_HEADS, tq, 1), F32),
                        pltpu.VMEM((2, sub, width), BF16), pltpu.VMEM((2, sub, width), BF16),
                        pltpu.SemaphoreType.DMA((2, 2))],
        compiler_params=_cparams(("arbitrary",)),
        name="sb_prompt",
    )(q, k, v, _scan_matrix(sub), k, v)


def _sb_sample_kernel(q_ref, nk_ref, nv_ref, un_ref, up_ref, pk_hbm, pv_hbm, o_ref,
                      acc_ref, carry_ref, kbuf, vbuf, sem, *, layer, tq, tk, n_kt):
    b = pl.program_id(0)

    def tile_copies(i, slot):
        start = (n_kt - 1 - i) * tk
        copies = []
        for h in range(SB_HEADS):
            copies.append(pltpu.make_async_copy(
                pk_hbm.at[layer, b, pl.ds(start, tk), h, :], kbuf.at[slot, h], sem.at[0, slot]))
            copies.append(pltpu.make_async_copy(
                pv_hbm.at[layer, b, pl.ds(start, tk), h, :], vbuf.at[slot, h], sem.at[1, slot]))
        return copies

    def start_tile(i, slot):
        for copy in tile_copies(i, slot):
            copy.start()

    def wait_tile(i, slot):
        for copy in tile_copies(i, slot):
            copy.wait()

    def alive():
        return (jnp.max(carry_ref[...]) > SB_DEAD_LOG2).astype(jnp.int32)

    start_tile(0, 0)

    acc_ref[...] = jnp.zeros(acc_ref.shape, F32)
    carry_ref[...] = jnp.zeros(carry_ref.shape, F32)
    before = (lax.broadcasted_iota(jnp.int32, (1, tq), 1)
              < lax.broadcasted_iota(jnp.int32, (tq, 1), 0))
    chunks = []
    for h in range(SB_HEADS):
        cols = slice(h * SB_DIM, (h + 1) * SB_DIM)
        chunks.append(_SbChunk(
            q=functools.partial(lambda cc: q_ref[:, cc], cols),
            k=functools.partial(lambda cc: nk_ref[:, cc], cols),
            v=functools.partial(lambda cc: nv_ref[:, cc], cols),
            before=before, carry_idx=(h, slice(None), slice(None)), acc_idx=(slice(None), cols)))
    _sb_pipeline(chunks, un_ref[...], acc_ref, carry_ref)

    def tile_step(state):
        i, _ = state
        slot = lax.rem(i, 2)
        wait_tile(i, slot)

        @pl.when(i + 1 < n_kt)
        def _():
            start_tile(i + 1, 1 - slot)

        chunks = []
        for h in range(SB_HEADS):
            cols = slice(h * SB_DIM, (h + 1) * SB_DIM)
            chunks.append(_SbChunk(
                q=functools.partial(lambda cc: q_ref[:, cc], cols),
                k=functools.partial(lambda hh: kbuf[slot, hh].astype(BF16), h),
                v=functools.partial(lambda hh: vbuf[slot, hh].astype(BF16), h),
                before=None, carry_idx=(h, slice(None), slice(None)), acc_idx=(slice(None), cols)))
        _sb_pipeline(chunks, up_ref[...], acc_ref, carry_ref)
        return i + 1, alive()

    tiles_done, _ = lax.while_loop(
        lambda state: jnp.logical_and(state[0] < n_kt, state[1] == 1),
        tile_step, (jnp.int32(0), alive()))

    @pl.when(tiles_done < n_kt)
    def _():
        wait_tile(tiles_done, lax.rem(tiles_done, 2))

    o_ref[...] = acc_ref[...].astype(BF16)


def _sb_sample(q, past_k, past_v, layer, k, v, row0, dec_batch, dec_seq):
    width = q.shape[1]
    n_past = past_k.shape[2]
    tk = _pick_tile(n_past, SB_SUB, 128)
    n_kt = n_past // tk
    blk0 = row0 // dec_seq
    new_rows = pl.BlockSpec((dec_seq, width), lambda b: (blk0 + b, 0))
    in_hbm = pl.BlockSpec(memory_space=pl.ANY)
    return pl.pallas_call(
        functools.partial(_sb_sample_kernel, layer=layer, tq=dec_seq, tk=tk, n_kt=n_kt),
        grid=(dec_batch,),
        in_specs=[
            new_rows, new_rows, new_rows,
            pl.BlockSpec((2 * dec_seq, dec_seq), lambda b: (0, 0)),
            pl.BlockSpec((2 * tk, tk), lambda b: (0, 0)),
            in_hbm, in_hbm,
        ],
        out_specs=pl.BlockSpec((dec_seq, width), lambda b: (b, 0)),
        out_shape=jax.ShapeDtypeStruct((dec_batch * dec_seq, width), BF16),
        scratch_shapes=[pltpu.VMEM((dec_seq, width), F32),
                        pltpu.VMEM((SB_HEADS, dec_seq, 1), F32),
                        pltpu.VMEM((2, SB_HEADS, tk, SB_DIM), F32),
                        pltpu.VMEM((2, SB_HEADS, tk, SB_DIM), F32),
                        pltpu.SemaphoreType.DMA((2, 2))],
        compiler_params=_cparams(("arbitrary",)),
        name="sb_sample",
    )(q, k, v, _scan_matrix(dec_seq), _scan_matrix(tk), past_k, past_v)


def _merge_kernel(oap_ref, oas_ref, obp_ref, obs_ref, wa_ref, wb_ref, g_ref, o_ref,
                  *, n_p, tn):
    i = pl.program_id(0)
    o_a = _load_split(i, n_p, oap_ref, oas_ref)
    o_b = _load_split(i, n_p, obp_ref, obs_ref)
    n = o_ref.shape[1]
    for c in range(n // tn):
        cols = slice(c * tn, (c + 1) * tn)
        a = _dot(o_a, wa_ref[:, cols])
        b = _dot(o_b, wb_ref[:, cols])
        g_a = g_ref[:, cols].astype(F32)
        g_b = g_ref[:, n + c * tn:n + (c + 1) * tn].astype(F32)
        o_ref[:, cols] = (g_a * a + g_b * b).astype(BF16)


def _merge(oa_p, oa_s, ob_p, ob_s, w_a, w_b, gates, tm):
    da = oa_p.shape[1]
    db = ob_p.shape[1]
    t = oa_p.shape[0] + oa_s.shape[0]
    n_p = oa_p.shape[0] // tm
    n = w_a.shape[1]
    tn = _pick_tile(n, COL_TILE, 128)
    return pl.pallas_call(
        functools.partial(_merge_kernel, n_p=n_p, tn=tn),
        grid=(t // tm,),
        in_specs=[*_split_specs(tm, da, n_p),
                  *_split_specs(tm, db, n_p),
                  _resident((da, n)), _resident((db, n)),
                  pl.BlockSpec((tm, 2 * n), lambda i: (i, 0))],
        out_specs=pl.BlockSpec((tm, n), lambda i: (i, 0)),
        out_shape=jax.ShapeDtypeStruct((t, n), BF16),
        compiler_params=_cparams(("parallel",)),
        name="merge",
    )(oa_p, oa_s, ob_p, ob_s, w_a, w_b, gates)


def _res_ln_finalize(z_ref, g_ref, b_ref, o_ref, ob_ref):
    n_tiles, _, tn = z_ref.shape
    width = n_tiles * tn
    total = z_ref[0].sum(axis=-1, keepdims=True)
    for c in range(1, n_tiles):
        total += z_ref[c].sum(axis=-1, keepdims=True)
    mu = total / width
    sq = jnp.square(z_ref[0] - mu).sum(axis=-1, keepdims=True)
    for c in range(1, n_tiles):
        sq += jnp.square(z_ref[c] - mu).sum(axis=-1, keepdims=True)
    rstd = lax.rsqrt(sq / width + LN_EPS)
    for c in range(n_tiles):
        cols = slice(c * tn, (c + 1) * tn)
        y = (z_ref[c] - mu) * rstd * g_ref[:, cols] + b_ref[:, cols]
        o_ref[:, cols] = y
        if ob_ref is not None:
            ob_ref[:, cols] = y.astype(BF16)


def _out_ln_kernel(m_ref, w_ref, x_ref, g_ref, b_ref, o_ref, ob_ref, z_ref, *, alpha):
    m = m_ref[...]
    n_tiles, _, tn = z_ref.shape
    for c in range(n_tiles):
        cols = slice(c * tn, (c + 1) * tn)
        z_ref[c] = alpha * x_ref[:, cols] + _dot(m, w_ref[:, cols])
    _res_ln_finalize(z_ref, g_ref, b_ref, o_ref, ob_ref)


def _out_ln(merged, w, x, g, b, alpha, tm):
    t, d = x.shape
    k = merged.shape[1]
    tn = _pick_tile(d, COL_TILE, 128)
    full_row = pl.BlockSpec((tm, d), lambda i: (i, 0))
    vec = pl.BlockSpec((1, d), lambda i: (0, 0))
    return pl.pallas_call(
        functools.partial(_out_ln_kernel, alpha=alpha),
        grid=(t // tm,),
        in_specs=[pl.BlockSpec((tm, k), lambda i: (i, 0)), _resident((k, d)), full_row, vec, vec],
        out_specs=[full_row, full_row],
        out_shape=[jax.ShapeDtypeStruct((t, d), F32), jax.ShapeDtypeStruct((t, d), BF16)],
        scratch_shapes=[pltpu.VMEM((d // tn, tm, tn), F32)],
        compiler_params=_cparams(("parallel",)),
        name="out_ln",
    )(merged, w, x, g.reshape(1, d), b.reshape(1, d))


def _ffn_ln_kernel(xb_ref, wg_ref, wu_ref, wd_ref, x_ref, g_ref, b_ref, o_ref, ob_ref,
                   acc_ref, *, alpha):
    f = pl.program_id(1)

    @pl.when(f == 0)
    def _():
        acc_ref[...] = jnp.zeros(acc_ref.shape, F32)

    xb = xb_ref[...]
    gate = _dot(xb, wg_ref[...])
    up = _dot(xb, wu_ref[...])
    hidden = (gate * _sigmoid(gate) * up).astype(BF16)
    acc_ref[...] += _dot(hidden, wd_ref[...])

    @pl.when(f == pl.num_programs(1) - 1)
    def _():
        y = _layer_norm_rows(alpha * x_ref[...] + acc_ref[...], g_ref[...], b_ref[...])
        o_ref[...] = y
        ob_ref[...] = y.astype(BF16)


def _ffn_ln(xb, w_gu, w_down, x, g, b, alpha, tm):
    t, d = x.shape
    ff = w_down.shape[0]
    tf = _pick_tile(ff, COL_TILE, 128)
    nf = ff // tf
    full_row = pl.BlockSpec((tm, d), lambda i, f: (i, 0))
    vec = pl.BlockSpec((1, d), lambda i, f: (0, 0))
    return pl.pallas_call(
        functools.partial(_ffn_ln_kernel, alpha=alpha),
        grid=(t // tm, nf),
        in_specs=[full_row,
                  pl.BlockSpec((d, tf), lambda i, f: (0, f)),
                  pl.BlockSpec((d, tf), lambda i, f: (0, f + nf)),
                  pl.BlockSpec((tf, d), lambda i, f: (f, 0)),
                  full_row, vec, vec],
        out_specs=[full_row, full_row],
        out_shape=[jax.ShapeDtypeStruct((t, d), F32), jax.ShapeDtypeStruct((t, d), BF16)],
        scratch_shapes=[pltpu.VMEM((tm, d), F32)],
        compiler_params=_cparams(("parallel", "arbitrary")),
        name="ffn_ln",
    )(xb, w_gu, w_gu, w_down, x, g.reshape(1, d), b.reshape(1, d))


def _ple_ln_kernel(xb_ref, wg_ref, bg_ref, pp_ref, ps_ref, wp_ref, x_ref, g_ref, b_ref,
                   o1_ref, o2_ref, z_ref, *, alpha, n_p, last):
    i = pl.program_id(0)
    xb = xb_ref[...]
    p = _load_split(i, n_p, pp_ref, ps_ref).astype(BF16)
    n_tiles, _, tn = z_ref.shape
    for c in range(n_tiles):
        cols = slice(c * tn, (c + 1) * tn)
        gate = _sigmoid(_dot(xb, wg_ref[:, cols]) + bg_ref[:, cols])
        z_ref[c] = alpha * x_ref[:, cols] + gate * _dot(p, wp_ref[:, cols])

    if not last:
        _res_ln_finalize(z_ref, g_ref, b_ref, o1_ref, o2_ref)
    else:
        @pl.when(i < n_p)
        def _():
            _res_ln_finalize(z_ref, g_ref, b_ref, o1_ref, None)

        @pl.when(i >= n_p)
        def _():
            _res_ln_finalize(z_ref, g_ref, b_ref, o2_ref, None)


def _ple_ln(xb, w_gate, b_gate, p_p, p_s, layer, w_proj, x, g, b, alpha, tm, last):
    t, d = x.shape
    pd = p_p.shape[2]
    tp = p_p.shape[1]
    n_p = tp // tm
    tn = _pick_tile(d, COL_TILE, 128)
    full_row = pl.BlockSpec((tm, d), lambda i: (i, 0))
    vec = pl.BlockSpec((1, d), lambda i: (0, 0))
    if last:
        out_specs = list(_split_specs(tm, d, n_p))
        out_shape = [jax.ShapeDtypeStruct((tp, d), F32), jax.ShapeDtypeStruct((t - tp, d), F32)]
    else:
        out_specs = [full_row, full_row]
        out_shape = [jax.ShapeDtypeStruct((t, d), F32), jax.ShapeDtypeStruct((t, d), BF16)]
    return pl.pallas_call(
        functools.partial(_ple_ln_kernel, alpha=alpha, n_p=n_p, last=last),
        grid=(t // tm,),
        in_specs=[full_row, _resident((d, d)), vec,
                  pl.BlockSpec((None, tm, pd), lambda i: (layer, jnp.minimum(i, n_p - 1), 0)),
                  pl.BlockSpec((None, tm, pd), lambda i: (layer, jnp.maximum(i - n_p, 0), 0)),
                  _resident((pd, d)), full_row, vec, vec],
        out_specs=out_specs,
        out_shape=out_shape,
        scratch_shapes=[pltpu.VMEM((d // tn, tm, tn), F32)],
        compiler_params=_cparams(("arbitrary",)),
        name="ple_ln",
    )(xb, w_gate, b_gate.reshape(1, d), p_p, p_s, w_proj, x, g.reshape(1, d), b.reshape(1, d))


def _cast_kernel(w_ref, *o_refs, bounds):
    w = w_ref[...]
    for o_ref, (lo, hi) in zip(o_refs, bounds):
        o_ref[...] = w[:, lo:hi].astype(BF16)


def _cast_weight(w, layer, col_groups=None):
    _, r, n = w.shape
    bounds = tuple(col_groups) if col_groups else ((0, n),)
    br = _pick_tile(r, max(16, CAST_BLOCK_BYTES // (4 * n)), 16)
    return pl.pallas_call(
        functools.partial(_cast_kernel, bounds=bounds),
        grid=(r // br,),
        in_specs=[pl.BlockSpec((None, br, n), lambda i: (layer, i, 0))],
        out_specs=[pl.BlockSpec((br, hi - lo), lambda i: (i, 0)) for lo, hi in bounds],
        out_shape=[jax.ShapeDtypeStruct((r, hi - lo), BF16) for lo, hi in bounds],
        compiler_params=_cparams(("parallel",)),
        name="cast_weight",
    )(w)


def _rope_tables(pos):
    half = ROPE_DIM // 2
    inv_freq = 1.0 / (ROPE_THETA ** (jnp.arange(half, dtype=F32) * (2.0 / ROPE_DIM)))
    ang = pos.astype(F32)[:, None] * inv_freq[None, :]
    return jnp.cos(ang), jnp.sin(ang)


def kernel(x_prompt, x_sample, cache_mla_latent, cache_mla_krope, cache_sb_k, cache_sb_v,
           p_prompt, p_sample, ln_in_g, ln_in_b, w_in, b_gate, q_a_norm_g, w_q_b,
           kv_a_norm_g, w_kv_b, w_branch_a, w_branch_b, w_out, ln1_g, ln1_b,
           w_ffn_gu, w_ffn_down, ln2_g, ln2_b, w_ple_gate, b_ple_gate, w_ple_proj,
           ln3_g, ln3_b):
    batch, seq, d = x_prompt.shape
    dec_batch, dec_seq, _ = x_sample.shape
    depth = w_in.shape[0]
    n_past = cache_mla_latent.shape[2]
    tp = batch * seq
    ts = dec_batch * dec_seq
    sb_width = SB_HEADS * SB_DIM
    alpha = (2 * depth) ** 0.25

    pos = jnp.concatenate([jnp.tile(jnp.arange(seq, dtype=jnp.int32), batch),
                           jnp.tile(n_past + jnp.arange(dec_seq, dtype=jnp.int32), dec_batch)])
    cos, sin = _rope_tables(pos)

    tm = _pick_tile(math.gcd(tp, ts), ROW_TILE, 16)
    p_p = p_prompt.reshape(depth, tp, -1)
    p_s = p_sample.reshape(depth, ts, -1)

    x, xb = _ln_in(x_prompt.reshape(tp, d), x_sample.reshape(ts, d), ln_in_g, ln_in_b, tm)

    splits = np.cumsum([0, Q_LORA, KV_LORA, ROPE_DIM, sb_width, sb_width, sb_width, 2 * d])
    def cache_arrays(widths):
        return [jnp.zeros((depth, n, w), F32) for w in widths for n in (tp, ts)]

    kv_cache = cache_arrays((KV_LORA, ROPE_DIM))
    sb_cache = cache_arrays((sb_width, sb_width))
    for l in range(depth):
        w_qa, w_lat, w_kr, w_sq, w_sk, w_sv, w_gates = _cast_weight(
            w_in, l, [(int(splits[i]), int(splits[i + 1])) for i in range(7)])
        (wb_a,) = _cast_weight(w_branch_a, l)
        (wb_b,) = _cast_weight(w_branch_b, l)
        (wb_out,) = _cast_weight(w_out, l)
        (wb_gu,) = _cast_weight(w_ffn_gu, l)
        (wb_down,) = _cast_weight(w_ffn_down, l)
        (wb_pg,) = _cast_weight(w_ple_gate, l)
        (wb_pp,) = _cast_weight(w_ple_proj, l)
        wq = w_q_b[l].reshape(Q_LORA, MLA_HEADS, NOPE_DIM + ROPE_DIM)
        wq_nope = wq[..., :NOPE_DIM].reshape(Q_LORA, MLA_HEADS * NOPE_DIM).astype(BF16)
        wq_rope = wq[..., NOPE_DIM:].reshape(Q_LORA, MLA_HEADS * ROPE_DIM).astype(BF16)
        w_kv = w_kv_b[l].reshape(KV_LORA, MLA_HEADS, NOPE_DIM + V_DIM)
        wuk_h = w_kv[..., :NOPE_DIM].transpose(1, 2, 0).astype(BF16)
        wuv_h = w_kv[..., NOPE_DIM:].transpose(1, 0, 2).astype(BF16)
        wuk_all = w_kv[..., :NOPE_DIM].reshape(KV_LORA, MLA_HEADS * NOPE_DIM).astype(BF16)
        wuv_all = w_kv[..., NOPE_DIM:].reshape(KV_LORA, MLA_HEADS * V_DIM).astype(BF16)

        qa_n = _proj_qa(xb, w_qa, q_a_norm_g[l], tm)
        *kv_cache, lat_b, kr_b = _proj_kv(xb, w_lat, w_kr, kv_a_norm_g[l], cos, sin,
                                          tm, tp, l, depth, kv_cache)
        *sb_cache, sbq, sbk_b, sbv_b = _proj_sb(xb, w_sq, w_sk, w_sv, tm, tp, l, depth, sb_cache)
        gates = _proj_gates(xb, w_gates, b_gate[l], tm)

        q_nope, q_lat, q_rope = _q_heads(qa_n, wq_nope, wq_rope, wuk_h, cos, sin, tm, tp)
        k_nope, v_heads = _kv_heads(lat_b, wuk_all, wuv_all, tm, tp)
        oa_p = _mla_prompt(q_nope, q_rope, k_nope, kr_b, v_heads, batch, seq)
        oa_s = _mla_sample(q_lat, q_rope, cache_mla_latent, cache_mla_krope, l, lat_b, kr_b,
                           wuv_h, tp, dec_batch, dec_seq)
        ob_p = _sb_prompt(sbq, sbk_b, sbv_b, batch, seq)
        ob_s = _sb_sample(sbq, cache_sb_k, cache_sb_v, l, sbk_b, sbv_b, tp, dec_batch, dec_seq)

        merged = _merge(oa_p, oa_s, ob_p, ob_s, wb_a, wb_b, gates, tm)
        x, xb = _out_ln(merged, wb_out, x, ln1_g[l], ln1_b[l], alpha, tm)
        x, xb = _ffn_ln(xb, wb_gu, wb_down, x, ln2_g[l], ln2_b[l], alpha, tm)
        x, xb = _ple_ln(xb, wb_pg, b_ple_gate[l], p_p, p_s, l, wb_pp, x, ln3_g[l], ln3_b[l],
                        alpha, tm, last=(l == depth - 1))

    lat_p, lat_s, kr_p, kr_s = kv_cache
    k_p, k_s, v_p, v_s = sb_cache
    head_dims = (SB_HEADS, SB_DIM)
    return (x.reshape(batch, seq, d), xb.reshape(dec_batch, dec_seq, d),
            lat_p.reshape(depth, batch, seq, KV_LORA), kr_p.reshape(depth, batch, seq, ROPE_DIM),
            k_p.reshape(depth, batch, seq, *head_dims), v_p.reshape(depth, batch, seq, *head_dims),
            lat_s.reshape(depth, dec_batch, dec_seq, KV_LORA),
            kr_s.reshape(depth, dec_batch, dec_seq, ROPE_DIM),
            k_s.reshape(depth, dec_batch, dec_seq, *head_dims),
            v_s.reshape(depth, dec_batch, dec_seq, *head_dims))
```

```python
import collections
import functools
import math

import numpy as np
import jax
import jax.numpy as jnp
from jax import lax
from jax.experimental import pallas as pl
from jax.experimental.pallas import tpu as pltpu

CHUNK = 64
MLA_HEADS = 16
Q_LORA = 512
KV_LORA = 512
NOPE_DIM = 128
ROPE_DIM = 64
V_DIM = 128
ROPE_THETA = 10000.0
MLA_SCALE = (NOPE_DIM + ROPE_DIM) ** -0.5
SB_HEADS = 8
SB_DIM = 128
SB_SCALE = SB_DIM ** -0.5
LN_EPS = 1e-5
RMS_EPS = 1e-6
LOG2E = 1.4426950408889634
MLA_QSCALE = MLA_SCALE * LOG2E
MLA_QK_DIM = 256
SB_QSCALE = SB_SCALE * LOG2E
SB_DEAD_LOG2 = -160.0

BF16 = jnp.bfloat16
F32 = jnp.float32

V7X_VMEM_LIMIT_BYTES = 56 * 1024 * 1024
ROW_TILE = 512
COL_TILE = 512
PROJ_COL_TILE = 1024
MLA_Q_TILE = 256
MLA_K_TILE = 512
MLA_SOFTMAX_COLS = 128
SB_TILE = 512
SB_SUB = 256
SB_ROWS = 128
CAST_BLOCK_BYTES = 6 * 1024 * 1024
NEG_BIG = -1e30


def _pick_tile(n, target, mult):
    best = None
    for t in range(mult, min(n, target) + 1, mult):
        if n % t == 0:
            best = t
    if best is None:
        return n
    return best


def _cparams(sem):
    return pltpu.CompilerParams(dimension_semantics=sem,
                                vmem_limit_bytes=V7X_VMEM_LIMIT_BYTES)


def _resident(shape):
    return pl.BlockSpec(shape, lambda *_: (0,) * len(shape), pipeline_mode=pl.Buffered(1))


def _dot(a, b):
    return jnp.dot(a, b, preferred_element_type=F32)


def _dot_nt(a, b):
    return lax.dot_general(a, b, (((1,), (1,)), ((), ())), preferred_element_type=F32)


def _layer_norm_rows(z, g, b):
    mu = jnp.mean(z, axis=-1, keepdims=True)
    zc = z - mu
    var = jnp.mean(zc * zc, axis=-1, keepdims=True)
    return zc * lax.rsqrt(var + LN_EPS) * g + b


def _rms_norm_rows(z, g):
    return z * lax.rsqrt(jnp.mean(z * z, axis=-1, keepdims=True) + RMS_EPS) * g


def _sigmoid(z):
    return 1.0 / (1.0 + jnp.exp(-z))


def _rope_rows(x, cos, sin):
    half = ROPE_DIM // 2
    x1 = x[:, :half]
    x2 = x[:, half:]
    return jnp.concatenate([x1 * cos - x2 * sin, x2 * cos + x1 * sin], axis=-1)


def _split_specs(tm, width, n_p):
    return (pl.BlockSpec((tm, width), lambda i, *_: (jnp.minimum(i, n_p - 1), 0)),
            pl.BlockSpec((tm, width), lambda i, *_: (jnp.maximum(i - n_p, 0), 0)))


def _load_split(i, n_p, p_ref, s_ref):
    return jnp.where(i < n_p, p_ref[...], s_ref[...])


def _store_split(i, n_p, p_ref, s_ref, value):
    @pl.when(i < n_p)
    def _():
        p_ref[...] = value

    @pl.when(i >= n_p)
    def _():
        s_ref[...] = value


def _ln_in_kernel(xp_ref, xs_ref, g_ref, b_ref, o_ref, ob_ref, *, n_p):
    x = _load_split(pl.program_id(0), n_p, xp_ref, xs_ref)
    y = _layer_norm_rows(x, g_ref[...], b_ref[...])
    o_ref[...] = y
    ob_ref[...] = y.astype(BF16)


def _ln_in(x_p, x_s, g, b, tm):
    d = x_p.shape[1]
    t = x_p.shape[0] + x_s.shape[0]
    n_p = x_p.shape[0] // tm
    row = pl.BlockSpec((tm, d), lambda i: (i, 0))
    vec = pl.BlockSpec((1, d), lambda i: (0, 0))
    return pl.pallas_call(
        functools.partial(_ln_in_kernel, n_p=n_p),
        grid=(t // tm,),
        in_specs=[*_split_specs(tm, d, n_p), vec, vec],
        out_specs=[row, row],
        out_shape=[jax.ShapeDtypeStruct((t, d), F32), jax.ShapeDtypeStruct((t, d), BF16)],
        compiler_params=_cparams(("parallel",)),
        name="ln_in",
    )(x_p, x_s, g.reshape(1, d), b.reshape(1, d))


def _proj_qa_kernel(x_ref, w_ref, g_ref, o_ref):
    o_ref[...] = _rms_norm_rows(_dot(x_ref[...], w_ref[...]), g_ref[...]).astype(BF16)


def _proj_kv_kernel(x_ref, wl_ref, wr_ref, g_ref, cos_ref, sin_ref, *refs, n_p):
    latp_ref, lats_ref, krp_ref, krs_ref, latb_ref, krb_ref = refs[-6:]
    i = pl.program_id(0)
    x = x_ref[...]
    lat = _rms_norm_rows(_dot(x, wl_ref[...]), g_ref[...])
    kr = _rope_rows(_dot(x, wr_ref[...]), cos_ref[...], sin_ref[...])
    _store_split(i, n_p, latp_ref, lats_ref, lat)
    _store_split(i, n_p, krp_ref, krs_ref, kr)
    latb_ref[...] = lat.astype(BF16)
    krb_ref[...] = kr.astype(BF16)


def _proj_sb_kernel(x_ref, wq_ref, wk_ref, wv_ref, *refs, n_p):
    kp_ref, ks_ref, vp_ref, vs_ref, q_ref, kb_ref, vb_ref = refs[-7:]
    i = pl.program_id(0)
    x = x_ref[...]
    q_ref[...] = (_dot(x, wq_ref[...]) * SB_QSCALE).astype(BF16)
    k = _dot(x, wk_ref[...])
    _store_split(i, n_p, kp_ref, ks_ref, k)
    kb_ref[...] = k.astype(BF16)
    v = _dot(x, wv_ref[...])
    _store_split(i, n_p, vp_ref, vs_ref, v)
    vb_ref[...] = v.astype(BF16)


def _proj_gate_kernel(x_ref, w_ref, b_ref, o_ref, *, tn):
    x = x_ref[...]
    for c in range(o_ref.shape[1] // tn):
        cols = slice(c * tn, (c + 1) * tn)
        o_ref[:, cols] = _sigmoid(_dot(x, w_ref[:, cols]) + b_ref[:, cols]).astype(BF16)


def _proj_qa(xb, w, g, tm):
    t, d = xb.shape
    n = w.shape[1]
    return pl.pallas_call(
        _proj_qa_kernel,
        grid=(t // tm,),
        in_specs=[pl.BlockSpec((tm, d), lambda i: (i, 0)),
                  pl.BlockSpec((d, n), lambda i: (0, 0)),
                  pl.BlockSpec((1, n), lambda i: (0, 0))],
        out_specs=pl.BlockSpec((tm, n), lambda i: (i, 0)),
        out_shape=jax.ShapeDtypeStruct((t, n), BF16),
        compiler_params=_cparams(("parallel",)),
        name="proj_qa",
    )(xb, w, g.reshape(1, n))


def _cache_outputs(widths, tm, tp, ts, layer, depth, prev, n_inputs):
    n_p = tp // tm
    out_specs, out_shape = [], []
    for w in widths:
        out_specs += [
            pl.BlockSpec((None, tm, w), lambda i: (layer, jnp.minimum(i, n_p - 1), 0)),
            pl.BlockSpec((None, tm, w), lambda i: (layer, jnp.maximum(i - n_p, 0), 0))]
        out_shape += [jax.ShapeDtypeStruct((depth, tp, w), F32),
                      jax.ShapeDtypeStruct((depth, ts, w), F32)]
    extra = list(prev)
    extra_specs = [pl.BlockSpec(memory_space=pl.ANY) for _ in extra]
    aliases = {n_inputs + k: k for k in range(len(extra))}
    return extra, extra_specs, out_specs, out_shape, aliases


def _proj_kv(xb, w_lat, w_kr, g, cos, sin, tm, tp, layer, depth, prev):
    t, d = xb.shape
    c = w_lat.shape[1]
    r = w_kr.shape[1]
    rows = lambda w: pl.BlockSpec((tm, w), lambda i: (i, 0))
    full = lambda a, b: pl.BlockSpec((a, b), lambda i: (0, 0))
    extra, extra_specs, out_specs, out_shape, aliases = _cache_outputs(
        (c, r), tm, tp, t - tp, layer, depth, prev, 6)
    return pl.pallas_call(
        functools.partial(_proj_kv_kernel, n_p=tp // tm),
        grid=(t // tm,),
        in_specs=[rows(d), full(d, c), full(d, r), full(1, c), rows(r // 2), rows(r // 2)]
                 + extra_specs,
        out_specs=out_specs + [rows(c), rows(r)],
        out_shape=out_shape + [jax.ShapeDtypeStruct((t, c), BF16),
                               jax.ShapeDtypeStruct((t, r), BF16)],
        input_output_aliases=aliases,
        compiler_params=_cparams(("arbitrary",)),
        name="proj_kv",
    )(xb, w_lat, w_kr, g.reshape(1, c), cos, sin, *extra)


def _proj_sb(xb, w_q, w_k, w_v, tm, tp, layer, depth, prev):
    t, d = xb.shape
    n = w_q.shape[1]
    rows = lambda w: pl.BlockSpec((tm, w), lambda i: (i, 0))
    weight = _resident((d, n))
    bf16 = jax.ShapeDtypeStruct((t, n), BF16)
    extra, extra_specs, out_specs, out_shape, aliases = _cache_outputs(
        (n, n), tm, tp, t - tp, layer, depth, prev, 4)
    return pl.pallas_call(
        functools.partial(_proj_sb_kernel, n_p=tp // tm),
        grid=(t // tm,),
        in_specs=[rows(d), weight, weight, weight] + extra_specs,
        out_specs=out_specs + [rows(n), rows(n), rows(n)],
        out_shape=out_shape + [bf16, bf16, bf16],
        input_output_aliases=aliases,
        compiler_params=_cparams(("arbitrary",)),
        name="proj_sb",
    )(xb, w_q, w_k, w_v, *extra)


def _proj_gates(xb, w, bias, tm):
    t, d = xb.shape
    n = w.shape[1]
    tn = _pick_tile(n, PROJ_COL_TILE, 128)
    return pl.pallas_call(
        functools.partial(_proj_gate_kernel, tn=tn),
        grid=(t // tm,),
        in_specs=[pl.BlockSpec((tm, d), lambda i: (i, 0)), _resident((d, n)),
                  pl.BlockSpec((1, n), lambda i: (0, 0))],
        out_specs=pl.BlockSpec((tm, n), lambda i: (i, 0)),
        out_shape=jax.ShapeDtypeStruct((t, n), BF16),
        compiler_params=_cparams(("parallel",)),
        name="proj_gates",
    )(xb, w, bias.reshape(1, n))


def _q_heads_kernel(qa_ref, wn_ref, wr_ref, wuk_ref, cos_ref, sin_ref, qt_ref, ql_ref, qr_ref,
                    *, n_p):
    i = pl.program_id(0)
    qa = qa_ref[...]
    q_nope = _dot(qa, wn_ref[...])
    q_rope = _dot(qa, wr_ref[...])
    cos = cos_ref[...]
    sin = sin_ref[...]
    roped = [_rope_rows(q_rope[:, h * ROPE_DIM:(h + 1) * ROPE_DIM], cos, sin) * MLA_QSCALE
             for h in range(MLA_HEADS)]

    @pl.when(i < n_p)
    def _():
        pad = jnp.zeros((q_nope.shape[0], MLA_QK_DIM - NOPE_DIM - ROPE_DIM), F32)
        for h in range(MLA_HEADS):
            nope = q_nope[:, h * NOPE_DIM:(h + 1) * NOPE_DIM] * MLA_QSCALE
            qt_ref[h] = jnp.concatenate([nope, roped[h], pad], axis=-1).T.astype(BF16)

    @pl.when(i >= n_p)
    def _():
        for h in range(MLA_HEADS):
            nope = q_nope[:, h * NOPE_DIM:(h + 1) * NOPE_DIM].astype(BF16)
            ql_ref[h] = (_dot(nope, wuk_ref[h]) * MLA_QSCALE).astype(BF16)
            qr_ref[h] = roped[h].astype(BF16)


def _q_heads(qa_n, w_nope, w_rope, wuk_h, cos, sin, tm, tp):
    t, ql = qa_n.shape
    h, _, c = wuk_h.shape
    half = ROPE_DIM // 2
    n_p = tp // tm
    return pl.pallas_call(
        functools.partial(_q_heads_kernel, n_p=n_p),
        grid=(t // tm,),
        in_specs=[pl.BlockSpec((tm, ql), lambda i: (i, 0)),
                  pl.BlockSpec((ql, h * NOPE_DIM), lambda i: (0, 0)),
                  pl.BlockSpec((ql, h * ROPE_DIM), lambda i: (0, 0)),
                  pl.BlockSpec((h, NOPE_DIM, c), lambda i: (0, 0, 0)),
                  pl.BlockSpec((tm, half), lambda i: (i, 0)),
                  pl.BlockSpec((tm, half), lambda i: (i, 0))],
        out_specs=[pl.BlockSpec((h, MLA_QK_DIM, tm), lambda i: (0, 0, jnp.minimum(i, n_p - 1))),
                   pl.BlockSpec((h, tm, c), lambda i: (0, jnp.maximum(i - n_p, 0), 0)),
                   pl.BlockSpec((h, tm, ROPE_DIM), lambda i: (0, jnp.maximum(i - n_p, 0), 0))],
        out_shape=[jax.ShapeDtypeStruct((h, MLA_QK_DIM, tp), BF16),
                   jax.ShapeDtypeStruct((h, t - tp, c), BF16),
                   jax.ShapeDtypeStruct((h, t - tp, ROPE_DIM), BF16)],
        compiler_params=_cparams(("arbitrary",)),
        name="q_heads",
    )(qa_n, w_nope, w_rope, wuk_h, cos, sin)


def _kv_heads_kernel(lat_ref, kr_ref, wk_ref, wv_ref, k_ref, v_ref):
    lat = lat_ref[...]
    k_nope = _dot(lat, wk_ref[...]).astype(BF16)
    kr = kr_ref[...]
    tail = jnp.concatenate(
        [kr, jnp.zeros((kr.shape[0], MLA_QK_DIM - NOPE_DIM - ROPE_DIM), BF16)], axis=-1)
    for h in range(MLA_HEADS):
        k_ref[:, h * MLA_QK_DIM:h * MLA_QK_DIM + NOPE_DIM] = k_nope[:, h * NOPE_DIM:(h + 1) * NOPE_DIM]
        k_ref[:, h * MLA_QK_DIM + NOPE_DIM:(h + 1) * MLA_QK_DIM] = tail
    v_ref[...] = _dot(lat, wv_ref[...]).astype(BF16)


def _kv_heads(lat_b, kr_b, w_uk, w_uv, tm, tp):
    c = lat_b.shape[1]
    n = w_uk.shape[1]
    rows = lambda w: pl.BlockSpec((tm, w), lambda i: (i, 0))
    kw = MLA_HEADS * MLA_QK_DIM
    return pl.pallas_call(
        _kv_heads_kernel,
        grid=(tp // tm,),
        in_specs=[rows(c), rows(ROPE_DIM), _resident((c, n)), _resident((c, n))],
        out_specs=[rows(kw), rows(n)],
        out_shape=[jax.ShapeDtypeStruct((tp, kw), BF16), jax.ShapeDtypeStruct((tp, n), BF16)],
        compiler_params=_cparams(("parallel",)),
        name="kv_heads",
    )(lat_b, kr_b, w_uk, w_uv)


def _mla_init(m_ref, l_ref, acc_ref):
    m_ref[...] = jnp.full(m_ref.shape, NEG_BIG, F32)
    l_ref[...] = jnp.zeros(l_ref.shape, F32)
    acc_ref[...] = jnp.zeros(acc_ref.shape, F32)


def _mla_softmax(masked, q_pos0, k_pos0, m_ref, l_ref, s_ref, p_ref, a_ref, tq, tk):
    rows = MLA_HEADS * tq
    if masked:
        k_chunk = (k_pos0 + lax.broadcasted_iota(jnp.int32, (tk, 1), 0)) // CHUNK
    for c in range(rows // MLA_SOFTMAX_COLS):
        cols = slice(c * MLA_SOFTMAX_COLS, (c + 1) * MLA_SOFTMAX_COLS)
        s = s_ref[:, cols]
        if masked:
            lane = c * MLA_SOFTMAX_COLS + lax.broadcasted_iota(jnp.int32, (1, MLA_SOFTMAX_COLS), 1)
            q_chunk = (q_pos0 + lax.rem(lane, tq)) // CHUNK
            s = jnp.where(k_chunk <= q_chunk, s, NEG_BIG)
        m_prev = m_ref[:, cols]
        m_new = jnp.maximum(m_prev, jnp.max(s, axis=0, keepdims=True))
        alpha = jnp.exp2(m_prev - m_new)
        p = jnp.exp2(s - m_new)
        l_ref[:, cols] = alpha * l_ref[:, cols] + jnp.sum(p, axis=0, keepdims=True)
        m_ref[:, cols] = m_new
        a_ref[:, cols] = alpha
        p_ref[:, cols] = p.astype(BF16)


def _dot_tn(a, b):
    return lax.dot_general(a, b, (((0,), (0,)), ((), ())), preferred_element_type=F32)


def _mla_latent_tile(ql_ref, qr_ref, klat, kr, masked, q_pos0, k_pos0,
                     m_ref, l_ref, acc_ref, s_ref, p_ref, a_ref, tq, tk):
    rows = MLA_HEADS * tq
    q_lat = ql_ref[...].reshape(rows, ql_ref.shape[-1])
    q_rope = qr_ref[...].reshape(rows, qr_ref.shape[-1])
    s_ref[...] = _dot_nt(klat, q_lat) + _dot_nt(kr, q_rope)
    _mla_softmax(masked, q_pos0, k_pos0, m_ref, l_ref, s_ref, p_ref, a_ref, tq, tk)
    acc_ref[...] = a_ref[...] * acc_ref[...] + _dot_tn(klat, p_ref[...])


def _mla_latent_finalize(wuv_ref, o_ref, l_ref, acc_ref, tq):
    inv = 1.0 / l_ref[...]
    for h in range(MLA_HEADS):
        cols = slice(h * tq, (h + 1) * tq)
        ctx_t = (acc_ref[:, cols] * inv[:, cols]).astype(BF16)
        o_ref[:, h * V_DIM:(h + 1) * V_DIM] = _dot_tn(ctx_t, wuv_ref[h]).astype(BF16)


def _mla_heads_tile(qt_ref, k_ref, v_ref, masked, q_pos0, k_pos0,
                    m_ref, l_ref, acc_ref, s_ref, p_ref, a_ref, tq, tk):
    for h in range(MLA_HEADS):
        cols = slice(h * tq, (h + 1) * tq)
        s_ref[:, cols] = _dot(k_ref[:, h * MLA_QK_DIM:(h + 1) * MLA_QK_DIM], qt_ref[h])
    _mla_softmax(masked, q_pos0, k_pos0, m_ref, l_ref, s_ref, p_ref, a_ref, tq, tk)
    for h in range(MLA_HEADS):
        cols = slice(h * tq, (h + 1) * tq)
        pv = _dot_tn(v_ref[:, h * V_DIM:(h + 1) * V_DIM], p_ref[:, cols])
        acc_ref[h] = a_ref[:, cols] * acc_ref[h] + pv


def _mla_heads_finalize(o_ref, l_ref, acc_ref, tq):
    inv = 1.0 / l_ref[...]
    for h in range(MLA_HEADS):
        cols = slice(h * tq, (h + 1) * tq)
        o_ref[:, h * V_DIM:(h + 1) * V_DIM] = (acc_ref[h] * inv[:, cols]).T.astype(BF16)


def _mla_prompt_kernel(qb_ref, kb_ref, qp_ref, kp_ref, fl_ref,
                       qt_ref, k_ref, v_ref, o_ref,
                       m_ref, l_ref, acc_ref, s_ref, p_ref, a_ref, *, tq, tk):
    n = pl.program_id(0)
    scratch = (m_ref, l_ref, acc_ref, s_ref, p_ref, a_ref)

    @pl.when(fl_ref[n] % 2 == 1)
    def _():
        _mla_init(m_ref, l_ref, acc_ref)

    @pl.when(fl_ref[n] < 2)
    def _():
        _mla_heads_tile(qt_ref, k_ref, v_ref, False, 0, 0, *scratch, tq, tk)

    @pl.when(fl_ref[n] >= 2)
    def _():
        _mla_heads_tile(qt_ref, k_ref, v_ref, True, qp_ref[n], kp_ref[n], *scratch, tq, tk)
        _mla_heads_finalize(o_ref, l_ref, acc_ref, tq)


def _mla_prompt(q_t, k_heads, v, batch, seq):
    h = q_t.shape[0]
    tq = _pick_tile(seq, MLA_Q_TILE, CHUNK)
    tk = _pick_tile(seq, MLA_K_TILE, CHUNK)
    nq, nk = seq // tq, seq // tk
    qb, kb, qp, kp, fl = [], [], [], [], []
    for b in range(batch):
        for qi in range(nq):
            last = ((qi + 1) * tq - 1) // tk
            for kj in range(last + 1):
                qb.append(b * nq + qi)
                kb.append(b * nk + kj)
                qp.append(qi * tq)
                kp.append(kj * tk)
                fl.append((1 if kj == 0 else 0) + (2 if kj == last else 0))
    tabs = [jnp.asarray(np.asarray(a, np.int32)) for a in (qb, kb, qp, kp, fl)]
    rows = h * tq
    grid_spec = pltpu.PrefetchScalarGridSpec(
        num_scalar_prefetch=5,
        grid=(len(qb),),
        in_specs=[
            pl.BlockSpec((h, MLA_QK_DIM, tq), lambda n, qb, kb, qp, kp, fl: (0, 0, qb[n])),
            pl.BlockSpec((tk, h * MLA_QK_DIM), lambda n, qb, kb, qp, kp, fl: (kb[n], 0)),
            pl.BlockSpec((tk, h * V_DIM), lambda n, qb, kb, qp, kp, fl: (kb[n], 0)),
        ],
        out_specs=pl.BlockSpec((tq, h * V_DIM), lambda n, qb, kb, qp, kp, fl: (qb[n], 0)),
        scratch_shapes=[pltpu.VMEM((1, rows), F32), pltpu.VMEM((1, rows), F32),
                        pltpu.VMEM((h, V_DIM, tq), F32), pltpu.VMEM((tk, rows), F32),
                        pltpu.VMEM((tk, rows), BF16), pltpu.VMEM((1, rows), F32)],
    )
    return pl.pallas_call(
        functools.partial(_mla_prompt_kernel, tq=tq, tk=tk),
        grid_spec=grid_spec,
        out_shape=jax.ShapeDtypeStruct((batch * seq, h * V_DIM), BF16),
        compiler_params=_cparams(("arbitrary",)),
        name="mla_prompt",
    )(*tabs, q_t, k_heads, v)


def _mla_sample_kernel(ql_ref, qr_ref, pl_ref, pr_ref, nl_ref, nr_ref, wuv_ref, o_ref,
                       m_ref, l_ref, acc_ref, a_ref, s_ref, p_ref, sn_ref, pn_ref,
                       *, tq, tk, n_past, n_kt):
    j = pl.program_id(1)

    @pl.when(j == 0)
    def _():
        _mla_init(m_ref, l_ref, acc_ref)

    @pl.when(j < n_kt)
    def _():
        _mla_latent_tile(ql_ref, qr_ref, pl_ref[...].astype(BF16), pr_ref[...].astype(BF16),
                         False, 0, 0, m_ref, l_ref, acc_ref, s_ref, p_ref, a_ref, tq, tk)

    @pl.when(j == n_kt)
    def _():
        _mla_latent_tile(ql_ref, qr_ref, nl_ref[...], nr_ref[...], True, n_past, n_past,
                         m_ref, l_ref, acc_ref, sn_ref, pn_ref, a_ref, tq, tq)
        _mla_latent_finalize(wuv_ref, o_ref, l_ref, acc_ref, tq)


def _mla_sample(q_lat, q_rope, past_lat, past_kr, layer, k_lat, k_rope, w_uv,
                row0, dec_batch, dec_seq):
    h, _, c = q_lat.shape
    n_past = past_lat.shape[2]
    tk = _pick_tile(n_past, MLA_K_TILE, 128)
    n_kt = n_past // tk
    blk0 = row0 // dec_seq
    rows = h * dec_seq
    past_idx = lambda b, j: (layer, b, jnp.minimum(j, n_kt - 1), 0)
    return pl.pallas_call(
        functools.partial(_mla_sample_kernel, tq=dec_seq, tk=tk, n_past=n_past, n_kt=n_kt),
        grid=(dec_batch, n_kt + 1),
        in_specs=[
            pl.BlockSpec((h, dec_seq, c), lambda b, j: (0, b, 0)),
            pl.BlockSpec((h, dec_seq, ROPE_DIM), lambda b, j: (0, b, 0)),
            pl.BlockSpec((None, None, tk, c), past_idx),
            pl.BlockSpec((None, None, tk, ROPE_DIM), past_idx),
            pl.BlockSpec((dec_seq, c), lambda b, j: (blk0 + b, 0)),
            pl.BlockSpec((dec_seq, ROPE_DIM), lambda b, j: (blk0 + b, 0)),
            pl.BlockSpec((h, c, V_DIM), lambda b, j: (0, 0, 0)),
        ],
        out_specs=pl.BlockSpec((dec_seq, h * V_DIM), lambda b, j: (b, 0)),
        out_shape=jax.ShapeDtypeStruct((dec_batch * dec_seq, h * V_DIM), BF16),
        scratch_shapes=[pltpu.VMEM((1, rows), F32), pltpu.VMEM((1, rows), F32),
                        pltpu.VMEM((c, rows), F32), pltpu.VMEM((1, rows), F32),
                        pltpu.VMEM((tk, rows), F32), pltpu.VMEM((tk, rows), BF16),
                        pltpu.VMEM((dec_seq, rows), F32), pltpu.VMEM((dec_seq, rows), BF16)],
        compiler_params=_cparams(("parallel", "arbitrary")),
        name="mla_sample",
    )(q_lat, q_rope, past_lat, past_kr, k_lat, k_rope, w_uv)


def _scan_matrix(sub):
    j = np.arange(sub)[:, None]
    s = np.arange(sub)[None, :]
    u = (j > s).astype(np.float32)
    return jnp.asarray(np.concatenate([u, u], axis=0), dtype=BF16)


_SbChunk = collections.namedtuple("_SbChunk", "q k v before carry_idx acc_idx")


def _sb_scores(qc, kc, before):
    z = _dot_nt(qc, kc)
    softplus = jnp.log(1.0 + jnp.exp2(-jnp.abs(z))) * LOG2E
    log_beta = jnp.minimum(z, 0.0) - softplus
    log_rest = log_beta - z
    if before is not None:
        log_rest = jnp.where(before, log_rest, 0.0)
    hi = log_rest.astype(BF16)
    lo = (log_rest - hi.astype(F32)).astype(BF16)
    return (log_beta, jnp.concatenate([hi, lo], axis=-1),
            jnp.sum(log_rest, axis=-1, keepdims=True))


def _sb_weights(log_beta, hi_lo, u2, carry, before):
    a = jnp.exp2(log_beta + _dot(hi_lo, u2) + carry)
    if before is not None:
        a = jnp.where(before, a, 0.0)
    return a.astype(BF16)


def _sb_pipeline(chunks, u2, acc_ref, carry_ref):
    scored = None
    weighted = None
    for j in range(len(chunks) + 2):
        new_scored = None
        if j < len(chunks):
            ch = chunks[j]
            new_scored = (ch,) + _sb_scores(ch.q(), ch.k(), ch.before)
        new_weighted = None
        if scored is not None:
            ch, log_beta, hi_lo, row_sum = scored
            carry = carry_ref[ch.carry_idx]
            new_weighted = (ch, _sb_weights(log_beta, hi_lo, u2, carry, ch.before))
            carry_ref[ch.carry_idx] = carry + row_sum
        if weighted is not None:
            ch, a = weighted
            acc_ref[ch.acc_idx] += _dot(a, ch.v())
        scored, weighted = new_scored, new_weighted


def _sb_tile(q_ref, k_ref, v_ref, u2, acc_ref, carry_ref, diagonal, tq, tk, sub, rq):
    chunks = []
    for c in reversed(range(tk // sub)):
        k0, k1 = c * sub, (c + 1) * sub
        for r in range(tq // rq):
            r0, r1 = r * rq, (r + 1) * rq
            before = None
            if diagonal:
                if k0 >= r1 - 1:
                    continue
                if k1 > r0:
                    before = ((k0 + lax.broadcasted_iota(jnp.int32, (1, sub), 1))
                              < (r0 + lax.broadcasted_iota(jnp.int32, (rq, 1), 0)))
            for h in range(SB_HEADS):
                cols = slice(h * SB_DIM, (h + 1) * SB_DIM)
                rows, keys = slice(r0, r1), slice(k0, k1)
                chunks.append(_SbChunk(
                    q=functools.partial(lambda rr, cc: q_ref[rr, cc], rows, cols),
                    k=functools.partial(lambda kk, cc: k_ref[kk, cc], keys, cols),
                    v=functools.partial(lambda kk, cc: v_ref[kk, cc], keys, cols),
                    before=before, carry_idx=(h, rows, slice(None)), acc_idx=(rows, cols)))
    _sb_pipeline(chunks, u2, acc_ref, carry_ref)


def _sb_prompt_kernel(q_ref, k_ref, v_ref, u_ref, k_hbm, v_hbm, o_ref,
                      acc_ref, carry_ref, kbuf, vbuf, sem, *, nq, tq, sub, rq):
    g = pl.program_id(0)
    n_old = lax.rem(g, nq) * (tq // sub)

    def tile_copies(i, slot):
        start = g * tq - (i + 1) * sub
        return (pltpu.make_async_copy(k_hbm.at[pl.ds(start, sub), :], kbuf.at[slot], sem.at[0, slot]),
                pltpu.make_async_copy(v_hbm.at[pl.ds(start, sub), :], vbuf.at[slot], sem.at[1, slot]))

    def start_tile(i, slot):
        for copy in tile_copies(i, slot):
            copy.start()

    def wait_tile(i, slot):
        for copy in tile_copies(i, slot):
            copy.wait()

    def any_alive():
        return (jnp.max(carry_ref[...]) > SB_DEAD_LOG2).astype(jnp.int32)

    @pl.when(n_old > 0)
    def _():
        start_tile(0, 0)

    acc_ref[...] = jnp.zeros(acc_ref.shape, F32)
    carry_ref[...] = jnp.zeros(carry_ref.shape, F32)
    _sb_tile(q_ref, k_ref, v_ref, u_ref[...], acc_ref, carry_ref, True, tq, tq, sub, rq)

    def tile_step(state):
        i, _ = state
        slot = lax.rem(i, 2)
        wait_tile(i, slot)

        @pl.when(i + 1 < n_old)
        def _():
            start_tile(i + 1, 1 - slot)

        for r in range(tq // rq):
            rows = slice(r * rq, (r + 1) * rq)

            @pl.when(jnp.max(carry_ref[:, rows, :]) > SB_DEAD_LOG2)
            def _(rows=rows):
                chunks = []
                for h in range(SB_HEADS):
                    cols = slice(h * SB_DIM, (h + 1) * SB_DIM)
                    chunks.append(_SbChunk(
                        q=functools.partial(lambda rr, cc: q_ref[rr, cc], rows, cols),
                        k=functools.partial(lambda cc: kbuf[slot, :, cc], cols),
                        v=functools.partial(lambda cc: vbuf[slot, :, cc], cols),
                        before=None, carry_idx=(h, rows, slice(None)), acc_idx=(rows, cols)))
                _sb_pipeline(chunks, u_ref[...], acc_ref, carry_ref)
        return i + 1, any_alive()

    tiles_done, _ = lax.while_loop(
        lambda state: jnp.logical_and(state[0] < n_old, state[1] == 1),
        tile_step, (jnp.int32(0), any_alive()))

    @pl.when(tiles_done < n_old)
    def _():
        wait_tile(tiles_done, lax.rem(tiles_done, 2))

    o_ref[...] = acc_ref[...].astype(BF16)


def _sb_prompt(q, k, v, batch, seq):
    width = q.shape[1]
    tq = _pick_tile(seq, SB_TILE, 128)
    sub = _pick_tile(tq, SB_SUB, 128)
    rq = _pick_tile(tq, SB_ROWS, 16)
    nq = seq // tq
    rows = pl.BlockSpec((tq, width), lambda g: (g, 0))
    in_hbm = pl.BlockSpec(memory_space=pl.ANY)
    return pl.pallas_call(
        functools.partial(_sb_prompt_kernel, nq=nq, tq=tq, sub=sub, rq=rq),
        grid=(batch * nq,),
        in_specs=[rows, rows, rows, pl.BlockSpec((2 * sub, sub), lambda g: (0, 0)),
                  in_hbm, in_hbm],
        out_specs=rows,
        out_shape=jax.ShapeDtypeStruct((batch * seq, width), BF16),
        scratch_shapes=[pltpu.VMEM((tq, width), F32), pltpu.VMEM((SB_HEADS, tq, 1), F32),
                        pltpu.VMEM((2, sub, width), BF16), pltpu.VMEM((2, sub, width), BF16),
                        pltpu.SemaphoreType.DMA((2, 2))],
        compiler_params=_cparams(("arbitrary",)),
        name="sb_prompt",
    )(q, k, v, _scan_matrix(sub), k, v)


def _sb_sample_kernel(q_ref, nk_ref, nv_ref, un_ref, up_ref, pk_hbm, pv_hbm, o_ref,
                      acc_ref, carry_ref, kbuf, vbuf, sem, *, layer, tq, tk, n_kt):
    b = pl.program_id(0)

    def tile_copies(i, slot):
        start = (n_kt - 1 - i) * tk
        copies = []
        for h in range(SB_HEADS):
            copies.append(pltpu.make_async_copy(
                pk_hbm.at[layer, b, pl.ds(start, tk), h, :], kbuf.at[slot, h], sem.at[0, slot]))
            copies.append(pltpu.make_async_copy(
                pv_hbm.at[layer, b, pl.ds(start, tk), h, :], vbuf.at[slot, h], sem.at[1, slot]))
        return copies

    def start_tile(i, slot):
        for copy in tile_copies(i, slot):
            copy.start()

    def wait_tile(i, slot):
        for copy in tile_copies(i, slot):
            copy.wait()

    def alive():
        return (jnp.max(carry_ref[...]) > SB_DEAD_LOG2).astype(jnp.int32)

    start_tile(0, 0)

    acc_ref[...] = jnp.zeros(acc_ref.shape, F32)
    carry_ref[...] = jnp.zeros(carry_ref.shape, F32)
    before = (lax.broadcasted_iota(jnp.int32, (1, tq), 1)
              < lax.broadcasted_iota(jnp.int32, (tq, 1), 0))
    chunks = []
    for h in range(SB_HEADS):
        cols = slice(h * SB_DIM, (h + 1) * SB_DIM)
        chunks.append(_SbChunk(
            q=functools.partial(lambda cc: q_ref[:, cc], cols),
            k=functools.partial(lambda cc: nk_ref[:, cc], cols),
            v=functools.partial(lambda cc: nv_ref[:, cc], cols),
            before=before, carry_idx=(h, slice(None), slice(None)), acc_idx=(slice(None), cols)))
    _sb_pipeline(chunks, un_ref[...], acc_ref, carry_ref)

    def tile_step(state):
        i, _ = state
        slot = lax.rem(i, 2)
        wait_tile(i, slot)

        @pl.when(i + 1 < n_kt)
        def _():
            start_tile(i + 1, 1 - slot)

        chunks = []
        for h in range(SB_HEADS):
            cols = slice(h * SB_DIM, (h + 1) * SB_DIM)
            chunks.append(_SbChunk(
                q=functools.partial(lambda cc: q_ref[:, cc], cols),
                k=functools.partial(lambda hh: kbuf[slot, hh].astype(BF16), h),
                v=functools.partial(lambda hh: vbuf[slot, hh].astype(BF16), h),
                before=None, carry_idx=(h, slice(None), slice(None)), acc_idx=(slice(None), cols)))
        _sb_pipeline(chunks, up_ref[...], acc_ref, carry_ref)
        return i + 1, alive()

    tiles_done, _ = lax.while_loop(
        lambda state: jnp.logical_and(state[0] < n_kt, state[1] == 1),
        tile_step, (jnp.int32(0), alive()))

    @pl.when(tiles_done < n_kt)
    def _():
        wait_tile(tiles_done, lax.rem(tiles_done, 2))

    o_ref[...] = acc_ref[...].astype(BF16)


def _sb_sample(q, past_k, past_v, layer, k, v, row0, dec_batch, dec_seq):
    width = q.shape[1]
    n_past = past_k.shape[2]
    tk = _pick_tile(n_past, SB_SUB, 128)
    n_kt = n_past // tk
    blk0 = row0 // dec_seq
    new_rows = pl.BlockSpec((dec_seq, width), lambda b: (blk0 + b, 0))
    in_hbm = pl.BlockSpec(memory_space=pl.ANY)
    return pl.pallas_call(
        functools.partial(_sb_sample_kernel, layer=layer, tq=dec_seq, tk=tk, n_kt=n_kt),
        grid=(dec_batch,),
        in_specs=[
            new_rows, new_rows, new_rows,
            pl.BlockSpec((2 * dec_seq, dec_seq), lambda b: (0, 0)),
            pl.BlockSpec((2 * tk, tk), lambda b: (0, 0)),
            in_hbm, in_hbm,
        ],
        out_specs=pl.BlockSpec((dec_seq, width), lambda b: (b, 0)),
        out_shape=jax.ShapeDtypeStruct((dec_batch * dec_seq, width), BF16),
        scratch_shapes=[pltpu.VMEM((dec_seq, width), F32),
                        pltpu.VMEM((SB_HEADS, dec_seq, 1), F32),
                        pltpu.VMEM((2, SB_HEADS, tk, SB_DIM), F32),
                        pltpu.VMEM((2, SB_HEADS, tk, SB_DIM), F32),
                        pltpu.SemaphoreType.DMA((2, 2))],
        compiler_params=_cparams(("arbitrary",)),
        name="sb_sample",
    )(q, k, v, _scan_matrix(dec_seq), _scan_matrix(tk), past_k, past_v)


def _merge_kernel(oap_ref, oas_ref, obp_ref, obs_ref, wa_ref, wb_ref, g_ref, o_ref,
                  *, n_p, tn):
    i = pl.program_id(0)
    o_a = _load_split(i, n_p, oap_ref, oas_ref)
    o_b = _load_split(i, n_p, obp_ref, obs_ref)
    n = o_ref.shape[1]
    for c in range(n // tn):
        cols = slice(c * tn, (c + 1) * tn)
        a = _dot(o_a, wa_ref[:, cols])
        b = _dot(o_b, wb_ref[:, cols])
        g_a = g_ref[:, cols].astype(F32)
        g_b = g_ref[:, n + c * tn:n + (c + 1) * tn].astype(F32)
        o_ref[:, cols] = (g_a * a + g_b * b).astype(BF16)


def _merge(oa_p, oa_s, ob_p, ob_s, w_a, w_b, gates, tm):
    da = oa_p.shape[1]
    db = ob_p.shape[1]
    t = oa_p.shape[0] + oa_s.shape[0]
    n_p = oa_p.shape[0] // tm
    n = w_a.shape[1]
    tn = _pick_tile(n, COL_TILE, 128)
    return pl.pallas_call(
        functools.partial(_merge_kernel, n_p=n_p, tn=tn),
        grid=(t // tm,),
        in_specs=[*_split_specs(tm, da, n_p),
                  *_split_specs(tm, db, n_p),
                  _resident((da, n)), _resident((db, n)),
                  pl.BlockSpec((tm, 2 * n), lambda i: (i, 0))],
        out_specs=pl.BlockSpec((tm, n), lambda i: (i, 0)),
        out_shape=jax.ShapeDtypeStruct((t, n), BF16),
        compiler_params=_cparams(("parallel",)),
        name="merge",
    )(oa_p, oa_s, ob_p, ob_s, w_a, w_b, gates)


def _res_ln_finalize(z_ref, g_ref, b_ref, o_ref, ob_ref):
    n_tiles, _, tn = z_ref.shape
    width = n_tiles * tn
    total = z_ref[0].sum(axis=-1, keepdims=True)
    for c in range(1, n_tiles):
        total += z_ref[c].sum(axis=-1, keepdims=True)
    mu = total / width
    sq = jnp.square(z_ref[0] - mu).sum(axis=-1, keepdims=True)
    for c in range(1, n_tiles):
        sq += jnp.square(z_ref[c] - mu).sum(axis=-1, keepdims=True)
    rstd = lax.rsqrt(sq / width + LN_EPS)
    for c in range(n_tiles):
        cols = slice(c * tn, (c + 1) * tn)
        y = (z_ref[c] - mu) * rstd * g_ref[:, cols] + b_ref[:, cols]
        o_ref[:, cols] = y
        if ob_ref is not None:
            ob_ref[:, cols] = y.astype(BF16)


def _out_ln_kernel(m_ref, w_ref, x_ref, g_ref, b_ref, o_ref, ob_ref, z_ref, *, alpha):
    m = m_ref[...]
    n_tiles, _, tn = z_ref.shape
    for c in range(n_tiles):
        cols = slice(c * tn, (c + 1) * tn)
        z_ref[c] = alpha * x_ref[:, cols] + _dot(m, w_ref[:, cols])
    _res_ln_finalize(z_ref, g_ref, b_ref, o_ref, ob_ref)


def _out_ln(merged, w, x, g, b, alpha, tm):
    t, d = x.shape
    k = merged.shape[1]
    tn = _pick_tile(d, COL_TILE, 128)
    full_row = pl.BlockSpec((tm, d), lambda i: (i, 0))
    vec = pl.BlockSpec((1, d), lambda i: (0, 0))
    return pl.pallas_call(
        functools.partial(_out_ln_kernel, alpha=alpha),
        grid=(t // tm,),
        in_specs=[pl.BlockSpec((tm, k), lambda i: (i, 0)), _resident((k, d)), full_row, vec, vec],
        out_specs=[full_row, full_row],
        out_shape=[jax.ShapeDtypeStruct((t, d), F32), jax.ShapeDtypeStruct((t, d), BF16)],
        scratch_shapes=[pltpu.VMEM((d // tn, tm, tn), F32)],
        compiler_params=_cparams(("parallel",)),
        name="out_ln",
    )(merged, w, x, g.reshape(1, d), b.reshape(1, d))


def _ffn_ln_kernel(xb_ref, wg_ref, wu_ref, wd_ref, x_ref, g_ref, b_ref, o_ref, ob_ref,
                   acc_ref, *, alpha):
    f = pl.program_id(1)

    @pl.when(f == 0)
    def _():
        acc_ref[...] = jnp.zeros(acc_ref.shape, F32)

    xb = xb_ref[...]
    gate = _dot(xb, wg_ref[...])
    up = _dot(xb, wu_ref[...])
    hidden = (gate * _sigmoid(gate) * up).astype(BF16)
    acc_ref[...] += _dot(hidden, wd_ref[...])

    @pl.when(f == pl.num_programs(1) - 1)
    def _():
        y = _layer_norm_rows(alpha * x_ref[...] + acc_ref[...], g_ref[...], b_ref[...])
        o_ref[...] = y
        ob_ref[...] = y.astype(BF16)


def _ffn_ln(xb, w_gu, w_down, x, g, b, alpha, tm):
    t, d = x.shape
    ff = w_down.shape[0]
    tf = _pick_tile(ff, COL_TILE, 128)
    nf = ff // tf
    full_row = pl.BlockSpec((tm, d), lambda i, f: (i, 0))
    vec = pl.BlockSpec((1, d), lambda i, f: (0, 0))
    return pl.pallas_call(
        functools.partial(_ffn_ln_kernel, alpha=alpha),
        grid=(t // tm, nf),
        in_specs=[full_row,
                  pl.BlockSpec((d, tf), lambda i, f: (0, f)),
                  pl.BlockSpec((d, tf), lambda i, f: (0, f + nf)),
                  pl.BlockSpec((tf, d), lambda i, f: (f, 0)),
                  full_row, vec, vec],
        out_specs=[full_row, full_row],
        out_shape=[jax.ShapeDtypeStruct((t, d), F32), jax.ShapeDtypeStruct((t, d), BF16)],
        scratch_shapes=[pltpu.VMEM((tm, d), F32)],
        compiler_params=_cparams(("parallel", "arbitrary")),
        name="ffn_ln",
    )(xb, w_gu, w_gu, w_down, x, g.reshape(1, d), b.reshape(1, d))


def _ple_ln_kernel(xb_ref, wg_ref, bg_ref, pp_ref, ps_ref, wp_ref, x_ref, g_ref, b_ref,
                   o1_ref, o2_ref, z_ref, *, alpha, n_p, last):
    i = pl.program_id(0)
    xb = xb_ref[...]
    p = _load_split(i, n_p, pp_ref, ps_ref).astype(BF16)
    n_tiles, _, tn = z_ref.shape
    for c in range(n_tiles):
        cols = slice(c * tn, (c + 1) * tn)
        gate = _sigmoid(_dot(xb, wg_ref[:, cols]) + bg_ref[:, cols])
        z_ref[c] = alpha * x_ref[:, cols] + gate * _dot(p, wp_ref[:, cols])

    if not last:
        _res_ln_finalize(z_ref, g_ref, b_ref, o1_ref, o2_ref)
    else:
        @pl.when(i < n_p)
        def _():
            _res_ln_finalize(z_ref, g_ref, b_ref, o1_ref, None)

        @pl.when(i >= n_p)
        def _():
            _res_ln_finalize(z_ref, g_ref, b_ref, o2_ref, None)


def _ple_ln(xb, w_gate, b_gate, p_p, p_s, layer, w_proj, x, g, b, alpha, tm, last):
    t, d = x.shape
    pd = p_p.shape[2]
    tp = p_p.shape[1]
    n_p = tp // tm
    tn = _pick_tile(d, COL_TILE, 128)
    full_row = pl.BlockSpec((tm, d), lambda i: (i, 0))
    vec = pl.BlockSpec((1, d), lambda i: (0, 0))
    if last:
        out_specs = list(_split_specs(tm, d, n_p))
        out_shape = [jax.ShapeDtypeStruct((tp, d), F32), jax.ShapeDtypeStruct((t - tp, d), F32)]
    else:
        out_specs = [full_row, full_row]
        out_shape = [jax.ShapeDtypeStruct((t, d), F32), jax.ShapeDtypeStruct((t, d), BF16)]
    return pl.pallas_call(
        functools.partial(_ple_ln_kernel, alpha=alpha, n_p=n_p, last=last),
        grid=(t // tm,),
        in_specs=[full_row, _resident((d, d)), vec,
                  pl.BlockSpec((None, tm, pd), lambda i: (layer, jnp.minimum(i, n_p - 1), 0)),
                  pl.BlockSpec((None, tm, pd), lambda i: (layer, jnp.maximum(i - n_p, 0), 0)),
                  _resident((pd, d)), full_row, vec, vec],
        out_specs=out_specs,
        out_shape=out_shape,
        scratch_shapes=[pltpu.VMEM((d // tn, tm, tn), F32)],
        compiler_params=_cparams(("arbitrary",)),
        name="ple_ln",
    )(xb, w_gate, b_gate.reshape(1, d), p_p, p_s, w_proj, x, g.reshape(1, d), b.reshape(1, d))


def _cast_kernel(w_ref, *o_refs, bounds):
    w = w_ref[...]
    for o_ref, (lo, hi) in zip(o_refs, bounds):
        o_ref[...] = w[:, lo:hi].astype(BF16)


def _cast_weight(w, layer, col_groups=None):
    _, r, n = w.shape
    bounds = tuple(col_groups) if col_groups else ((0, n),)
    br = _pick_tile(r, max(16, CAST_BLOCK_BYTES // (4 * n)), 16)
    return pl.pallas_call(
        functools.partial(_cast_kernel, bounds=bounds),
        grid=(r // br,),
        in_specs=[pl.BlockSpec((None, br, n), lambda i: (layer, i, 0))],
        out_specs=[pl.BlockSpec((br, hi - lo), lambda i: (i, 0)) for lo, hi in bounds],
        out_shape=[jax.ShapeDtypeStruct((r, hi - lo), BF16) for lo, hi in bounds],
        compiler_params=_cparams(("parallel",)),
        name="cast_weight",
    )(w)


def _rope_tables(pos):
    half = ROPE_DIM // 2
    inv_freq = 1.0 / (ROPE_THETA ** (jnp.arange(half, dtype=F32) * (2.0 / ROPE_DIM)))
    ang = pos.astype(F32)[:, None] * inv_freq[None, :]
    return jnp.cos(ang), jnp.sin(ang)


def kernel(x_prompt, x_sample, cache_mla_latent, cache_mla_krope, cache_sb_k, cache_sb_v,
           p_prompt, p_sample, ln_in_g, ln_in_b, w_in, b_gate, q_a_norm_g, w_q_b,
           kv_a_norm_g, w_kv_b, w_branch_a, w_branch_b, w_out, ln1_g, ln1_b,
           w_ffn_gu, w_ffn_down, ln2_g, ln2_b, w_ple_gate, b_ple_gate, w_ple_proj,
           ln3_g, ln3_b):
    batch, seq, d = x_prompt.shape
    dec_batch, dec_seq, _ = x_sample.shape
    depth = w_in.shape[0]
    n_past = cache_mla_latent.shape[2]
    tp = batch * seq
    ts = dec_batch * dec_seq
    sb_width = SB_HEADS * SB_DIM
    alpha = (2 * depth) ** 0.25

    pos = jnp.concatenate([jnp.tile(jnp.arange(seq, dtype=jnp.int32), batch),
                           jnp.tile(n_past + jnp.arange(dec_seq, dtype=jnp.int32), dec_batch)])
    cos, sin = _rope_tables(pos)

    tm = _pick_tile(math.gcd(tp, ts), ROW_TILE, 16)
    p_p = p_prompt.reshape(depth, tp, -1)
    p_s = p_sample.reshape(depth, ts, -1)

    x, xb = _ln_in(x_prompt.reshape(tp, d), x_sample.reshape(ts, d), ln_in_g, ln_in_b, tm)

    splits = np.cumsum([0, Q_LORA, KV_LORA, ROPE_DIM, sb_width, sb_width, sb_width, 2 * d])
    def cache_arrays(widths):
        return [jnp.zeros((depth, n, w), F32) for w in widths for n in (tp, ts)]

    kv_cache = cache_arrays((KV_LORA, ROPE_DIM))
    sb_cache = cache_arrays((sb_width, sb_width))
    for l in range(depth):
        w_qa, w_lat, w_kr, w_sq, w_sk, w_sv, w_gates = _cast_weight(
            w_in, l, [(int(splits[i]), int(splits[i + 1])) for i in range(7)])
        (wb_a,) = _cast_weight(w_branch_a, l)
        (wb_b,) = _cast_weight(w_branch_b, l)
        (wb_out,) = _cast_weight(w_out, l)
        (wb_gu,) = _cast_weight(w_ffn_gu, l)
        (wb_down,) = _cast_weight(w_ffn_down, l)
        (wb_pg,) = _cast_weight(w_ple_gate, l)
        (wb_pp,) = _cast_weight(w_ple_proj, l)
        wq = w_q_b[l].reshape(Q_LORA, MLA_HEADS, NOPE_DIM + ROPE_DIM)
        wq_nope = wq[..., :NOPE_DIM].reshape(Q_LORA, MLA_HEADS * NOPE_DIM).astype(BF16)
        wq_rope = wq[..., NOPE_DIM:].reshape(Q_LORA, MLA_HEADS * ROPE_DIM).astype(BF16)
        w_kv = w_kv_b[l].reshape(KV_LORA, MLA_HEADS, NOPE_DIM + V_DIM)
        wuk_h = w_kv[..., :NOPE_DIM].transpose(1, 2, 0).astype(BF16)
        wuv_h = w_kv[..., NOPE_DIM:].transpose(1, 0, 2).astype(BF16)
        wuk_all = w_kv[..., :NOPE_DIM].reshape(KV_LORA, MLA_HEADS * NOPE_DIM).astype(BF16)
        wuv_all = w_kv[..., NOPE_DIM:].reshape(KV_LORA, MLA_HEADS * V_DIM).astype(BF16)

        qa_n = _proj_qa(xb, w_qa, q_a_norm_g[l], tm)
        *kv_cache, lat_b, kr_b = _proj_kv(xb, w_lat, w_kr, kv_a_norm_g[l], cos, sin,
                                          tm, tp, l, depth, kv_cache)
        *sb_cache, sbq, sbk_b, sbv_b = _proj_sb(xb, w_sq, w_sk, w_sv, tm, tp, l, depth, sb_cache)
        gates = _proj_gates(xb, w_gates, b_gate[l], tm)

        q_t, q_lat, q_rope = _q_heads(qa_n, wq_nope, wq_rope, wuk_h, cos, sin, tm, tp)
        k_heads, v_heads = _kv_heads(lat_b, kr_b, wuk_all, wuv_all, tm, tp)
        oa_p = _mla_prompt(q_t, k_heads, v_heads, batch, seq)
        oa_s = _mla_sample(q_lat, q_rope, cache_mla_latent, cache_mla_krope, l, lat_b, kr_b,
                           wuv_h, tp, dec_batch, dec_seq)
        ob_p = _sb_prompt(sbq, sbk_b, sbv_b, batch, seq)
        ob_s = _sb_sample(sbq, cache_sb_k, cache_sb_v, l, sbk_b, sbv_b, tp, dec_batch, dec_seq)

        merged = _merge(oa_p, oa_s, ob_p, ob_s, wb_a, wb_b, gates, tm)
        x, xb = _out_ln(merged, wb_out, x, ln1_g[l], ln1_b[l], alpha, tm)
        x, xb = _ffn_ln(xb, wb_gu, wb_down, x, ln2_g[l], ln2_b[l], alpha, tm)
        x, xb = _ple_ln(xb, wb_pg, b_ple_gate[l], p_p, p_s, l, wb_pp, x, ln3_g[l], ln3_b[l],
                        alpha, tm, last=(l == depth - 1))

    lat_p, lat_s, kr_p, kr_s = kv_cache
    k_p, k_s, v_p, v_s = sb_cache
    head_dims = (SB_HEADS, SB_DIM)
    return (x.reshape(batch, seq, d), xb.reshape(dec_batch, dec_seq, d),
            lat_p.reshape(depth, batch, seq, KV_LORA), kr_p.reshape(depth, batch, seq, ROPE_DIM),
            k_p.reshape(depth, batch, seq, *head_dims), v_p.reshape(depth, batch, seq, *head_dims),
            lat_s.reshape(depth, dec_batch, dec_seq, KV_LORA),
            kr_s.reshape(depth, dec_batch, dec_seq, ROPE_DIM),
            k_s.reshape(depth, dec_batch, dec_seq, *head_dims),
            v_s.reshape(depth, dec_batch, dec_seq, *head_dims))
```

```python
import collections
import functools
import math

import numpy as np
import jax
import jax.numpy as jnp
from jax import lax
from jax.experimental import pallas as pl
from jax.experimental.pallas import tpu as pltpu

CHUNK = 64
MLA_HEADS = 16
Q_LORA = 512
KV_LORA = 512
NOPE_DIM = 128
ROPE_DIM = 64
V_DIM = 128
ROPE_THETA = 10000.0
MLA_SCALE = (NOPE_DIM + ROPE_DIM) ** -0.5
SB_HEADS = 8
SB_DIM = 128
SB_SCALE = SB_DIM ** -0.5
LN_EPS = 1e-5
RMS_EPS = 1e-6
LOG2E = 1.4426950408889634
MLA_QSCALE = MLA_SCALE * LOG2E
MLA_QK_DIM = 256
SB_QSCALE = SB_SCALE * LOG2E
SB_DEAD_LOG2 = -160.0

BF16 = jnp.bfloat16
F32 = jnp.float32

V7X_VMEM_LIMIT_BYTES = 56 * 1024 * 1024
ROW_TILE = 512
COL_TILE = 512
PROJ_COL_TILE = 1024
MLA_Q_TILE = 256
MLA_K_TILE = 512
MLA_SOFTMAX_COLS = 128
SB_TILE = 512
SB_SUB = 256
SB_ROWS = 128
CAST_BLOCK_BYTES = 6 * 1024 * 1024
NEG_BIG = -1e30


def _pick_tile(n, target, mult):
    best = None
    for t in range(mult, min(n, target) + 1, mult):
        if n % t == 0:
            best = t
    if best is None:
        return n
    return best


def _cparams(sem):
    return pltpu.CompilerParams(dimension_semantics=sem,
                                vmem_limit_bytes=V7X_VMEM_LIMIT_BYTES)


def _resident(shape):
    return pl.BlockSpec(shape, lambda *_: (0,) * len(shape), pipeline_mode=pl.Buffered(1))


def _dot(a, b):
    return jnp.dot(a, b, preferred_element_type=F32)


def _dot_nt(a, b):
    return lax.dot_general(a, b, (((1,), (1,)), ((), ())), preferred_element_type=F32)


def _layer_norm_rows(z, g, b):
    mu = jnp.mean(z, axis=-1, keepdims=True)
    zc = z - mu
    var = jnp.mean(zc * zc, axis=-1, keepdims=True)
    return zc * lax.rsqrt(var + LN_EPS) * g + b


def _rms_norm_rows(z, g):
    return z * lax.rsqrt(jnp.mean(z * z, axis=-1, keepdims=True) + RMS_EPS) * g


def _sigmoid(z):
    return 1.0 / (1.0 + jnp.exp(-z))


def _rope_rows(x, cos, sin):
    half = ROPE_DIM // 2
    x1 = x[:, :half]
    x2 = x[:, half:]
    return jnp.concatenate([x1 * cos - x2 * sin, x2 * cos + x1 * sin], axis=-1)


def _split_specs(tm, width, n_p):
    return (pl.BlockSpec((tm, width), lambda i, *_: (jnp.minimum(i, n_p - 1), 0)),
            pl.BlockSpec((tm, width), lambda i, *_: (jnp.maximum(i - n_p, 0), 0)))


def _load_split(i, n_p, p_ref, s_ref):
    return jnp.where(i < n_p, p_ref[...], s_ref[...])


def _store_split(i, n_p, p_ref, s_ref, value):
    @pl.when(i < n_p)
    def _():
        p_ref[...] = value

    @pl.when(i >= n_p)
    def _():
        s_ref[...] = value


def _ln_in_kernel(xp_ref, xs_ref, g_ref, b_ref, o_ref, ob_ref, *, n_p):
    x = _load_split(pl.program_id(0), n_p, xp_ref, xs_ref)
    y = _layer_norm_rows(x, g_ref[...], b_ref[...])
    o_ref[...] = y
    ob_ref[...] = y.astype(BF16)


def _ln_in(x_p, x_s, g, b, tm):
    d = x_p.shape[1]
    t = x_p.shape[0] + x_s.shape[0]
    n_p = x_p.shape[0] // tm
    row = pl.BlockSpec((tm, d), lambda i: (i, 0))
    vec = pl.BlockSpec((1, d), lambda i: (0, 0))
    return pl.pallas_call(
        functools.partial(_ln_in_kernel, n_p=n_p),
        grid=(t // tm,),
        in_specs=[*_split_specs(tm, d, n_p), vec, vec],
        out_specs=[row, row],
        out_shape=[jax.ShapeDtypeStruct((t, d), F32), jax.ShapeDtypeStruct((t, d), BF16)],
        compiler_params=_cparams(("parallel",)),
        name="ln_in",
    )(x_p, x_s, g.reshape(1, d), b.reshape(1, d))


def _proj_qa_kernel(x_ref, w_ref, g_ref, o_ref):
    o_ref[...] = _rms_norm_rows(_dot(x_ref[...], w_ref[...]), g_ref[...]).astype(BF16)


def _proj_kv_kernel(x_ref, wl_ref, wr_ref, g_ref, cos_ref, sin_ref, *refs, n_p):
    latp_ref, lats_ref, krp_ref, krs_ref, latb_ref, krb_ref = refs[-6:]
    i = pl.program_id(0)
    x = x_ref[...]
    lat = _rms_norm_rows(_dot(x, wl_ref[...]), g_ref[...])
    kr = _rope_rows(_dot(x, wr_ref[...]), cos_ref[...], sin_ref[...])
    _store_split(i, n_p, latp_ref, lats_ref, lat)
    _store_split(i, n_p, krp_ref, krs_ref, kr)
    latb_ref[...] = lat.astype(BF16)
    krb_ref[...] = kr.astype(BF16)


def _proj_sb_kernel(x_ref, wq_ref, wk_ref, wv_ref, *refs, n_p):
    kp_ref, ks_ref, vp_ref, vs_ref, q_ref, kb_ref, vb_ref = refs[-7:]
    i = pl.program_id(0)
    x = x_ref[...]
    q_ref[...] = (_dot(x, wq_ref[...]) * SB_QSCALE).astype(BF16)
    k = _dot(x, wk_ref[...])
    _store_split(i, n_p, kp_ref, ks_ref, k)
    kb_ref[...] = k.astype(BF16)
    v = _dot(x, wv_ref[...])
    _store_split(i, n_p, vp_ref, vs_ref, v)
    vb_ref[...] = v.astype(BF16)


def _proj_gate_kernel(x_ref, w_ref, b_ref, o_ref, *, tn):
    x = x_ref[...]
    for c in range(o_ref.shape[1] // tn):
        cols = slice(c * tn, (c + 1) * tn)
        o_ref[:, cols] = _sigmoid(_dot(x, w_ref[:, cols]) + b_ref[:, cols]).astype(BF16)


def _proj_qa(xb, w, g, tm):
    t, d = xb.shape
    n = w.shape[1]
    return pl.pallas_call(
        _proj_qa_kernel,
        grid=(t // tm,),
        in_specs=[pl.BlockSpec((tm, d), lambda i: (i, 0)),
                  pl.BlockSpec((d, n), lambda i: (0, 0)),
                  pl.BlockSpec((1, n), lambda i: (0, 0))],
        out_specs=pl.BlockSpec((tm, n), lambda i: (i, 0)),
        out_shape=jax.ShapeDtypeStruct((t, n), BF16),
        compiler_params=_cparams(("parallel",)),
        name="proj_qa",
    )(xb, w, g.reshape(1, n))


def _cache_outputs(widths, tm, tp, ts, layer, depth, prev, n_inputs):
    n_p = tp // tm
    out_specs, out_shape = [], []
    for w in widths:
        out_specs += [
            pl.BlockSpec((None, tm, w), lambda i: (layer, jnp.minimum(i, n_p - 1), 0)),
            pl.BlockSpec((None, tm, w), lambda i: (layer, jnp.maximum(i - n_p, 0), 0))]
        out_shape += [jax.ShapeDtypeStruct((depth, tp, w), F32),
                      jax.ShapeDtypeStruct((depth, ts, w), F32)]
    extra = list(prev)
    extra_specs = [pl.BlockSpec(memory_space=pl.ANY) for _ in extra]
    aliases = {n_inputs + k: k for k in range(len(extra))}
    return extra, extra_specs, out_specs, out_shape, aliases


def _proj_kv(xb, w_lat, w_kr, g, cos, sin, tm, tp, layer, depth, prev):
    t, d = xb.shape
    c = w_lat.shape[1]
    r = w_kr.shape[1]
    rows = lambda w: pl.BlockSpec((tm, w), lambda i: (i, 0))
    full = lambda a, b: pl.BlockSpec((a, b), lambda i: (0, 0))
    extra, extra_specs, out_specs, out_shape, aliases = _cache_outputs(
        (c, r), tm, tp, t - tp, layer, depth, prev, 6)
    return pl.pallas_call(
        functools.partial(_proj_kv_kernel, n_p=tp // tm),
        grid=(t // tm,),
        in_specs=[rows(d), full(d, c), full(d, r), full(1, c), rows(r // 2), rows(r // 2)]
                 + extra_specs,
        out_specs=out_specs + [rows(c), rows(r)],
        out_shape=out_shape + [jax.ShapeDtypeStruct((t, c), BF16),
                               jax.ShapeDtypeStruct((t, r), BF16)],
        input_output_aliases=aliases,
        compiler_params=_cparams(("arbitrary",)),
        name="proj_kv",
    )(xb, w_lat, w_kr, g.reshape(1, c), cos, sin, *extra)


def _proj_sb(xb, w_q, w_k, w_v, tm, tp, layer, depth, prev):
    t, d = xb.shape
    n = w_q.shape[1]
    rows = lambda w: pl.BlockSpec((tm, w), lambda i: (i, 0))
    weight = _resident((d, n))
    bf16 = jax.ShapeDtypeStruct((t, n), BF16)
    extra, extra_specs, out_specs, out_shape, aliases = _cache_outputs(
        (n, n), tm, tp, t - tp, layer, depth, prev, 4)
    return pl.pallas_call(
        functools.partial(_proj_sb_kernel, n_p=tp // tm),
        grid=(t // tm,),
        in_specs=[rows(d), weight, weight, weight] + extra_specs,
        out_specs=out_specs + [rows(n), rows(n), rows(n)],
        out_shape=out_shape + [bf16, bf16, bf16],
        input_output_aliases=aliases,
        compiler_params=_cparams(("arbitrary",)),
        name="proj_sb",
    )(xb, w_q, w_k, w_v, *extra)


def _proj_gates(xb, w, bias, tm):
    t, d = xb.shape
    n = w.shape[1]
    tn = _pick_tile(n, PROJ_COL_TILE, 128)
    return pl.pallas_call(
        functools.partial(_proj_gate_kernel, tn=tn),
        grid=(t // tm,),
        in_specs=[pl.BlockSpec((tm, d), lambda i: (i, 0)), _resident((d, n)),
                  pl.BlockSpec((1, n), lambda i: (0, 0))],
        out_specs=pl.BlockSpec((tm, n), lambda i: (i, 0)),
        out_shape=jax.ShapeDtypeStruct((t, n), BF16),
        compiler_params=_cparams(("parallel",)),
        name="proj_gates",
    )(xb, w, bias.reshape(1, n))


def _q_heads_kernel(qa_ref, wn_ref, wr_ref, wuk_ref, cos_ref, sin_ref, qt_ref, ql_ref, qr_ref,
                    *, n_p):
    i = pl.program_id(0)
    qa = qa_ref[...]
    q_nope = _dot(qa, wn_ref[...])
    q_rope = _dot(qa, wr_ref[...])
    cos = cos_ref[...]
    sin = sin_ref[...]
    roped = [_rope_rows(q_rope[:, h * ROPE_DIM:(h + 1) * ROPE_DIM], cos, sin) * MLA_QSCALE
             for h in range(MLA_HEADS)]

    @pl.when(i < n_p)
    def _():
        pad = jnp.zeros((q_nope.shape[0], MLA_QK_DIM - NOPE_DIM - ROPE_DIM), F32)
        for h in range(MLA_HEADS):
            nope = q_nope[:, h * NOPE_DIM:(h + 1) * NOPE_DIM] * MLA_QSCALE
            qt_ref[h] = jnp.concatenate([nope, roped[h], pad], axis=-1).T.astype(BF16)

    @pl.when(i >= n_p)
    def _():
        for h in range(MLA_HEADS):
            nope = q_nope[:, h * NOPE_DIM:(h + 1) * NOPE_DIM].astype(BF16)
            ql_ref[h] = (_dot(nope, wuk_ref[h]) * MLA_QSCALE).astype(BF16)
            qr_ref[h] = roped[h].astype(BF16)


def _q_heads(qa_n, w_nope, w_rope, wuk_h, cos, sin, tm, tp):
    t, ql = qa_n.shape
    h, _, c = wuk_h.shape
    half = ROPE_DIM // 2
    n_p = tp // tm
    return pl.pallas_call(
        functools.partial(_q_heads_kernel, n_p=n_p),
        grid=(t // tm,),
        in_specs=[pl.BlockSpec((tm, ql), lambda i: (i, 0)),
                  pl.BlockSpec((ql, h * NOPE_DIM), lambda i: (0, 0)),
                  pl.BlockSpec((ql, h * ROPE_DIM), lambda i: (0, 0)),
                  pl.BlockSpec((h, NOPE_DIM, c), lambda i: (0, 0, 0)),
                  pl.BlockSpec((tm, half), lambda i: (i, 0)),
                  pl.BlockSpec((tm, half), lambda i: (i, 0))],
        out_specs=[pl.BlockSpec((h, MLA_QK_DIM, tm), lambda i: (0, 0, jnp.minimum(i, n_p - 1))),
                   pl.BlockSpec((h, tm, c), lambda i: (0, jnp.maximum(i - n_p, 0), 0)),
                   pl.BlockSpec((h, tm, ROPE_DIM), lambda i: (0, jnp.maximum(i - n_p, 0), 0))],
        out_shape=[jax.ShapeDtypeStruct((h, MLA_QK_DIM, tp), BF16),
                   jax.ShapeDtypeStruct((h, t - tp, c), BF16),
                   jax.ShapeDtypeStruct((h, t - tp, ROPE_DIM), BF16)],
        compiler_params=_cparams(("arbitrary",)),
        name="q_heads",
    )(qa_n, w_nope, w_rope, wuk_h, cos, sin)


def _kv_heads_kernel(lat_ref, kr_ref, wk_ref, wv_ref, k_ref, v_ref):
    lat = lat_ref[...]
    k_nope = _dot(lat, wk_ref[...]).astype(BF16)
    kr = kr_ref[...]
    tail = jnp.concatenate(
        [kr, jnp.zeros((kr.shape[0], MLA_QK_DIM - NOPE_DIM - ROPE_DIM), BF16)], axis=-1)
    for h in range(MLA_HEADS):
        k_ref[:, h * MLA_QK_DIM:h * MLA_QK_DIM + NOPE_DIM] = k_nope[:, h * NOPE_DIM:(h + 1) * NOPE_DIM]
        k_ref[:, h * MLA_QK_DIM + NOPE_DIM:(h + 1) * MLA_QK_DIM] = tail
    v_ref[...] = _dot(lat, wv_ref[...]).astype(BF16)


def _kv_heads(lat_b, kr_b, w_uk, w_uv, tm, tp):
    c = lat_b.shape[1]
    n = w_uk.shape[1]
    rows = lambda w: pl.BlockSpec((tm, w), lambda i: (i, 0))
    kw = MLA_HEADS * MLA_QK_DIM
    return pl.pallas_call(
        _kv_heads_kernel,
        grid=(tp // tm,),
        in_specs=[rows(c), rows(ROPE_DIM), _resident((c, n)), _resident((c, n))],
        out_specs=[rows(kw), rows(n)],
        out_shape=[jax.ShapeDtypeStruct((tp, kw), BF16), jax.ShapeDtypeStruct((tp, n), BF16)],
        compiler_params=_cparams(("parallel",)),
        name="kv_heads",
    )(lat_b, kr_b, w_uk, w_uv)


def _mla_init(m_ref, l_ref, acc_ref):
    m_ref[...] = jnp.full(m_ref.shape, NEG_BIG, F32)
    l_ref[...] = jnp.zeros(l_ref.shape, F32)
    acc_ref[...] = jnp.zeros(acc_ref.shape, F32)


def _mla_softmax(masked, q_pos0, k_pos0, m_ref, l_ref, s_ref, p_ref, a_ref, tq, tk):
    rows = MLA_HEADS * tq
    if masked:
        k_chunk = (k_pos0 + lax.broadcasted_iota(jnp.int32, (tk, 1), 0)) // CHUNK
    for c in range(rows // MLA_SOFTMAX_COLS):
        cols = slice(c * MLA_SOFTMAX_COLS, (c + 1) * MLA_SOFTMAX_COLS)
        s = s_ref[0:tk, cols]
        if masked:
            lane = c * MLA_SOFTMAX_COLS + lax.broadcasted_iota(jnp.int32, (1, MLA_SOFTMAX_COLS), 1)
            q_chunk = (q_pos0 + lax.rem(lane, tq)) // CHUNK
            s = jnp.where(k_chunk <= q_chunk, s, NEG_BIG)
        m_prev = m_ref[:, cols]
        m_new = jnp.maximum(m_prev, jnp.max(s, axis=0, keepdims=True))
        alpha = jnp.exp2(m_prev - m_new)
        p = jnp.exp2(s - m_new)
        l_ref[:, cols] = alpha * l_ref[:, cols] + jnp.sum(p, axis=0, keepdims=True)
        m_ref[:, cols] = m_new
        a_ref[:, cols] = alpha
        p_ref[0:tk, cols] = p.astype(BF16)


def _dot_tn(a, b):
    return lax.dot_general(a, b, (((0,), (0,)), ((), ())), preferred_element_type=F32)


def _mla_latent_tile(ql_ref, qr_ref, klat, kr, masked, q_pos0, k_pos0,
                     m_ref, l_ref, acc_ref, s_ref, p_ref, a_ref, tq, tk):
    rows = MLA_HEADS * tq
    q_lat = ql_ref[...].reshape(rows, ql_ref.shape[-1])
    q_rope = qr_ref[...].reshape(rows, qr_ref.shape[-1])
    s_ref[...] = _dot_nt(klat, q_lat) + _dot_nt(kr, q_rope)
    _mla_softmax(masked, q_pos0, k_pos0, m_ref, l_ref, s_ref, p_ref, a_ref, tq, tk)
    acc_ref[...] = a_ref[...] * acc_ref[...] + _dot_tn(klat, p_ref[...])


def _mla_latent_finalize(wuv_ref, o_ref, l_ref, acc_ref, tq):
    inv = 1.0 / l_ref[...]
    for h in range(MLA_HEADS):
        cols = slice(h * tq, (h + 1) * tq)
        ctx_t = (acc_ref[:, cols] * inv[:, cols]).astype(BF16)
        o_ref[:, h * V_DIM:(h + 1) * V_DIM] = _dot_tn(ctx_t, wuv_ref[h]).astype(BF16)


def _mla_heads_tile(qt_ref, k_ref, v_ref, masked, q_pos0, k_pos0,
                    m_ref, l_ref, acc_ref, s_ref, p_ref, a_ref, tq, tk):
    for h in range(MLA_HEADS):
        cols = slice(h * tq, (h + 1) * tq)
        s_ref[0:tk, cols] = _dot(k_ref[0:tk, h * MLA_QK_DIM:(h + 1) * MLA_QK_DIM], qt_ref[h])
    _mla_softmax(masked, q_pos0, k_pos0, m_ref, l_ref, s_ref, p_ref, a_ref, tq, tk)
    for h in range(MLA_HEADS):
        cols = slice(h * tq, (h + 1) * tq)
        pv = _dot_tn(v_ref[0:tk, h * V_DIM:(h + 1) * V_DIM], p_ref[0:tk, cols])
        acc_ref[h] = a_ref[:, cols] * acc_ref[h] + pv


def _mla_heads_finalize(o_ref, l_ref, acc_ref, tq):
    inv = 1.0 / l_ref[...]
    for h in range(MLA_HEADS):
        cols = slice(h * tq, (h + 1) * tq)
        o_ref[:, h * V_DIM:(h + 1) * V_DIM] = (acc_ref[h] * inv[:, cols]).T.astype(BF16)


def _mla_prompt_kernel(qb_ref, kb_ref, qp_ref, kp_ref, fl_ref,
                       qt_ref, k_ref, v_ref, o_ref,
                       m_ref, l_ref, acc_ref, s_ref, p_ref, a_ref, *, tq, tk):
    n = pl.program_id(0)
    scratch = (m_ref, l_ref, acc_ref, s_ref, p_ref, a_ref)

    @pl.when(fl_ref[n] % 2 == 1)
    def _():
        _mla_init(m_ref, l_ref, acc_ref)

    @pl.when(fl_ref[n] < 2)
    def _():
        _mla_heads_tile(qt_ref, k_ref, v_ref, False, 0, 0, *scratch, tq, tk)

    @pl.when(jnp.logical_and(fl_ref[n] >= 2, fl_ref[n] < 4))
    def _():
        _mla_heads_tile(qt_ref, k_ref, v_ref, True, qp_ref[n], kp_ref[n], *scratch, tq, tk)
        _mla_heads_finalize(o_ref, l_ref, acc_ref, tq)

    @pl.when(fl_ref[n] >= 4)
    def _():
        _mla_heads_tile(qt_ref, k_ref, v_ref, True, qp_ref[n], kp_ref[n], *scratch, tq, tk // 2)
        _mla_heads_finalize(o_ref, l_ref, acc_ref, tq)


def _mla_prompt(q_t, k_heads, v, batch, seq):
    h = q_t.shape[0]
    tq = _pick_tile(seq, MLA_Q_TILE, CHUNK)
    tk = _pick_tile(seq, MLA_K_TILE, CHUNK)
    nq, nk = seq // tq, seq // tk
    qb, kb, qp, kp, fl = [], [], [], [], []
    for b in range(batch):
        for qi in range(nq):
            last = ((qi + 1) * tq - 1) // tk
            for kj in range(last + 1):
                qb.append(b * nq + qi)
                kb.append(b * nk + kj)
                qp.append(qi * tq)
                kp.append(kj * tk)
                half = kj == last and (qi + 1) * tq - last * tk <= tk // 2
                fl.append((1 if kj == 0 else 0) + (2 if kj == last else 0) + (4 if half else 0))
    tabs = [jnp.asarray(np.asarray(a, np.int32)) for a in (qb, kb, qp, kp, fl)]
    rows = h * tq
    grid_spec = pltpu.PrefetchScalarGridSpec(
        num_scalar_prefetch=5,
        grid=(len(qb),),
        in_specs=[
            pl.BlockSpec((h, MLA_QK_DIM, tq), lambda n, qb, kb, qp, kp, fl: (0, 0, qb[n])),
            pl.BlockSpec((tk, h * MLA_QK_DIM), lambda n, qb, kb, qp, kp, fl: (kb[n], 0)),
            pl.BlockSpec((tk, h * V_DIM), lambda n, qb, kb, qp, kp, fl: (kb[n], 0)),
        ],
        out_specs=pl.BlockSpec((tq, h * V_DIM), lambda n, qb, kb, qp, kp, fl: (qb[n], 0)),
        scratch_shapes=[pltpu.VMEM((1, rows), F32), pltpu.VMEM((1, rows), F32),
                        pltpu.VMEM((h, V_DIM, tq), F32), pltpu.VMEM((tk, rows), F32),
                        pltpu.VMEM((tk, rows), BF16), pltpu.VMEM((1, rows), F32)],
    )
    return pl.pallas_call(
        functools.partial(_mla_prompt_kernel, tq=tq, tk=tk),
        grid_spec=grid_spec,
        out_shape=jax.ShapeDtypeStruct((batch * seq, h * V_DIM), BF16),
        compiler_params=_cparams(("arbitrary",)),
        name="mla_prompt",
    )(*tabs, q_t, k_heads, v)


def _mla_sample_kernel(ql_ref, qr_ref, pl_ref, pr_ref, nl_ref, nr_ref, wuv_ref, o_ref,
                       m_ref, l_ref, acc_ref, a_ref, s_ref, p_ref, sn_ref, pn_ref,
                       *, tq, tk, n_past, n_kt):
    j = pl.program_id(1)

    @pl.when(j == 0)
    def _():
        _mla_init(m_ref, l_ref, acc_ref)

    @pl.when(j < n_kt)
    def _():
        _mla_latent_tile(ql_ref, qr_ref, pl_ref[...].astype(BF16), pr_ref[...].astype(BF16),
                         False, 0, 0, m_ref, l_ref, acc_ref, s_ref, p_ref, a_ref, tq, tk)

    @pl.when(j == n_kt)
    def _():
        _mla_latent_tile(ql_ref, qr_ref, nl_ref[...], nr_ref[...], True, n_past, n_past,
                         m_ref, l_ref, acc_ref, sn_ref, pn_ref, a_ref, tq, tq)
        _mla_latent_finalize(wuv_ref, o_ref, l_ref, acc_ref, tq)


def _mla_sample(q_lat, q_rope, past_lat, past_kr, layer, k_lat, k_rope, w_uv,
                row0, dec_batch, dec_seq):
    h, _, c = q_lat.shape
    n_past = past_lat.shape[2]
    tk = _pick_tile(n_past, MLA_K_TILE, 128)
    n_kt = n_past // tk
    blk0 = row0 // dec_seq
    rows = h * dec_seq
    past_idx = lambda b, j: (layer, b, jnp.minimum(j, n_kt - 1), 0)
    return pl.pallas_call(
        functools.partial(_mla_sample_kernel, tq=dec_seq, tk=tk, n_past=n_past, n_kt=n_kt),
        grid=(dec_batch, n_kt + 1),
        in_specs=[
            pl.BlockSpec((h, dec_seq, c), lambda b, j: (0, b, 0)),
            pl.BlockSpec((h, dec_seq, ROPE_DIM), lambda b, j: (0, b, 0)),
            pl.BlockSpec((None, None, tk, c), past_idx),
            pl.BlockSpec((None, None, tk, ROPE_DIM), past_idx),
            pl.BlockSpec((dec_seq, c), lambda b, j: (blk0 + b, 0)),
            pl.BlockSpec((dec_seq, ROPE_DIM), lambda b, j: (blk0 + b, 0)),
            pl.BlockSpec((h, c, V_DIM), lambda b, j: (0, 0, 0)),
        ],
        out_specs=pl.BlockSpec((dec_seq, h * V_DIM), lambda b, j: (b, 0)),
        out_shape=jax.ShapeDtypeStruct((dec_batch * dec_seq, h * V_DIM), BF16),
        scratch_shapes=[pltpu.VMEM((1, rows), F32), pltpu.VMEM((1, rows), F32),
                        pltpu.VMEM((c, rows), F32), pltpu.VMEM((1, rows), F32),
                        pltpu.VMEM((tk, rows), F32), pltpu.VMEM((tk, rows), BF16),
                        pltpu.VMEM((dec_seq, rows), F32), pltpu.VMEM((dec_seq, rows), BF16)],
        compiler_params=_cparams(("parallel", "arbitrary")),
        name="mla_sample",
    )(q_lat, q_rope, past_lat, past_kr, k_lat, k_rope, w_uv)


def _scan_matrix(sub):
    j = np.arange(sub)[:, None]
    s = np.arange(sub)[None, :]
    u = (j > s).astype(np.float32)
    return jnp.asarray(np.concatenate([u, u], axis=0), dtype=BF16)


_SbChunk = collections.namedtuple("_SbChunk", "q k v before carry_idx acc_idx")


def _sb_scores(qc, kc, before):
    z = _dot_nt(qc, kc)
    softplus = jnp.log(1.0 + jnp.exp2(-jnp.abs(z))) * LOG2E
    log_beta = jnp.minimum(z, 0.0) - softplus
    log_rest = log_beta - z
    if before is not None:
        log_rest = jnp.where(before, log_rest, 0.0)
    hi = log_rest.astype(BF16)
    lo = (log_rest - hi.astype(F32)).astype(BF16)
    return (log_beta, jnp.concatenate([hi, lo], axis=-1),
            jnp.sum(log_rest, axis=-1, keepdims=True))


def _sb_weights(log_beta, hi_lo, u2, carry, before):
    a = jnp.exp2(log_beta + _dot(hi_lo, u2) + carry)
    if before is not None:
        a = jnp.where(before, a, 0.0)
    return a.astype(BF16)


def _sb_pipeline(chunks, u2, acc_ref, carry_ref):
    scored = None
    weighted = None
    for j in range(len(chunks) + 2):
        new_scored = None
        if j < len(chunks):
            ch = chunks[j]
            new_scored = (ch,) + _sb_scores(ch.q(), ch.k(), ch.before)
        new_weighted = None
        if scored is not None:
            ch, log_beta, hi_lo, row_sum = scored
            carry = carry_ref[ch.carry_idx]
            new_weighted = (ch, _sb_weights(log_beta, hi_lo, u2, carry, ch.before))
            carry_ref[ch.carry_idx] = carry + row_sum
        if weighted is not None:
            ch, a = weighted
            acc_ref[ch.acc_idx] += _dot(a, ch.v())
        scored, weighted = new_scored, new_weighted


def _sb_tile(q_ref, k_ref, v_ref, u2, acc_ref, carry_ref, diagonal, tq, tk, sub, rq):
    chunks = []
    for c in reversed(range(tk // sub)):
        k0, k1 = c * sub, (c + 1) * sub
        for r in range(tq // rq):
            r0, r1 = r * rq, (r + 1) * rq
            before = None
            if diagonal:
                if k0 >= r1 - 1:
                    continue
                if k1 > r0:
                    before = ((k0 + lax.broadcasted_iota(jnp.int32, (1, sub), 1))
                              < (r0 + lax.broadcasted_iota(jnp.int32, (rq, 1), 0)))
            for h in range(SB_HEADS):
                cols = slice(h * SB_DIM, (h + 1) * SB_DIM)
                rows, keys = slice(r0, r1), slice(k0, k1)
                chunks.append(_SbChunk(
                    q=functools.partial(lambda rr, cc: q_ref[rr, cc], rows, cols),
                    k=functools.partial(lambda kk, cc: k_ref[kk, cc], keys, cols),
                    v=functools.partial(lambda kk, cc: v_ref[kk, cc], keys, cols),
                    before=before, carry_idx=(h, rows, slice(None)), acc_idx=(rows, cols)))
    _sb_pipeline(chunks, u2, acc_ref, carry_ref)


def _sb_prompt_kernel(q_ref, k_ref, v_ref, u_ref, k_hbm, v_hbm, o_ref,
                      acc_ref, carry_ref, kbuf, vbuf, sem, *, nq, tq, sub, rq):
    g = pl.program_id(0)
    n_old = lax.rem(g, nq) * (tq // sub)

    def tile_copies(i, slot):
        start = g * tq - (i + 1) * sub
        return (pltpu.make_async_copy(k_hbm.at[pl.ds(start, sub), :], kbuf.at[slot], sem.at[0, slot]),
                pltpu.make_async_copy(v_hbm.at[pl.ds(start, sub), :], vbuf.at[slot], sem.at[1, slot]))

    def start_tile(i, slot):
        for copy in tile_copies(i, slot):
            copy.start()

    def wait_tile(i, slot):
        for copy in tile_copies(i, slot):
            copy.wait()

    def any_alive():
        return (jnp.max(carry_ref[...]) > SB_DEAD_LOG2).astype(jnp.int32)

    @pl.when(n_old > 0)
    def _():
        start_tile(0, 0)

    acc_ref[...] = jnp.zeros(acc_ref.shape, F32)
    carry_ref[...] = jnp.zeros(carry_ref.shape, F32)
    _sb_tile(q_ref, k_ref, v_ref, u_ref[...], acc_ref, carry_ref, True, tq, tq, sub, rq)

    def tile_step(state):
        i, _ = state
        slot = lax.rem(i, 2)
        wait_tile(i, slot)

        @pl.when(i + 1 < n_old)
        def _():
            start_tile(i + 1, 1 - slot)

        for r in range(tq // rq):
            rows = slice(r * rq, (r + 1) * rq)

            @pl.when(jnp.max(carry_ref[:, rows, :]) > SB_DEAD_LOG2)
            def _(rows=rows):
                chunks = []
                for h in range(SB_HEADS):
                    cols = slice(h * SB_DIM, (h + 1) * SB_DIM)
                    chunks.append(_SbChunk(
                        q=functools.partial(lambda rr, cc: q_ref[rr, cc], rows, cols),
                        k=functools.partial(lambda cc: kbuf[slot, :, cc], cols),
                        v=functools.partial(lambda cc: vbuf[slot, :, cc], cols),
                        before=None, carry_idx=(h, rows, slice(None)), acc_idx=(rows, cols)))
                _sb_pipeline(chunks, u_ref[...], acc_ref, carry_ref)
        return i + 1, any_alive()

    tiles_done, _ = lax.while_loop(
        lambda state: jnp.logical_and(state[0] < n_old, state[1] == 1),
        tile_step, (jnp.int32(0), any_alive()))

    @pl.when(tiles_done < n_old)
    def _():
        wait_tile(tiles_done, lax.rem(tiles_done, 2))

    o_ref[...] = acc_ref[...].astype(BF16)


def _sb_prompt(q, k, v, batch, seq):
    width = q.shape[1]
    tq = _pick_tile(seq, SB_TILE, 128)
    sub = _pick_tile(tq, SB_SUB, 128)
    rq = _pick_tile(tq, SB_ROWS, 16)
    nq = seq // tq
    rows = pl.BlockSpec((tq, width), lambda g: (g, 0))
    in_hbm = pl.BlockSpec(memory_space=pl.ANY)
    return pl.pallas_call(
        functools.partial(_sb_prompt_kernel, nq=nq, tq=tq, sub=sub, rq=rq),
        grid=(batch * nq,),
        in_specs=[rows, rows, rows, pl.BlockSpec((2 * sub, sub), lambda g: (0, 0)),
                  in_hbm, in_hbm],
        out_specs=rows,
        out_shape=jax.ShapeDtypeStruct((batch * seq, width), BF16),
        scratch_shapes=[pltpu.VMEM((tq, width), F32), pltpu.VMEM((SB_HEADS, tq, 1), F32),
                        pltpu.VMEM((2, sub, width), BF16), pltpu.VMEM((2, sub, width), BF16),
                        pltpu.SemaphoreType.DMA((2, 2))],
        compiler_params=_cparams(("arbitrary",)),
        name="sb_prompt",
    )(q, k, v, _scan_matrix(sub), k, v)


def _sb_sample_kernel(q_ref, nk_ref, nv_ref, un_ref, up_ref, pk_hbm, pv_hbm, o_ref,
                      acc_ref, carry_ref, kbuf, vbuf, sem, *, layer, tq, tk, n_kt):
    b = pl.program_id(0)

    def tile_copies(i, slot):
        start = (n_kt - 1 - i) * tk
        copies = []
        for h in range(SB_HEADS):
            copies.append(pltpu.make_async_copy(
                pk_hbm.at[layer, b, pl.ds(start, tk), h, :], kbuf.at[slot, h], sem.at[0, slot]))
            copies.append(pltpu.make_async_copy(
                pv_hbm.at[layer, b, pl.ds(start, tk), h, :], vbuf.at[slot, h], sem.at[1, slot]))
        return copies

    def start_tile(i, slot):
        for copy in tile_copies(i, slot):
            copy.start()

    def wait_tile(i, slot):
        for copy in tile_copies(i, slot):
            copy.wait()

    def alive():
        return (jnp.max(carry_ref[...]) > SB_DEAD_LOG2).astype(jnp.int32)

    start_tile(0, 0)

    acc_ref[...] = jnp.zeros(acc_ref.shape, F32)
    carry_ref[...] = jnp.zeros(carry_ref.shape, F32)
    before = (lax.broadcasted_iota(jnp.int32, (1, tq), 1)
              < lax.broadcasted_iota(jnp.int32, (tq, 1), 0))
    chunks = []
    for h in range(SB_HEADS):
        cols = slice(h * SB_DIM, (h + 1) * SB_DIM)
        chunks.append(_SbChunk(
            q=functools.partial(lambda cc: q_ref[:, cc], cols),
            k=functools.partial(lambda cc: nk_ref[:, cc], cols),
            v=functools.partial(lambda cc: nv_ref[:, cc], cols),
            before=before, carry_idx=(h, slice(None), slice(None)), acc_idx=(slice(None), cols)))
    _sb_pipeline(chunks, un_ref[...], acc_ref, carry_ref)

    def tile_step(state):
        i, _ = state
        slot = lax.rem(i, 2)
        wait_tile(i, slot)

        @pl.when(i + 1 < n_kt)
        def _():
            start_tile(i + 1, 1 - slot)

        chunks = []
        for h in range(SB_HEADS):
            cols = slice(h * SB_DIM, (h + 1) * SB_DIM)
            chunks.append(_SbChunk(
                q=functools.partial(lambda cc: q_ref[:, cc], cols),
                k=functools.partial(lambda hh: kbuf[slot, hh].astype(BF16), h),
                v=functools.partial(lambda hh: vbuf[slot, hh].astype(BF16), h),
                before=None, carry_idx=(h, slice(None), slice(None)), acc_idx=(slice(None), cols)))
        _sb_pipeline(chunks, up_ref[...], acc_ref, carry_ref)
        return i + 1, alive()

    tiles_done, _ = lax.while_loop(
        lambda state: jnp.logical_and(state[0] < n_kt, state[1] == 1),
        tile_step, (jnp.int32(0), alive()))

    @pl.when(tiles_done < n_kt)
    def _():
        wait_tile(tiles_done, lax.rem(tiles_done, 2))

    o_ref[...] = acc_ref[...].astype(BF16)


def _sb_sample(q, past_k, past_v, layer, k, v, row0, dec_batch, dec_seq):
    width = q.shape[1]
    n_past = past_k.shape[2]
    tk = _pick_tile(n_past, SB_SUB, 128)
    n_kt = n_past // tk
    blk0 = row0 // dec_seq
    new_rows = pl.BlockSpec((dec_seq, width), lambda b: (blk0 + b, 0))
    in_hbm = pl.BlockSpec(memory_space=pl.ANY)
    return pl.pallas_call(
        functools.partial(_sb_sample_kernel, layer=layer, tq=dec_seq, tk=tk, n_kt=n_kt),
        grid=(dec_batch,),
        in_specs=[
            new_rows, new_rows, new_rows,
            pl.BlockSpec((2 * dec_seq, dec_seq), lambda b: (0, 0)),
            pl.BlockSpec((2 * tk, tk), lambda b: (0, 0)),
            in_hbm, in_hbm,
        ],
        out_specs=pl.BlockSpec((dec_seq, width), lambda b: (b, 0)),
        out_shape=jax.ShapeDtypeStruct((dec_batch * dec_seq, width), BF16),
        scratch_shapes=[pltpu.VMEM((dec_seq, width), F32),
                        pltpu.VMEM((SB_HEADS, dec_seq, 1), F32),
                        pltpu.VMEM((2, SB_HEADS, tk, SB_DIM), F32),
                        pltpu.VMEM((2, SB_HEADS, tk, SB_DIM), F32),
                        pltpu.SemaphoreType.DMA((2, 2))],
        compiler_params=_cparams(("arbitrary",)),
        name="sb_sample",
    )(q, k, v, _scan_matrix(dec_seq), _scan_matrix(tk), past_k, past_v)


def _merge_kernel(oap_ref, oas_ref, obp_ref, obs_ref, wa_ref, wb_ref, g_ref, o_ref,
                  *, n_p, tn):
    i = pl.program_id(0)
    o_a = _load_split(i, n_p, oap_ref, oas_ref)
    o_b = _load_split(i, n_p, obp_ref, obs_ref)
    n = o_ref.shape[1]
    for c in range(n // tn):
        cols = slice(c * tn, (c + 1) * tn)
        a = _dot(o_a, wa_ref[:, cols])
        b = _dot(o_b, wb_ref[:, cols])
        g_a = g_ref[:, cols].astype(F32)
        g_b = g_ref[:, n + c * tn:n + (c + 1) * tn].astype(F32)
        o_ref[:, cols] = (g_a * a + g_b * b).astype(BF16)


def _merge(oa_p, oa_s, ob_p, ob_s, w_a, w_b, gates, tm):
    da = oa_p.shape[1]
    db = ob_p.shape[1]
    t = oa_p.shape[0] + oa_s.shape[0]
    n_p = oa_p.shape[0] // tm
    n = w_a.shape[1]
    tn = _pick_tile(n, COL_TILE, 128)
    return pl.pallas_call(
        functools.partial(_merge_kernel, n_p=n_p, tn=tn),
        grid=(t // tm,),
        in_specs=[*_split_specs(tm, da, n_p),
                  *_split_specs(tm, db, n_p),
                  _resident((da, n)), _resident((db, n)),
                  pl.BlockSpec((tm, 2 * n), lambda i: (i, 0))],
        out_specs=pl.BlockSpec((tm, n), lambda i: (i, 0)),
        out_shape=jax.ShapeDtypeStruct((t, n), BF16),
        compiler_params=_cparams(("parallel",)),
        name="merge",
    )(oa_p, oa_s, ob_p, ob_s, w_a, w_b, gates)


def _res_ln_finalize(z_ref, g_ref, b_ref, o_ref, ob_ref):
    n_tiles, _, tn = z_ref.shape
    width = n_tiles * tn
    total = z_ref[0].sum(axis=-1, keepdims=True)
    for c in range(1, n_tiles):
        total += z_ref[c].sum(axis=-1, keepdims=True)
    mu = total / width
    sq = jnp.square(z_ref[0] - mu).sum(axis=-1, keepdims=True)
    for c in range(1, n_tiles):
        sq += jnp.square(z_ref[c] - mu).sum(axis=-1, keepdims=True)
    rstd = lax.rsqrt(sq / width + LN_EPS)
    for c in range(n_tiles):
        cols = slice(c * tn, (c + 1) * tn)
        y = (z_ref[c] - mu) * rstd * g_ref[:, cols] + b_ref[:, cols]
        o_ref[:, cols] = y
        if ob_ref is not None:
            ob_ref[:, cols] = y.astype(BF16)


def _out_ln_kernel(m_ref, w_ref, x_ref, g_ref, b_ref, o_ref, ob_ref, z_ref, *, alpha):
    m = m_ref[...]
    n_tiles, _, tn = z_ref.shape
    for c in range(n_tiles):
        cols = slice(c * tn, (c + 1) * tn)
        z_ref[c] = alpha * x_ref[:, cols] + _dot(m, w_ref[:, cols])
    _res_ln_finalize(z_ref, g_ref, b_ref, o_ref, ob_ref)


def _out_ln(merged, w, x, g, b, alpha, tm):
    t, d = x.shape
    k = merged.shape[1]
    tn = _pick_tile(d, COL_TILE, 128)
    full_row = pl.BlockSpec((tm, d), lambda i: (i, 0))
    vec = pl.BlockSpec((1, d), lambda i: (0, 0))
    return pl.pallas_call(
        functools.partial(_out_ln_kernel, alpha=alpha),
        grid=(t // tm,),
        in_specs=[pl.BlockSpec((tm, k), lambda i: (i, 0)), _resident((k, d)), full_row, vec, vec],
        out_specs=[full_row, full_row],
        out_shape=[jax.ShapeDtypeStruct((t, d), F32), jax.ShapeDtypeStruct((t, d), BF16)],
        scratch_shapes=[pltpu.VMEM((d // tn, tm, tn), F32)],
        compiler_params=_cparams(("parallel",)),
        name="out_ln",
    )(merged, w, x, g.reshape(1, d), b.reshape(1, d))


def _ffn_ln_kernel(xb_ref, wg_ref, wu_ref, wd_ref, x_ref, g_ref, b_ref, o_ref, ob_ref,
                   acc_ref, *, alpha):
    f = pl.program_id(1)

    @pl.when(f == 0)
    def _():
        acc_ref[...] = jnp.zeros(acc_ref.shape, F32)

    xb = xb_ref[...]
    gate = _dot(xb, wg_ref[...])
    up = _dot(xb, wu_ref[...])
    hidden = (gate * _sigmoid(gate) * up).astype(BF16)
    acc_ref[...] += _dot(hidden, wd_ref[...])

    @pl.when(f == pl.num_programs(1) - 1)
    def _():
        y = _layer_norm_rows(alpha * x_ref[...] + acc_ref[...], g_ref[...], b_ref[...])
        o_ref[...] = y
        ob_ref[...] = y.astype(BF16)


def _ffn_ln(xb, w_gu, w_down, x, g, b, alpha, tm):
    t, d = x.shape
    ff = w_down.shape[0]
    tf = _pick_tile(ff, COL_TILE, 128)
    nf = ff // tf
    full_row = pl.BlockSpec((tm, d), lambda i, f: (i, 0))
    vec = pl.BlockSpec((1, d), lambda i, f: (0, 0))
    return pl.pallas_call(
        functools.partial(_ffn_ln_kernel, alpha=alpha),
        grid=(t // tm, nf),
        in_specs=[full_row,
                  pl.BlockSpec((d, tf), lambda i, f: (0, f)),
                  pl.BlockSpec((d, tf), lambda i, f: (0, f + nf)),
                  pl.BlockSpec((tf, d), lambda i, f: (f, 0)),
                  full_row, vec, vec],
        out_specs=[full_row, full_row],
        out_shape=[jax.ShapeDtypeStruct((t, d), F32), jax.ShapeDtypeStruct((t, d), BF16)],
        scratch_shapes=[pltpu.VMEM((tm, d), F32)],
        compiler_params=_cparams(("parallel", "arbitrary")),
        name="ffn_ln",
    )(xb, w_gu, w_gu, w_down, x, g.reshape(1, d), b.reshape(1, d))


def _ple_ln_kernel(xb_ref, wg_ref, bg_ref, pp_ref, ps_ref, wp_ref, x_ref, g_ref, b_ref,
                   o1_ref, o2_ref, z_ref, *, alpha, n_p, last):
    i = pl.program_id(0)
    xb = xb_ref[...]
    p = _load_split(i, n_p, pp_ref, ps_ref).astype(BF16)
    n_tiles, _, tn = z_ref.shape
    for c in range(n_tiles):
        cols = slice(c * tn, (c + 1) * tn)
        gate = _sigmoid(_dot(xb, wg_ref[:, cols]) + bg_ref[:, cols])
        z_ref[c] = alpha * x_ref[:, cols] + gate * _dot(p, wp_ref[:, cols])

    if not last:
        _res_ln_finalize(z_ref, g_ref, b_ref, o1_ref, o2_ref)
    else:
        @pl.when(i < n_p)
        def _():
            _res_ln_finalize(z_ref, g_ref, b_ref, o1_ref, None)

        @pl.when(i >= n_p)
        def _():
            _res_ln_finalize(z_ref, g_ref, b_ref, o2_ref, None)


def _ple_ln(xb, w_gate, b_gate, p_p, p_s, layer, w_proj, x, g, b, alpha, tm, last):
    t, d = x.shape
    pd = p_p.shape[2]
    tp = p_p.shape[1]
    n_p = tp // tm
    tn = _pick_tile(d, COL_TILE, 128)
    full_row = pl.BlockSpec((tm, d), lambda i: (i, 0))
    vec = pl.BlockSpec((1, d), lambda i: (0, 0))
    if last:
        out_specs = list(_split_specs(tm, d, n_p))
        out_shape = [jax.ShapeDtypeStruct((tp, d), F32), jax.ShapeDtypeStruct((t - tp, d), F32)]
    else:
        out_specs = [full_row, full_row]
        out_shape = [jax.ShapeDtypeStruct((t, d), F32), jax.ShapeDtypeStruct((t, d), BF16)]
    return pl.pallas_call(
        functools.partial(_ple_ln_kernel, alpha=alpha, n_p=n_p, last=last),
        grid=(t // tm,),
        in_specs=[full_row, _resident((d, d)), vec,
                  pl.BlockSpec((None, tm, pd), lambda i: (layer, jnp.minimum(i, n_p - 1), 0)),
                  pl.BlockSpec((None, tm, pd), lambda i: (layer, jnp.maximum(i - n_p, 0), 0)),
                  _resident((pd, d)), full_row, vec, vec],
        out_specs=out_specs,
        out_shape=out_shape,
        scratch_shapes=[pltpu.VMEM((d // tn, tm, tn), F32)],
        compiler_params=_cparams(("arbitrary",)),
        name="ple_ln",
    )(xb, w_gate, b_gate.reshape(1, d), p_p, p_s, w_proj, x, g.reshape(1, d), b.reshape(1, d))


def _cast_kernel(w_ref, *o_refs, bounds):
    w = w_ref[...]
    for o_ref, (lo, hi) in zip(o_refs, bounds):
        o_ref[...] = w[:, lo:hi].astype(BF16)


def _cast_weight(w, layer, col_groups=None):
    _, r, n = w.shape
    bounds = tuple(col_groups) if col_groups else ((0, n),)
    br = _pick_tile(r, max(16, CAST_BLOCK_BYTES // (4 * n)), 16)
    return pl.pallas_call(
        functools.partial(_cast_kernel, bounds=bounds),
        grid=(r // br,),
        in_specs=[pl.BlockSpec((None, br, n), lambda i: (layer, i, 0))],
        out_specs=[pl.BlockSpec((br, hi - lo), lambda i: (i, 0)) for lo, hi in bounds],
        out_shape=[jax.ShapeDtypeStruct((r, hi - lo), BF16) for lo, hi in bounds],
        compiler_params=_cparams(("parallel",)),
        name="cast_weight",
    )(w)


def _rope_tables(pos):
    half = ROPE_DIM // 2
    inv_freq = 1.0 / (ROPE_THETA ** (jnp.arange(half, dtype=F32) * (2.0 / ROPE_DIM)))
    ang = pos.astype(F32)[:, None] * inv_freq[None, :]
    return jnp.cos(ang), jnp.sin(ang)


def kernel(x_prompt, x_sample, cache_mla_latent, cache_mla_krope, cache_sb_k, cache_sb_v,
           p_prompt, p_sample, ln_in_g, ln_in_b, w_in, b_gate, q_a_norm_g, w_q_b,
           kv_a_norm_g, w_kv_b, w_branch_a, w_branch_b, w_out, ln1_g, ln1_b,
           w_ffn_gu, w_ffn_down, ln2_g, ln2_b, w_ple_gate, b_ple_gate, w_ple_proj,
           ln3_g, ln3_b):
    batch, seq, d = x_prompt.shape
    dec_batch, dec_seq, _ = x_sample.shape
    depth = w_in.shape[0]
    n_past = cache_mla_latent.shape[2]
    tp = batch * seq
    ts = dec_batch * dec_seq
    sb_width = SB_HEADS * SB_DIM
    alpha = (2 * depth) ** 0.25

    pos = jnp.concatenate([jnp.tile(jnp.arange(seq, dtype=jnp.int32), batch),
                           jnp.tile(n_past + jnp.arange(dec_seq, dtype=jnp.int32), dec_batch)])
    cos, sin = _rope_tables(pos)

    tm = _pick_tile(math.gcd(tp, ts), ROW_TILE, 16)
    p_p = p_prompt.reshape(depth, tp, -1)
    p_s = p_sample.reshape(depth, ts, -1)

    x, xb = _ln_in(x_prompt.reshape(tp, d), x_sample.reshape(ts, d), ln_in_g, ln_in_b, tm)

    splits = np.cumsum([0, Q_LORA, KV_LORA, ROPE_DIM, sb_width, sb_width, sb_width, 2 * d])
    def cache_arrays(widths):
        return [jnp.zeros((depth, n, w), F32) for w in widths for n in (tp, ts)]

    kv_cache = cache_arrays((KV_LORA, ROPE_DIM))
    sb_cache = cache_arrays((sb_width, sb_width))
    for l in range(depth):
        w_qa, w_lat, w_kr, w_sq, w_sk, w_sv, w_gates = _cast_weight(
            w_in, l, [(int(splits[i]), int(splits[i + 1])) for i in range(7)])
        (wb_a,) = _cast_weight(w_branch_a, l)
        (wb_b,) = _cast_weight(w_branch_b, l)
        (wb_out,) = _cast_weight(w_out, l)
        (wb_gu,) = _cast_weight(w_ffn_gu, l)
        (wb_down,) = _cast_weight(w_ffn_down, l)
        (wb_pg,) = _cast_weight(w_ple_gate, l)
        (wb_pp,) = _cast_weight(w_ple_proj, l)
        wq = w_q_b[l].reshape(Q_LORA, MLA_HEADS, NOPE_DIM + ROPE_DIM)
        wq_nope = wq[..., :NOPE_DIM].reshape(Q_LORA, MLA_HEADS * NOPE_DIM).astype(BF16)
        wq_rope = wq[..., NOPE_DIM:].reshape(Q_LORA, MLA_HEADS * ROPE_DIM).astype(BF16)
        w_kv = w_kv_b[l].reshape(KV_LORA, MLA_HEADS, NOPE_DIM + V_DIM)
        wuk_h = w_kv[..., :NOPE_DIM].transpose(1, 2, 0).astype(BF16)
        wuv_h = w_kv[..., NOPE_DIM:].transpose(1, 0, 2).astype(BF16)
        wuk_all = w_kv[..., :NOPE_DIM].reshape(KV_LORA, MLA_HEADS * NOPE_DIM).astype(BF16)
        wuv_all = w_kv[..., NOPE_DIM:].reshape(KV_LORA, MLA_HEADS * V_DIM).astype(BF16)

        qa_n = _proj_qa(xb, w_qa, q_a_norm_g[l], tm)
        *kv_cache, lat_b, kr_b = _proj_kv(xb, w_lat, w_kr, kv_a_norm_g[l], cos, sin,
                                          tm, tp, l, depth, kv_cache)
        *sb_cache, sbq, sbk_b, sbv_b = _proj_sb(xb, w_sq, w_sk, w_sv, tm, tp, l, depth, sb_cache)
        gates = _proj_gates(xb, w_gates, b_gate[l], tm)

        q_t, q_lat, q_rope = _q_heads(qa_n, wq_nope, wq_rope, wuk_h, cos, sin, tm, tp)
        k_heads, v_heads = _kv_heads(lat_b, kr_b, wuk_all, wuv_all, tm, tp)
        oa_p = _mla_prompt(q_t, k_heads, v_heads, batch, seq)
        oa_s = _mla_sample(q_lat, q_rope, cache_mla_latent, cache_mla_krope, l, lat_b, kr_b,
                           wuv_h, tp, dec_batch, dec_seq)
        ob_p = _sb_prompt(sbq, sbk_b, sbv_b, batch, seq)
        ob_s = _sb_sample(sbq, cache_sb_k, cache_sb_v, l, sbk_b, sbv_b, tp, dec_batch, dec_seq)

        merged = _merge(oa_p, oa_s, ob_p, ob_s, wb_a, wb_b, gates, tm)
        x, xb = _out_ln(merged, wb_out, x, ln1_g[l], ln1_b[l], alpha, tm)
        x, xb = _ffn_ln(xb, wb_gu, wb_down, x, ln2_g[l], ln2_b[l], alpha, tm)
        x, xb = _ple_ln(xb, wb_pg, b_ple_gate[l], p_p, p_s, l, wb_pp, x, ln3_g[l], ln3_b[l],
                        alpha, tm, last=(l == depth - 1))

    lat_p, lat_s, kr_p, kr_s = kv_cache
    k_p, k_s, v_p, v_s = sb_cache
    head_dims = (SB_HEADS, SB_DIM)
    return (x.reshape(batch, seq, d), xb.reshape(dec_batch, dec_seq, d),
            lat_p.reshape(depth, batch, seq, KV_LORA), kr_p.reshape(depth, batch, seq, ROPE_DIM),
            k_p.reshape(depth, batch, seq, *head_dims), v_p.reshape(depth, batch, seq, *head_dims),
            lat_s.reshape(depth, dec_batch, dec_seq, KV_LORA),
            kr_s.reshape(depth, dec_batch, dec_seq, ROPE_DIM),
            k_s.reshape(depth, dec_batch, dec_seq, *head_dims),
            v_s.reshape(depth, dec_batch, dec_seq, *head_dims))
```

```python
import collections
import functools
import math

import numpy as np
import jax
import jax.numpy as jnp
from jax import lax
from jax.experimental import pallas as pl
from jax.experimental.pallas import tpu as pltpu

CHUNK = 64
MLA_HEADS = 16
Q_LORA = 512
KV_LORA = 512
NOPE_DIM = 128
ROPE_DIM = 64
V_DIM = 128
ROPE_THETA = 10000.0
MLA_SCALE = (NOPE_DIM + ROPE_DIM) ** -0.5
SB_HEADS = 8
SB_DIM = 128
SB_SCALE = SB_DIM ** -0.5
LN_EPS = 1e-5
RMS_EPS = 1e-6
LOG2E = 1.4426950408889634
MLA_QSCALE = MLA_SCALE * LOG2E
MLA_QK_DIM = 256
SB_QSCALE = SB_SCALE * LOG2E
SB_DEAD_LOG2 = -160.0

BF16 = jnp.bfloat16
F32 = jnp.float32

V7X_VMEM_LIMIT_BYTES = 56 * 1024 * 1024
ROW_TILE = 512
COL_TILE = 512
PROJ_COL_TILE = 1024
MLA_Q_TILE = 256
MLA_K_TILE = 512
MLA_SOFTMAX_COLS = 128
SB_TILE = 512
SB_SUB = 256
SB_ROWS = 128
CAST_BLOCK_BYTES = 6 * 1024 * 1024
NEG_BIG = -1e30


def _pick_tile(n, target, mult):
    best = None
    for t in range(mult, min(n, target) + 1, mult):
        if n % t == 0:
            best = t
    if best is None:
        return n
    return best


def _cparams(sem):
    return pltpu.CompilerParams(dimension_semantics=sem,
                                vmem_limit_bytes=V7X_VMEM_LIMIT_BYTES)


def _resident(shape):
    return pl.BlockSpec(shape, lambda *_: (0,) * len(shape), pipeline_mode=pl.Buffered(1))


def _dot(a, b):
    return jnp.dot(a, b, preferred_element_type=F32)


def _dot_nt(a, b):
    return lax.dot_general(a, b, (((1,), (1,)), ((), ())), preferred_element_type=F32)


def _layer_norm_rows(z, g, b):
    mu = jnp.mean(z, axis=-1, keepdims=True)
    zc = z - mu
    var = jnp.mean(zc * zc, axis=-1, keepdims=True)
    return zc * lax.rsqrt(var + LN_EPS) * g + b


def _rms_norm_rows(z, g):
    return z * lax.rsqrt(jnp.mean(z * z, axis=-1, keepdims=True) + RMS_EPS) * g


def _sigmoid(z):
    return 1.0 / (1.0 + jnp.exp(-z))


def _rope_rows(x, cos, sin):
    half = ROPE_DIM // 2
    x1 = x[:, :half]
    x2 = x[:, half:]
    return jnp.concatenate([x1 * cos - x2 * sin, x2 * cos + x1 * sin], axis=-1)


def _split_specs(tm, width, n_p):
    return (pl.BlockSpec((tm, width), lambda i, *_: (jnp.minimum(i, n_p - 1), 0)),
            pl.BlockSpec((tm, width), lambda i, *_: (jnp.maximum(i - n_p, 0), 0)))


def _load_split(i, n_p, p_ref, s_ref):
    return jnp.where(i < n_p, p_ref[...], s_ref[...])


def _store_split(i, n_p, p_ref, s_ref, value):
    @pl.when(i < n_p)
    def _():
        p_ref[...] = value

    @pl.when(i >= n_p)
    def _():
        s_ref[...] = value


def _ln_in_kernel(xp_ref, xs_ref, g_ref, b_ref, o_ref, ob_ref, *, n_p):
    x = _load_split(pl.program_id(0), n_p, xp_ref, xs_ref)
    y = _layer_norm_rows(x, g_ref[...], b_ref[...])
    o_ref[...] = y
    ob_ref[...] = y.astype(BF16)


def _ln_in(x_p, x_s, g, b, tm):
    d = x_p.shape[1]
    t = x_p.shape[0] + x_s.shape[0]
    n_p = x_p.shape[0] // tm
    row = pl.BlockSpec((tm, d), lambda i: (i, 0))
    vec = pl.BlockSpec((1, d), lambda i: (0, 0))
    return pl.pallas_call(
        functools.partial(_ln_in_kernel, n_p=n_p),
        grid=(t // tm,),
        in_specs=[*_split_specs(tm, d, n_p), vec, vec],
        out_specs=[row, row],
        out_shape=[jax.ShapeDtypeStruct((t, d), F32), jax.ShapeDtypeStruct((t, d), BF16)],
        compiler_params=_cparams(("parallel",)),
        name="ln_in",
    )(x_p, x_s, g.reshape(1, d), b.reshape(1, d))


def _proj_qa_kernel(x_ref, w_ref, g_ref, o_ref):
    o_ref[...] = _rms_norm_rows(_dot(x_ref[...], w_ref[...]), g_ref[...]).astype(BF16)


def _proj_kv_kernel(x_ref, wl_ref, wr_ref, g_ref, cos_ref, sin_ref, *refs, n_p):
    latp_ref, lats_ref, krp_ref, krs_ref, latb_ref, krb_ref = refs[-6:]
    i = pl.program_id(0)
    x = x_ref[...]
    lat = _rms_norm_rows(_dot(x, wl_ref[...]), g_ref[...])
    kr = _rope_rows(_dot(x, wr_ref[...]), cos_ref[...], sin_ref[...])
    _store_split(i, n_p, latp_ref, lats_ref, lat)
    _store_split(i, n_p, krp_ref, krs_ref, kr)
    latb_ref[...] = lat.astype(BF16)
    krb_ref[...] = kr.astype(BF16)


def _proj_sb_kernel(x_ref, wq_ref, wk_ref, wv_ref, *refs, n_p):
    kp_ref, ks_ref, vp_ref, vs_ref, q_ref, kb_ref, vb_ref = refs[-7:]
    i = pl.program_id(0)
    x = x_ref[...]
    q_ref[...] = (_dot(x, wq_ref[...]) * SB_QSCALE).astype(BF16)
    k = _dot(x, wk_ref[...])
    _store_split(i, n_p, kp_ref, ks_ref, k)
    kb_ref[...] = k.astype(BF16)
    v = _dot(x, wv_ref[...])
    _store_split(i, n_p, vp_ref, vs_ref, v)
    vb_ref[...] = v.astype(BF16)


def _proj_gate_kernel(x_ref, w_ref, b_ref, o_ref, *, tn):
    x = x_ref[...]
    for c in range(o_ref.shape[1] // tn):
        cols = slice(c * tn, (c + 1) * tn)
        o_ref[:, cols] = _sigmoid(_dot(x, w_ref[:, cols]) + b_ref[:, cols]).astype(BF16)


def _proj_qa(xb, w, g, tm):
    t, d = xb.shape
    n = w.shape[1]
    return pl.pallas_call(
        _proj_qa_kernel,
        grid=(t // tm,),
        in_specs=[pl.BlockSpec((tm, d), lambda i: (i, 0)),
                  pl.BlockSpec((d, n), lambda i: (0, 0)),
                  pl.BlockSpec((1, n), lambda i: (0, 0))],
        out_specs=pl.BlockSpec((tm, n), lambda i: (i, 0)),
        out_shape=jax.ShapeDtypeStruct((t, n), BF16),
        compiler_params=_cparams(("parallel",)),
        name="proj_qa",
    )(xb, w, g.reshape(1, n))


def _cache_outputs(widths, tm, tp, ts, layer, depth, prev, n_inputs):
    n_p = tp // tm
    out_specs, out_shape = [], []
    for w in widths:
        out_specs += [
            pl.BlockSpec((None, tm, w), lambda i: (layer, jnp.minimum(i, n_p - 1), 0)),
            pl.BlockSpec((None, tm, w), lambda i: (layer, jnp.maximum(i - n_p, 0), 0))]
        out_shape += [jax.ShapeDtypeStruct((depth, tp, w), F32),
                      jax.ShapeDtypeStruct((depth, ts, w), F32)]
    extra = list(prev)
    extra_specs = [pl.BlockSpec(memory_space=pl.ANY) for _ in extra]
    aliases = {n_inputs + k: k for k in range(len(extra))}
    return extra, extra_specs, out_specs, out_shape, aliases


def _proj_kv(xb, w_lat, w_kr, g, cos, sin, tm, tp, layer, depth, prev):
    t, d = xb.shape
    c = w_lat.shape[1]
    r = w_kr.shape[1]
    rows = lambda w: pl.BlockSpec((tm, w), lambda i: (i, 0))
    full = lambda a, b: pl.BlockSpec((a, b), lambda i: (0, 0))
    extra, extra_specs, out_specs, out_shape, aliases = _cache_outputs(
        (c, r), tm, tp, t - tp, layer, depth, prev, 6)
    return pl.pallas_call(
        functools.partial(_proj_kv_kernel, n_p=tp // tm),
        grid=(t // tm,),
        in_specs=[rows(d), full(d, c), full(d, r), full(1, c), rows(r // 2), rows(r // 2)]
                 + extra_specs,
        out_specs=out_specs + [rows(c), rows(r)],
        out_shape=out_shape + [jax.ShapeDtypeStruct((t, c), BF16),
                               jax.ShapeDtypeStruct((t, r), BF16)],
        input_output_aliases=aliases,
        compiler_params=_cparams(("arbitrary",)),
        name="proj_kv",
    )(xb, w_lat, w_kr, g.reshape(1, c), cos, sin, *extra)


def _proj_sb(xb, w_q, w_k, w_v, tm, tp, layer, depth, prev):
    t, d = xb.shape
    n = w_q.shape[1]
    rows = lambda w: pl.BlockSpec((tm, w), lambda i: (i, 0))
    weight = _resident((d, n))
    bf16 = jax.ShapeDtypeStruct((t, n), BF16)
    extra, extra_specs, out_specs, out_shape, aliases = _cache_outputs(
        (n, n), tm, tp, t - tp, layer, depth, prev, 4)
    return pl.pallas_call(
        functools.partial(_proj_sb_kernel, n_p=tp // tm),
        grid=(t // tm,),
        in_specs=[rows(d), weight, weight, weight] + extra_specs,
        out_specs=out_specs + [rows(n), rows(n), rows(n)],
        out_shape=out_shape + [bf16, bf16, bf16],
        input_output_aliases=aliases,
        compiler_params=_cparams(("arbitrary",)),
        name="proj_sb",
    )(xb, w_q, w_k, w_v, *extra)


def _proj_gates(xb, w, bias, tm):
    t, d = xb.shape
    n = w.shape[1]
    tn = _pick_tile(n, PROJ_COL_TILE, 128)
    return pl.pallas_call(
        functools.partial(_proj_gate_kernel, tn=tn),
        grid=(t // tm,),
        in_specs=[pl.BlockSpec((tm, d), lambda i: (i, 0)), _resident((d, n)),
                  pl.BlockSpec((1, n), lambda i: (0, 0))],
        out_specs=pl.BlockSpec((tm, n), lambda i: (i, 0)),
        out_shape=jax.ShapeDtypeStruct((t, n), BF16),
        compiler_params=_cparams(("parallel",)),
        name="proj_gates",
    )(xb, w, bias.reshape(1, n))


def _q_heads_kernel(qa_ref, wn_ref, wr_ref, wuk_ref, cos_ref, sin_ref, qt_ref, ql_ref, qr_ref,
                    *, n_p, tq):
    i = pl.program_id(0)
    qa = qa_ref[...]
    q_nope = _dot(qa, wn_ref[...])
    q_rope = _dot(qa, wr_ref[...])
    cos = cos_ref[...]
    sin = sin_ref[...]
    roped = [_rope_rows(q_rope[:, h * ROPE_DIM:(h + 1) * ROPE_DIM], cos, sin) * MLA_QSCALE
             for h in range(MLA_HEADS)]

    @pl.when(i < n_p)
    def _():
        pad = jnp.zeros((q_nope.shape[0], MLA_QK_DIM - NOPE_DIM - ROPE_DIM), F32)
        for h in range(MLA_HEADS):
            nope = q_nope[:, h * NOPE_DIM:(h + 1) * NOPE_DIM] * MLA_QSCALE
            q_t = jnp.concatenate([nope, roped[h], pad], axis=-1).T.astype(BF16)
            for j in range(qt_ref.shape[0]):
                qt_ref[j, h] = q_t[:, j * tq:(j + 1) * tq]

    @pl.when(i >= n_p)
    def _():
        for h in range(MLA_HEADS):
            nope = q_nope[:, h * NOPE_DIM:(h + 1) * NOPE_DIM].astype(BF16)
            ql_ref[h] = (_dot(nope, wuk_ref[h]) * MLA_QSCALE).astype(BF16)
            qr_ref[h] = roped[h].astype(BF16)


def _q_heads(qa_n, w_nope, w_rope, wuk_h, cos, sin, tm, tp, tq):
    t, ql = qa_n.shape
    h, _, c = wuk_h.shape
    half = ROPE_DIM // 2
    n_p = tp // tm
    assert tm % tq == 0, (tm, tq)
    per_step = tm // tq
    return pl.pallas_call(
        functools.partial(_q_heads_kernel, n_p=n_p, tq=tq),
        grid=(t // tm,),
        in_specs=[pl.BlockSpec((tm, ql), lambda i: (i, 0)),
                  pl.BlockSpec((ql, h * NOPE_DIM), lambda i: (0, 0)),
                  pl.BlockSpec((ql, h * ROPE_DIM), lambda i: (0, 0)),
                  pl.BlockSpec((h, NOPE_DIM, c), lambda i: (0, 0, 0)),
                  pl.BlockSpec((tm, half), lambda i: (i, 0)),
                  pl.BlockSpec((tm, half), lambda i: (i, 0))],
        out_specs=[pl.BlockSpec((per_step, h, MLA_QK_DIM, tq),
                                lambda i: (jnp.minimum(i, n_p - 1), 0, 0, 0)),
                   pl.BlockSpec((h, tm, c), lambda i: (0, jnp.maximum(i - n_p, 0), 0)),
                   pl.BlockSpec((h, tm, ROPE_DIM), lambda i: (0, jnp.maximum(i - n_p, 0), 0))],
        out_shape=[jax.ShapeDtypeStruct((tp // tq, h, MLA_QK_DIM, tq), BF16),
                   jax.ShapeDtypeStruct((h, t - tp, c), BF16),
                   jax.ShapeDtypeStruct((h, t - tp, ROPE_DIM), BF16)],
        compiler_params=_cparams(("arbitrary",)),
        name="q_heads",
    )(qa_n, w_nope, w_rope, wuk_h, cos, sin)


def _kv_heads_kernel(lat_ref, kr_ref, wk_ref, wv_ref, k_ref, v_ref):
    lat = lat_ref[...]
    k_nope = _dot(lat, wk_ref[...]).astype(BF16)
    kr = kr_ref[...]
    tail = jnp.concatenate(
        [kr, jnp.zeros((kr.shape[0], MLA_QK_DIM - NOPE_DIM - ROPE_DIM), BF16)], axis=-1)
    for h in range(MLA_HEADS):
        k_ref[:, h * MLA_QK_DIM:h * MLA_QK_DIM + NOPE_DIM] = k_nope[:, h * NOPE_DIM:(h + 1) * NOPE_DIM]
        k_ref[:, h * MLA_QK_DIM + NOPE_DIM:(h + 1) * MLA_QK_DIM] = tail
    v_ref[...] = _dot(lat, wv_ref[...]).astype(BF16)


def _kv_heads(lat_b, kr_b, w_uk, w_uv, tm, tp):
    c = lat_b.shape[1]
    n = w_uk.shape[1]
    rows = lambda w: pl.BlockSpec((tm, w), lambda i: (i, 0))
    kw = MLA_HEADS * MLA_QK_DIM
    return pl.pallas_call(
        _kv_heads_kernel,
        grid=(tp // tm,),
        in_specs=[rows(c), rows(ROPE_DIM), _resident((c, n)), _resident((c, n))],
        out_specs=[rows(kw), rows(n)],
        out_shape=[jax.ShapeDtypeStruct((tp, kw), BF16), jax.ShapeDtypeStruct((tp, n), BF16)],
        compiler_params=_cparams(("parallel",)),
        name="kv_heads",
    )(lat_b, kr_b, w_uk, w_uv)


def _mla_init(m_ref, l_ref, acc_ref):
    m_ref[...] = jnp.full(m_ref.shape, NEG_BIG, F32)
    l_ref[...] = jnp.zeros(l_ref.shape, F32)
    acc_ref[...] = jnp.zeros(acc_ref.shape, F32)


def _mla_softmax(masked, q_pos0, k_pos0, m_ref, l_ref, s_ref, p_ref, a_ref, tq, tk):
    rows = MLA_HEADS * tq
    if masked:
        k_chunk = (k_pos0 + lax.broadcasted_iota(jnp.int32, (tk, 1), 0)) // CHUNK
    for c in range(rows // MLA_SOFTMAX_COLS):
        cols = slice(c * MLA_SOFTMAX_COLS, (c + 1) * MLA_SOFTMAX_COLS)
        s = s_ref[0:tk, cols]
        if masked:
            lane = c * MLA_SOFTMAX_COLS + lax.broadcasted_iota(jnp.int32, (1, MLA_SOFTMAX_COLS), 1)
            q_chunk = (q_pos0 + lax.rem(lane, tq)) // CHUNK
            s = jnp.where(k_chunk <= q_chunk, s, NEG_BIG)
        m_prev = m_ref[:, cols]
        m_new = jnp.maximum(m_prev, jnp.max(s, axis=0, keepdims=True))
        alpha = jnp.exp2(m_prev - m_new)
        p = jnp.exp2(s - m_new)
        l_ref[:, cols] = alpha * l_ref[:, cols] + jnp.sum(p, axis=0, keepdims=True)
        m_ref[:, cols] = m_new
        a_ref[:, cols] = alpha
        p_ref[0:tk, cols] = p.astype(BF16)


def _dot_tn(a, b):
    return lax.dot_general(a, b, (((0,), (0,)), ((), ())), preferred_element_type=F32)


def _mla_latent_tile(ql_ref, qr_ref, klat, kr, masked, q_pos0, k_pos0,
                     m_ref, l_ref, acc_ref, s_ref, p_ref, a_ref, tq, tk):
    rows = MLA_HEADS * tq
    q_lat = ql_ref[...].reshape(rows, ql_ref.shape[-1])
    q_rope = qr_ref[...].reshape(rows, qr_ref.shape[-1])
    s_ref[...] = _dot_nt(klat, q_lat) + _dot_nt(kr, q_rope)
    _mla_softmax(masked, q_pos0, k_pos0, m_ref, l_ref, s_ref, p_ref, a_ref, tq, tk)
    acc_ref[...] = a_ref[...] * acc_ref[...] + _dot_tn(klat, p_ref[...])


def _mla_latent_finalize(wuv_ref, o_ref, l_ref, acc_ref, tq):
    inv = 1.0 / l_ref[...]
    for h in range(MLA_HEADS):
        cols = slice(h * tq, (h + 1) * tq)
        ctx_t = (acc_ref[:, cols] * inv[:, cols]).astype(BF16)
        o_ref[:, h * V_DIM:(h + 1) * V_DIM] = _dot_tn(ctx_t, wuv_ref[h]).astype(BF16)


def _mla_heads_tile(qt_ref, k_ref, v_ref, masked, q_pos0, k_pos0,
                    m_ref, l_ref, acc_ref, s_ref, p_ref, a_ref, tq, tk):
    for h in range(MLA_HEADS):
        cols = slice(h * tq, (h + 1) * tq)
        s_ref[0:tk, cols] = _dot(k_ref[0:tk, h * MLA_QK_DIM:(h + 1) * MLA_QK_DIM], qt_ref[h])
    _mla_softmax(masked, q_pos0, k_pos0, m_ref, l_ref, s_ref, p_ref, a_ref, tq, tk)
    for h in range(MLA_HEADS):
        cols = slice(h * tq, (h + 1) * tq)
        pv = _dot_tn(v_ref[0:tk, h * V_DIM:(h + 1) * V_DIM], p_ref[0:tk, cols])
        acc_ref[h] = a_ref[:, cols] * acc_ref[h] + pv


def _mla_heads_finalize(o_ref, l_ref, acc_ref, tq):
    inv = 1.0 / l_ref[...]
    for h in range(MLA_HEADS):
        cols = slice(h * tq, (h + 1) * tq)
        o_ref[:, h * V_DIM:(h + 1) * V_DIM] = (acc_ref[h] * inv[:, cols]).T.astype(BF16)


def _mla_prompt_kernel(qb_ref, kb_ref, qp_ref, kp_ref, fl_ref,
                       qt_ref, k_ref, v_ref, o_ref,
                       m_ref, l_ref, acc_ref, s_ref, p_ref, a_ref, *, tq, tk):
    n = pl.program_id(0)
    scratch = (m_ref, l_ref, acc_ref, s_ref, p_ref, a_ref)

    @pl.when(fl_ref[n] % 2 == 1)
    def _():
        _mla_init(m_ref, l_ref, acc_ref)

    @pl.when(fl_ref[n] < 2)
    def _():
        _mla_heads_tile(qt_ref, k_ref, v_ref, False, 0, 0, *scratch, tq, tk)

    @pl.when(jnp.logical_and(fl_ref[n] >= 2, fl_ref[n] < 4))
    def _():
        _mla_heads_tile(qt_ref, k_ref, v_ref, True, qp_ref[n], kp_ref[n], *scratch, tq, tk)
        _mla_heads_finalize(o_ref, l_ref, acc_ref, tq)

    @pl.when(fl_ref[n] >= 4)
    def _():
        _mla_heads_tile(qt_ref, k_ref, v_ref, True, qp_ref[n], kp_ref[n], *scratch, tq, tk // 2)
        _mla_heads_finalize(o_ref, l_ref, acc_ref, tq)


def _mla_prompt(q_t, k_heads, v, batch, seq):
    _, h, _, tq = q_t.shape
    tk = _pick_tile(seq, MLA_K_TILE, CHUNK)
    nq, nk = seq // tq, seq // tk
    qb, kb, qp, kp, fl = [], [], [], [], []
    for b in range(batch):
        for qi in range(nq):
            last = ((qi + 1) * tq - 1) // tk
            for kj in range(last + 1):
                qb.append(b * nq + qi)
                kb.append(b * nk + kj)
                qp.append(qi * tq)
                kp.append(kj * tk)
                half = kj == last and (qi + 1) * tq - last * tk <= tk // 2
                fl.append((1 if kj == 0 else 0) + (2 if kj == last else 0) + (4 if half else 0))
    tabs = [jnp.asarray(np.asarray(a, np.int32)) for a in (qb, kb, qp, kp, fl)]
    rows = h * tq
    grid_spec = pltpu.PrefetchScalarGridSpec(
        num_scalar_prefetch=5,
        grid=(len(qb),),
        in_specs=[
            pl.BlockSpec((None, h, MLA_QK_DIM, tq),
                         lambda n, qb, kb, qp, kp, fl: (qb[n], 0, 0, 0)),
            pl.BlockSpec((tk, h * MLA_QK_DIM), lambda n, qb, kb, qp, kp, fl: (kb[n], 0)),
            pl.BlockSpec((tk, h * V_DIM), lambda n, qb, kb, qp, kp, fl: (kb[n], 0)),
        ],
        out_specs=pl.BlockSpec((tq, h * V_DIM), lambda n, qb, kb, qp, kp, fl: (qb[n], 0)),
        scratch_shapes=[pltpu.VMEM((1, rows), F32), pltpu.VMEM((1, rows), F32),
                        pltpu.VMEM((h, V_DIM, tq), F32), pltpu.VMEM((tk, rows), F32),
                        pltpu.VMEM((tk, rows), BF16), pltpu.VMEM((1, rows), F32)],
    )
    return pl.pallas_call(
        functools.partial(_mla_prompt_kernel, tq=tq, tk=tk),
        grid_spec=grid_spec,
        out_shape=jax.ShapeDtypeStruct((batch * seq, h * V_DIM), BF16),
        compiler_params=_cparams(("arbitrary",)),
        name="mla_prompt",
    )(*tabs, q_t, k_heads, v)


def _mla_sample_kernel(ql_ref, qr_ref, pl_ref, pr_ref, nl_ref, nr_ref, wuv_ref, o_ref,
                       m_ref, l_ref, acc_ref, a_ref, s_ref, p_ref, sn_ref, pn_ref,
                       *, tq, tk, n_past, n_kt):
    j = pl.program_id(1)

    @pl.when(j == 0)
    def _():
        _mla_init(m_ref, l_ref, acc_ref)

    @pl.when(j < n_kt)
    def _():
        _mla_latent_tile(ql_ref, qr_ref, pl_ref[...].astype(BF16), pr_ref[...].astype(BF16),
                         False, 0, 0, m_ref, l_ref, acc_ref, s_ref, p_ref, a_ref, tq, tk)

    @pl.when(j == n_kt)
    def _():
        _mla_latent_tile(ql_ref, qr_ref, nl_ref[...], nr_ref[...], True, n_past, n_past,
                         m_ref, l_ref, acc_ref, sn_ref, pn_ref, a_ref, tq, tq)
        _mla_latent_finalize(wuv_ref, o_ref, l_ref, acc_ref, tq)


def _mla_sample(q_lat, q_rope, past_lat, past_kr, layer, k_lat, k_rope, w_uv,
                row0, dec_batch, dec_seq):
    h, _, c = q_lat.shape
    n_past = past_lat.shape[2]
    tk = _pick_tile(n_past, MLA_K_TILE, 128)
    n_kt = n_past // tk
    blk0 = row0 // dec_seq
    rows = h * dec_seq
    past_idx = lambda b, j: (layer, b, jnp.minimum(j, n_kt - 1), 0)
    return pl.pallas_call(
        functools.partial(_mla_sample_kernel, tq=dec_seq, tk=tk, n_past=n_past, n_kt=n_kt),
        grid=(dec_batch, n_kt + 1),
        in_specs=[
            pl.BlockSpec((h, dec_seq, c), lambda b, j: (0, b, 0)),
            pl.BlockSpec((h, dec_seq, ROPE_DIM), lambda b, j: (0, b, 0)),
            pl.BlockSpec((None, None, tk, c), past_idx),
            pl.BlockSpec((None, None, tk, ROPE_DIM), past_idx),
            pl.BlockSpec((dec_seq, c), lambda b, j: (blk0 + b, 0)),
            pl.BlockSpec((dec_seq, ROPE_DIM), lambda b, j: (blk0 + b, 0)),
            pl.BlockSpec((h, c, V_DIM), lambda b, j: (0, 0, 0)),
        ],
        out_specs=pl.BlockSpec((dec_seq, h * V_DIM), lambda b, j: (b, 0)),
        out_shape=jax.ShapeDtypeStruct((dec_batch * dec_seq, h * V_DIM), BF16),
        scratch_shapes=[pltpu.VMEM((1, rows), F32), pltpu.VMEM((1, rows), F32),
                        pltpu.VMEM((c, rows), F32), pltpu.VMEM((1, rows), F32),
                        pltpu.VMEM((tk, rows), F32), pltpu.VMEM((tk, rows), BF16),
                        pltpu.VMEM((dec_seq, rows), F32), pltpu.VMEM((dec_seq, rows), BF16)],
        compiler_params=_cparams(("parallel", "arbitrary")),
        name="mla_sample",
    )(q_lat, q_rope, past_lat, past_kr, k_lat, k_rope, w_uv)


def _scan_matrix(sub):
    j = np.arange(sub)[:, None]
    s = np.arange(sub)[None, :]
    u = (j > s).astype(np.float32)
    return jnp.asarray(np.concatenate([u, u], axis=0), dtype=BF16)


_SbChunk = collections.namedtuple("_SbChunk", "q k v before carry_idx acc_idx")


def _sb_scores(qc, kc, before):
    z = _dot_nt(qc, kc)
    softplus = jnp.log(1.0 + jnp.exp2(-jnp.abs(z))) * LOG2E
    log_beta = jnp.minimum(z, 0.0) - softplus
    log_rest = log_beta - z
    if before is not None:
        log_rest = jnp.where(before, log_rest, 0.0)
    hi = log_rest.astype(BF16)
    lo = (log_rest - hi.astype(F32)).astype(BF16)
    return (log_beta, jnp.concatenate([hi, lo], axis=-1),
            jnp.sum(log_rest, axis=-1, keepdims=True))


def _sb_weights(log_beta, hi_lo, u2, carry, before):
    a = jnp.exp2(log_beta + _dot(hi_lo, u2) + carry)
    if before is not None:
        a = jnp.where(before, a, 0.0)
    return a.astype(BF16)


def _sb_pipeline(chunks, u2, acc_ref, carry_ref):
    scored = None
    weighted = None
    for j in range(len(chunks) + 2):
        new_scored = None
        if j < len(chunks):
            ch = chunks[j]
            new_scored = (ch,) + _sb_scores(ch.q(), ch.k(), ch.before)
        new_weighted = None
        if scored is not None:
            ch, log_beta, hi_lo, row_sum = scored
            carry = carry_ref[ch.carry_idx]
            new_weighted = (ch, _sb_weights(log_beta, hi_lo, u2, carry, ch.before))
            carry_ref[ch.carry_idx] = carry + row_sum
        if weighted is not None:
            ch, a = weighted
            acc_ref[ch.acc_idx] += _dot(a, ch.v())
        scored, weighted = new_scored, new_weighted


def _sb_tile(q_ref, k_ref, v_ref, u2, acc_ref, carry_ref, diagonal, tq, tk, sub, rq):
    chunks = []
    for c in reversed(range(tk // sub)):
        k0, k1 = c * sub, (c + 1) * sub
        for r in range(tq // rq):
            r0, r1 = r * rq, (r + 1) * rq
            before = None
            if diagonal:
                if k0 >= r1 - 1:
                    continue
                if k1 > r0:
                    before = ((k0 + lax.broadcasted_iota(jnp.int32, (1, sub), 1))
                              < (r0 + lax.broadcasted_iota(jnp.int32, (rq, 1), 0)))
            for h in range(SB_HEADS):
                cols = slice(h * SB_DIM, (h + 1) * SB_DIM)
                rows, keys = slice(r0, r1), slice(k0, k1)
                chunks.append(_SbChunk(
                    q=functools.partial(lambda rr, cc: q_ref[rr, cc], rows, cols),
                    k=functools.partial(lambda kk, cc: k_ref[kk, cc], keys, cols),
                    v=functools.partial(lambda kk, cc: v_ref[kk, cc], keys, cols),
                    before=before, carry_idx=(h, rows, slice(None)), acc_idx=(rows, cols)))
    _sb_pipeline(chunks, u2, acc_ref, carry_ref)


def _sb_prompt_kernel(q_ref, k_ref, v_ref, u_ref, k_hbm, v_hbm, o_ref,
                      acc_ref, carry_ref, kbuf, vbuf, sem, *, nq, tq, sub, rq):
    g = pl.program_id(0)
    n_old = lax.rem(g, nq) * (tq // sub)

    def tile_copies(i, slot):
        start = g * tq - (i + 1) * sub
        return (pltpu.make_async_copy(k_hbm.at[pl.ds(start, sub), :], kbuf.at[slot], sem.at[0, slot]),
                pltpu.make_async_copy(v_hbm.at[pl.ds(start, sub), :], vbuf.at[slot], sem.at[1, slot]))

    def start_tile(i, slot):
        for copy in tile_copies(i, slot):
            copy.start()

    def wait_tile(i, slot):
        for copy in tile_copies(i, slot):
            copy.wait()

    def any_alive():
        return (jnp.max(carry_ref[...]) > SB_DEAD_LOG2).astype(jnp.int32)

    @pl.when(n_old > 0)
    def _():
        start_tile(0, 0)

    acc_ref[...] = jnp.zeros(acc_ref.shape, F32)
    carry_ref[...] = jnp.zeros(carry_ref.shape, F32)
    _sb_tile(q_ref, k_ref, v_ref, u_ref[...], acc_ref, carry_ref, True, tq, tq, sub, rq)

    def tile_step(state):
        i, _ = state
        slot = lax.rem(i, 2)
        wait_tile(i, slot)

        @pl.when(i + 1 < n_old)
        def _():
            start_tile(i + 1, 1 - slot)

        for r in range(tq // rq):
            rows = slice(r * rq, (r + 1) * rq)

            @pl.when(jnp.max(carry_ref[:, rows, :]) > SB_DEAD_LOG2)
            def _(rows=rows):
                chunks = []
                for h in range(SB_HEADS):
                    cols = slice(h * SB_DIM, (h + 1) * SB_DIM)
                    chunks.append(_SbChunk(
                        q=functools.partial(lambda rr, cc: q_ref[rr, cc], rows, cols),
                        k=functools.partial(lambda cc: kbuf[slot, :, cc], cols),
                        v=functools.partial(lambda cc: vbuf[slot, :, cc], cols),
                        before=None, carry_idx=(h, rows, slice(None)), acc_idx=(rows, cols)))
                _sb_pipeline(chunks, u_ref[...], acc_ref, carry_ref)
        return i + 1, any_alive()

    tiles_done, _ = lax.while_loop(
        lambda state: jnp.logical_and(state[0] < n_old, state[1] == 1),
        tile_step, (jnp.int32(0), any_alive()))

    @pl.when(tiles_done < n_old)
    def _():
        wait_tile(tiles_done, lax.rem(tiles_done, 2))

    o_ref[...] = acc_ref[...].astype(BF16)


def _sb_prompt(q, k, v, batch, seq):
    width = q.shape[1]
    tq = _pick_tile(seq, SB_TILE, 128)
    sub = _pick_tile(tq, SB_SUB, 128)
    rq = _pick_tile(tq, SB_ROWS, 16)
    nq = seq // tq
    rows = pl.BlockSpec((tq, width), lambda g: (g, 0))
    in_hbm = pl.BlockSpec(memory_space=pl.ANY)
    return pl.pallas_call(
        functools.partial(_sb_prompt_kernel, nq=nq, tq=tq, sub=sub, rq=rq),
        grid=(batch * nq,),
        in_specs=[rows, rows, rows, pl.BlockSpec((2 * sub, sub), lambda g: (0, 0)),
                  in_hbm, in_hbm],
        out_specs=rows,
        out_shape=jax.ShapeDtypeStruct((batch * seq, width), BF16),
        scratch_shapes=[pltpu.VMEM((tq, width), F32), pltpu.VMEM((SB_HEADS, tq, 1), F32),
                        pltpu.VMEM((2, sub, width), BF16), pltpu.VMEM((2, sub, width), BF16),
                        pltpu.SemaphoreType.DMA((2, 2))],
        compiler_params=_cparams(("arbitrary",)),
        name="sb_prompt",
    )(q, k, v, _scan_matrix(sub), k, v)


def _sb_sample_kernel(q_ref, nk_ref, nv_ref, un_ref, up_ref, pk_hbm, pv_hbm, o_ref,
                      acc_ref, carry_ref, kbuf, vbuf, sem, *, layer, tq, tk, n_kt):
    b = pl.program_id(0)

    def tile_copies(i, slot):
        start = (n_kt - 1 - i) * tk
        copies = []
        for h in range(SB_HEADS):
            copies.append(pltpu.make_async_copy(
                pk_hbm.at[layer, b, pl.ds(start, tk), h, :], kbuf.at[slot, h], sem.at[0, slot]))
            copies.append(pltpu.make_async_copy(
                pv_hbm.at[layer, b, pl.ds(start, tk), h, :], vbuf.at[slot, h], sem.at[1, slot]))
        return copies

    def start_tile(i, slot):
        for copy in tile_copies(i, slot):
            copy.start()

    def wait_tile(i, slot):
        for copy in tile_copies(i, slot):
            copy.wait()

    def alive():
        return (jnp.max(carry_ref[...]) > SB_DEAD_LOG2).astype(jnp.int32)

    start_tile(0, 0)

    acc_ref[...] = jnp.zeros(acc_ref.shape, F32)
    carry_ref[...] = jnp.zeros(carry_ref.shape, F32)
    before = (lax.broadcasted_iota(jnp.int32, (1, tq), 1)
              < lax.broadcasted_iota(jnp.int32, (tq, 1), 0))
    chunks = []
    for h in range(SB_HEADS):
        cols = slice(h * SB_DIM, (h + 1) * SB_DIM)
        chunks.append(_SbChunk(
            q=functools.partial(lambda cc: q_ref[:, cc], cols),
            k=functools.partial(lambda cc: nk_ref[:, cc], cols),
            v=functools.partial(lambda cc: nv_ref[:, cc], cols),
            before=before, carry_idx=(h, slice(None), slice(None)), acc_idx=(slice(None), cols)))
    _sb_pipeline(chunks, un_ref[...], acc_ref, carry_ref)

    def tile_step(state):
        i, _ = state
        slot = lax.rem(i, 2)
        wait_tile(i, slot)

        @pl.when(i + 1 < n_kt)
        def _():
            start_tile(i + 1, 1 - slot)

        chunks = []
        for h in range(SB_HEADS):
            cols = slice(h * SB_DIM, (h + 1) * SB_DIM)
            chunks.append(_SbChunk(
                q=functools.partial(lambda cc: q_ref[:, cc], cols),
                k=functools.partial(lambda hh: kbuf[slot, hh].astype(BF16), h),
                v=functools.partial(lambda hh: vbuf[slot, hh].astype(BF16), h),
                before=None, carry_idx=(h, slice(None), slice(None)), acc_idx=(slice(None), cols)))
        _sb_pipeline(chunks, up_ref[...], acc_ref, carry_ref)
        return i + 1, alive()

    tiles_done, _ = lax.while_loop(
        lambda state: jnp.logical_and(state[0] < n_kt, state[1] == 1),
        tile_step, (jnp.int32(0), alive()))

    @pl.when(tiles_done < n_kt)
    def _():
        wait_tile(tiles_done, lax.rem(tiles_done, 2))

    o_ref[...] = acc_ref[...].astype(BF16)


def _sb_sample(q, past_k, past_v, layer, k, v, row0, dec_batch, dec_seq):
    width = q.shape[1]
    n_past = past_k.shape[2]
    tk = _pick_tile(n_past, SB_SUB, 128)
    n_kt = n_past // tk
    blk0 = row0 // dec_seq
    new_rows = pl.BlockSpec((dec_seq, width), lambda b: (blk0 + b, 0))
    in_hbm = pl.BlockSpec(memory_space=pl.ANY)
    return pl.pallas_call(
        functools.partial(_sb_sample_kernel, layer=layer, tq=dec_seq, tk=tk, n_kt=n_kt),
        grid=(dec_batch,),
        in_specs=[
            new_rows, new_rows, new_rows,
            pl.BlockSpec((2 * dec_seq, dec_seq), lambda b: (0, 0)),
            pl.BlockSpec((2 * tk, tk), lambda b: (0, 0)),
            in_hbm, in_hbm,
        ],
        out_specs=pl.BlockSpec((dec_seq, width), lambda b: (b, 0)),
        out_shape=jax.ShapeDtypeStruct((dec_batch * dec_seq, width), BF16),
        scratch_shapes=[pltpu.VMEM((dec_seq, width), F32),
                        pltpu.VMEM((SB_HEADS, dec_seq, 1), F32),
                        pltpu.VMEM((2, SB_HEADS, tk, SB_DIM), F32),
                        pltpu.VMEM((2, SB_HEADS, tk, SB_DIM), F32),
                        pltpu.SemaphoreType.DMA((2, 2))],
        compiler_params=_cparams(("arbitrary",)),
        name="sb_sample",
    )(q, k, v, _scan_matrix(dec_seq), _scan_matrix(tk), past_k, past_v)


def _merge_kernel(oap_ref, oas_ref, obp_ref, obs_ref, wa_ref, wb_ref, g_ref, o_ref,
                  *, n_p, tn):
    i = pl.program_id(0)
    o_a = _load_split(i, n_p, oap_ref, oas_ref)
    o_b = _load_split(i, n_p, obp_ref, obs_ref)
    n = o_ref.shape[1]
    for c in range(n // tn):
        cols = slice(c * tn, (c + 1) * tn)
        a = _dot(o_a, wa_ref[:, cols])
        b = _dot(o_b, wb_ref[:, cols])
        g_a = g_ref[:, cols].astype(F32)
        g_b = g_ref[:, n + c * tn:n + (c + 1) * tn].astype(F32)
        o_ref[:, cols] = (g_a * a + g_b * b).astype(BF16)


def _merge(oa_p, oa_s, ob_p, ob_s, w_a, w_b, gates, tm):
    da = oa_p.shape[1]
    db = ob_p.shape[1]
    t = oa_p.shape[0] + oa_s.shape[0]
    n_p = oa_p.shape[0] // tm
    n = w_a.shape[1]
    tn = _pick_tile(n, COL_TILE, 128)
    return pl.pallas_call(
        functools.partial(_merge_kernel, n_p=n_p, tn=tn),
        grid=(t // tm,),
        in_specs=[*_split_specs(tm, da, n_p),
                  *_split_specs(tm, db, n_p),
                  _resident((da, n)), _resident((db, n)),
                  pl.BlockSpec((tm, 2 * n), lambda i: (i, 0))],
        out_specs=pl.BlockSpec((tm, n), lambda i: (i, 0)),
        out_shape=jax.ShapeDtypeStruct((t, n), BF16),
        compiler_params=_cparams(("parallel",)),
        name="merge",
    )(oa_p, oa_s, ob_p, ob_s, w_a, w_b, gates)


def _res_ln_finalize(z_ref, g_ref, b_ref, o_ref, ob_ref):
    n_tiles, _, tn = z_ref.shape
    width = n_tiles * tn
    total = z_ref[0].sum(axis=-1, keepdims=True)
    for c in range(1, n_tiles):
        total += z_ref[c].sum(axis=-1, keepdims=True)
    mu = total / width
    sq = jnp.square(z_ref[0] - mu).sum(axis=-1, keepdims=True)
    for c in range(1, n_tiles):
        sq += jnp.square(z_ref[c] - mu).sum(axis=-1, keepdims=True)
    rstd = lax.rsqrt(sq / width + LN_EPS)
    for c in range(n_tiles):
        cols = slice(c * tn, (c + 1) * tn)
        y = (z_ref[c] - mu) * rstd * g_ref[:, cols] + b_ref[:, cols]
        o_ref[:, cols] = y
        if ob_ref is not None:
            ob_ref[:, cols] = y.astype(BF16)


def _out_ln_kernel(m_ref, w_ref, x_ref, g_ref, b_ref, o_ref, ob_ref, z_ref, *, alpha):
    m = m_ref[...]
    n_tiles, _, tn = z_ref.shape
    for c in range(n_tiles):
        cols = slice(c * tn, (c + 1) * tn)
        z_ref[c] = alpha * x_ref[:, cols] + _dot(m, w_ref[:, cols])
    _res_ln_finalize(z_ref, g_ref, b_ref, o_ref, ob_ref)


def _out_ln(merged, w, x, g, b, alpha, tm):
    t, d = x.shape
    k = merged.shape[1]
    tn = _pick_tile(d, COL_TILE, 128)
    full_row = pl.BlockSpec((tm, d), lambda i: (i, 0))
    vec = pl.BlockSpec((1, d), lambda i: (0, 0))
    return pl.pallas_call(
        functools.partial(_out_ln_kernel, alpha=alpha),
        grid=(t // tm,),
        in_specs=[pl.BlockSpec((tm, k), lambda i: (i, 0)), _resident((k, d)), full_row, vec, vec],
        out_specs=[full_row, full_row],
        out_shape=[jax.ShapeDtypeStruct((t, d), F32), jax.ShapeDtypeStruct((t, d), BF16)],
        scratch_shapes=[pltpu.VMEM((d // tn, tm, tn), F32)],
        compiler_params=_cparams(("parallel",)),
        name="out_ln",
    )(merged, w, x, g.reshape(1, d), b.reshape(1, d))


def _ffn_ln_kernel(xb_ref, wg_ref, wu_ref, wd_ref, x_ref, g_ref, b_ref, o_ref, ob_ref,
                   acc_ref, *, alpha):
    f = pl.program_id(1)

    @pl.when(f == 0)
    def _():
        acc_ref[...] = jnp.zeros(acc_ref.shape, F32)

    xb = xb_ref[...]
    gate = _dot(xb, wg_ref[...])
    up = _dot(xb, wu_ref[...])
    hidden = (gate * _sigmoid(gate) * up).astype(BF16)
    acc_ref[...] += _dot(hidden, wd_ref[...])

    @pl.when(f == pl.num_programs(1) - 1)
    def _():
        y = _layer_norm_rows(alpha * x_ref[...] + acc_ref[...], g_ref[...], b_ref[...])
        o_ref[...] = y
        ob_ref[...] = y.astype(BF16)


def _ffn_ln(xb, w_gu, w_down, x, g, b, alpha, tm):
    t, d = x.shape
    ff = w_down.shape[0]
    tf = _pick_tile(ff, COL_TILE, 128)
    nf = ff // tf
    full_row = pl.BlockSpec((tm, d), lambda i, f: (i, 0))
    vec = pl.BlockSpec((1, d), lambda i, f: (0, 0))
    return pl.pallas_call(
        functools.partial(_ffn_ln_kernel, alpha=alpha),
        grid=(t // tm, nf),
        in_specs=[full_row,
                  pl.BlockSpec((d, tf), lambda i, f: (0, f)),
                  pl.BlockSpec((d, tf), lambda i, f: (0, f + nf)),
                  pl.BlockSpec((tf, d), lambda i, f: (f, 0)),
                  full_row, vec, vec],
        out_specs=[full_row, full_row],
        out_shape=[jax.ShapeDtypeStruct((t, d), F32), jax.ShapeDtypeStruct((t, d), BF16)],
        scratch_shapes=[pltpu.VMEM((tm, d), F32)],
        compiler_params=_cparams(("parallel", "arbitrary")),
        name="ffn_ln",
    )(xb, w_gu, w_gu, w_down, x, g.reshape(1, d), b.reshape(1, d))


def _ple_ln_kernel(xb_ref, wg_ref, bg_ref, pp_ref, ps_ref, wp_ref, x_ref, g_ref, b_ref,
                   o1_ref, o2_ref, z_ref, *, alpha, n_p, last):
    i = pl.program_id(0)
    xb = xb_ref[...]
    p = _load_split(i, n_p, pp_ref, ps_ref).astype(BF16)
    n_tiles, _, tn = z_ref.shape
    for c in range(n_tiles):
        cols = slice(c * tn, (c + 1) * tn)
        gate = _sigmoid(_dot(xb, wg_ref[:, cols]) + bg_ref[:, cols])
        z_ref[c] = alpha * x_ref[:, cols] + gate * _dot(p, wp_ref[:, cols])

    if not last:
        _res_ln_finalize(z_ref, g_ref, b_ref, o1_ref, o2_ref)
    else:
        @pl.when(i < n_p)
        def _():
            _res_ln_finalize(z_ref, g_ref, b_ref, o1_ref, None)

        @pl.when(i >= n_p)
        def _():
            _res_ln_finalize(z_ref, g_ref, b_ref, o2_ref, None)


def _ple_ln(xb, w_gate, b_gate, p_p, p_s, layer, w_proj, x, g, b, alpha, tm, last):
    t, d = x.shape
    pd = p_p.shape[2]
    tp = p_p.shape[1]
    n_p = tp // tm
    tn = _pick_tile(d, COL_TILE, 128)
    full_row = pl.BlockSpec((tm, d), lambda i: (i, 0))
    vec = pl.BlockSpec((1, d), lambda i: (0, 0))
    if last:
        out_specs = list(_split_specs(tm, d, n_p))
        out_shape = [jax.ShapeDtypeStruct((tp, d), F32), jax.ShapeDtypeStruct((t - tp, d), F32)]
    else:
        out_specs = [full_row, full_row]
        out_shape = [jax.ShapeDtypeStruct((t, d), F32), jax.ShapeDtypeStruct((t, d), BF16)]
    return pl.pallas_call(
        functools.partial(_ple_ln_kernel, alpha=alpha, n_p=n_p, last=last),
        grid=(t // tm,),
        in_specs=[full_row, _resident((d, d)), vec,
                  pl.BlockSpec((None, tm, pd), lambda i: (layer, jnp.minimum(i, n_p - 1), 0)),
                  pl.BlockSpec((None, tm, pd), lambda i: (layer, jnp.maximum(i - n_p, 0), 0)),
                  _resident((pd, d)), full_row, vec, vec],
        out_specs=out_specs,
        out_shape=out_shape,
        scratch_shapes=[pltpu.VMEM((d // tn, tm, tn), F32)],
        compiler_params=_cparams(("arbitrary",)),
        name="ple_ln",
    )(xb, w_gate, b_gate.reshape(1, d), p_p, p_s, w_proj, x, g.reshape(1, d), b.reshape(1, d))


def _cast_kernel(w_ref, *o_refs, bounds):
    w = w_ref[...]
    for o_ref, (lo, hi) in zip(o_refs, bounds):
        o_ref[...] = w[:, lo:hi].astype(BF16)


def _cast_weight(w, layer, col_groups=None):
    _, r, n = w.shape
    bounds = tuple(col_groups) if col_groups else ((0, n),)
    br = _pick_tile(r, max(16, CAST_BLOCK_BYTES // (4 * n)), 16)
    return pl.pallas_call(
        functools.partial(_cast_kernel, bounds=bounds),
        grid=(r // br,),
        in_specs=[pl.BlockSpec((None, br, n), lambda i: (layer, i, 0))],
        out_specs=[pl.BlockSpec((br, hi - lo), lambda i: (i, 0)) for lo, hi in bounds],
        out_shape=[jax.ShapeDtypeStruct((r, hi - lo), BF16) for lo, hi in bounds],
        compiler_params=_cparams(("parallel",)),
        name="cast_weight",
    )(w)


def _rope_tables(pos):
    half = ROPE_DIM // 2
    inv_freq = 1.0 / (ROPE_THETA ** (jnp.arange(half, dtype=F32) * (2.0 / ROPE_DIM)))
    ang = pos.astype(F32)[:, None] * inv_freq[None, :]
    return jnp.cos(ang), jnp.sin(ang)


def kernel(x_prompt, x_sample, cache_mla_latent, cache_mla_krope, cache_sb_k, cache_sb_v,
           p_prompt, p_sample, ln_in_g, ln_in_b, w_in, b_gate, q_a_norm_g, w_q_b,
           kv_a_norm_g, w_kv_b, w_branch_a, w_branch_b, w_out, ln1_g, ln1_b,
           w_ffn_gu, w_ffn_down, ln2_g, ln2_b, w_ple_gate, b_ple_gate, w_ple_proj,
           ln3_g, ln3_b):
    batch, seq, d = x_prompt.shape
    dec_batch, dec_seq, _ = x_sample.shape
    depth = w_in.shape[0]
    n_past = cache_mla_latent.shape[2]
    tp = batch * seq
    ts = dec_batch * dec_seq
    sb_width = SB_HEADS * SB_DIM
    alpha = (2 * depth) ** 0.25

    pos = jnp.concatenate([jnp.tile(jnp.arange(seq, dtype=jnp.int32), batch),
                           jnp.tile(n_past + jnp.arange(dec_seq, dtype=jnp.int32), dec_batch)])
    cos, sin = _rope_tables(pos)

    tm = _pick_tile(math.gcd(tp, ts), ROW_TILE, 16)
    p_p = p_prompt.reshape(depth, tp, -1)
    p_s = p_sample.reshape(depth, ts, -1)

    x, xb = _ln_in(x_prompt.reshape(tp, d), x_sample.reshape(ts, d), ln_in_g, ln_in_b, tm)

    splits = np.cumsum([0, Q_LORA, KV_LORA, ROPE_DIM, sb_width, sb_width, sb_width, 2 * d])
    def cache_arrays(widths):
        return [jnp.zeros((depth, n, w), F32) for w in widths for n in (tp, ts)]

    kv_cache = cache_arrays((KV_LORA, ROPE_DIM))
    sb_cache = cache_arrays((sb_width, sb_width))
    for l in range(depth):
        w_qa, w_lat, w_kr, w_sq, w_sk, w_sv, w_gates = _cast_weight(
            w_in, l, [(int(splits[i]), int(splits[i + 1])) for i in range(7)])
        (wb_a,) = _cast_weight(w_branch_a, l)
        (wb_b,) = _cast_weight(w_branch_b, l)
        (wb_out,) = _cast_weight(w_out, l)
        (wb_gu,) = _cast_weight(w_ffn_gu, l)
        (wb_down,) = _cast_weight(w_ffn_down, l)
        (wb_pg,) = _cast_weight(w_ple_gate, l)
        (wb_pp,) = _cast_weight(w_ple_proj, l)
        wq = w_q_b[l].reshape(Q_LORA, MLA_HEADS, NOPE_DIM + ROPE_DIM)
        wq_nope = wq[..., :NOPE_DIM].reshape(Q_LORA, MLA_HEADS * NOPE_DIM).astype(BF16)
        wq_rope = wq[..., NOPE_DIM:].reshape(Q_LORA, MLA_HEADS * ROPE_DIM).astype(BF16)
        w_kv = w_kv_b[l].reshape(KV_LORA, MLA_HEADS, NOPE_DIM + V_DIM)
        wuk_h = w_kv[..., :NOPE_DIM].transpose(1, 2, 0).astype(BF16)
        wuv_h = w_kv[..., NOPE_DIM:].transpose(1, 0, 2).astype(BF16)
        wuk_all = w_kv[..., :NOPE_DIM].reshape(KV_LORA, MLA_HEADS * NOPE_DIM).astype(BF16)
        wuv_all = w_kv[..., NOPE_DIM:].reshape(KV_LORA, MLA_HEADS * V_DIM).astype(BF16)

        qa_n = _proj_qa(xb, w_qa, q_a_norm_g[l], tm)
        *kv_cache, lat_b, kr_b = _proj_kv(xb, w_lat, w_kr, kv_a_norm_g[l], cos, sin,
                                          tm, tp, l, depth, kv_cache)
        *sb_cache, sbq, sbk_b, sbv_b = _proj_sb(xb, w_sq, w_sk, w_sv, tm, tp, l, depth, sb_cache)
        gates = _proj_gates(xb, w_gates, b_gate[l], tm)

        q_t, q_lat, q_rope = _q_heads(qa_n, wq_nope, wq_rope, wuk_h, cos, sin, tm, tp,
                                      _pick_tile(seq, MLA_Q_TILE, CHUNK))
        k_heads, v_heads = _kv_heads(lat_b, kr_b, wuk_all, wuv_all, tm, tp)
        oa_p = _mla_prompt(q_t, k_heads, v_heads, batch, seq)
        oa_s = _mla_sample(q_lat, q_rope, cache_mla_latent, cache_mla_krope, l, lat_b, kr_b,
                           wuv_h, tp, dec_batch, dec_seq)
        ob_p = _sb_prompt(sbq, sbk_b, sbv_b, batch, seq)
        ob_s = _sb_sample(sbq, cache_sb_k, cache_sb_v, l, sbk_b, sbv_b, tp, dec_batch, dec_seq)

        merged = _merge(oa_p, oa_s, ob_p, ob_s, wb_a, wb_b, gates, tm)
        x, xb = _out_ln(merged, wb_out, x, ln1_g[l], ln1_b[l], alpha, tm)
        x, xb = _ffn_ln(xb, wb_gu, wb_down, x, ln2_g[l], ln2_b[l], alpha, tm)
        x, xb = _ple_ln(xb, wb_pg, b_ple_gate[l], p_p, p_s, l, wb_pp, x, ln3_g[l], ln3_b[l],
                        alpha, tm, last=(l == depth - 1))

    lat_p, lat_s, kr_p, kr_s = kv_cache
    k_p, k_s, v_p, v_s = sb_cache
    head_dims = (SB_HEADS, SB_DIM)
    return (x.reshape(batch, seq, d), xb.reshape(dec_batch, dec_seq, d),
            lat_p.reshape(depth, batch, seq, KV_LORA), kr_p.reshape(depth, batch, seq, ROPE_DIM),
            k_p.reshape(depth, batch, seq, *head_dims), v_p.reshape(depth, batch, seq, *head_dims),
            lat_s.reshape(depth, dec_batch, dec_seq, KV_LORA),
            kr_s.reshape(depth, dec_batch, dec_seq, ROPE_DIM),
            k_s.reshape(depth, dec_batch, dec_seq, *head_dims),
            v_s.reshape(depth, dec_batch, dec_seq, *head_dims))
```
